```python
import jax
import jax.numpy as jnp
from jax import lax
import numpy as np

D_MODEL = 4096
BATCH = 1
SEQ = 8192
DEPTH = 4

GRID_W = 64
CTX_LEN = 256
N_MIXERS = 3
N_A = (DEPTH + 2) // 3
N_B = (DEPTH + 1) // 3
N_C = DEPTH // 3
N_MOD = 9
ADA_RANK = 256
D_FF = 6144
EPS = 1e-6

LRU_WIDTH = D_MODEL
LRU_HEADS = 16
LRU_BLOCK = LRU_WIDTH // LRU_HEADS
CONV_W = 4
LRU_C = 8.0

RWKV_HEAD = 64
RWKV_HEADS = D_MODEL // RWKV_HEAD
LORA_W = max(32, int(round(1.8 * D_MODEL ** 0.5 / 32)) * 32)
LORA_A = max(32, int(round(1.8 * D_MODEL ** 0.5 / 32)) * 32)
LORA_G = max(32, int(round(0.6 * D_MODEL ** 0.8 / 32)) * 32)
GN_EPS = 64e-5

HGRN_DK = 128
HGRN_HEADS = D_MODEL // HGRN_DK
HGRN_DV = D_MODEL // HGRN_HEADS
CHUNK = 64

kernel_name = 'hybrid_rglru_rwkv7_hgrn2_flow_backbone'


def rmsnorm(x, g):
    x32 = x.astype(jnp.float32)
    y = x32 * lax.rsqrt(jnp.mean(x32 * x32, axis=-1, keepdims=True) + EPS)
    return (y * g).astype(x.dtype)


def modulate(h, shift, scale):
    return h * (1 + scale) + shift


def ada_modulation(s, down, up, bias):
    m = (s @ down) @ up + bias
    m = m.reshape(m.shape[:-1] + (N_MOD, D_MODEL))
    return jnp.moveaxis(m, -2, 0).reshape(N_MOD, -1, 1, D_MODEL)


def swiglu(h, w13, w2):
    gate, up = jnp.split(h @ w13, 2, axis=-1)
    return (jax.nn.silu(gate) * up) @ w2


def half_ffn(x, m, k, g, w13, w2):
    h = modulate(rmsnorm(x, g), m[k], m[k + 1])
    return x + 0.5 * m[k + 2] * swiglu(h, w13, w2)


def flip(t, d):
    return t[:, ::-1] if d == 1 else t


def centred_dwconv(u, w, b):
    T = u.shape[1]
    left = CONV_W // 2
    up = jnp.pad(u, ((0, 0), (left, CONV_W - 1 - left), (0, 0)))
    out = b
    for j in range(CONV_W):
        out = out + up[:, j:j + T] * w[j]
    return out


def linear_scan(a, b, h0):
    b = b.at[:, 0].add(a[:, 0] * h0)

    def combine(lhs, rhs):
        a_l, b_l = lhs
        a_r, b_r = rhs
        return a_l * a_r, a_r * b_l + b_r

    return lax.associative_scan(combine, (a, b), axis=1)[1]


def rglru_coeffs(v, wa, ba, wx, bx, lam):
    B, T, C = v.shape
    v32 = v.astype(jnp.float32)
    vb = v32.reshape(B, T, LRU_HEADS, LRU_BLOCK)
    r = jax.nn.sigmoid(jnp.einsum('bthi,hij->bthj', vb, wa).reshape(B, T, C) + ba)
    gi = jax.nn.sigmoid(jnp.einsum('bthi,hij->bthj', vb, wx).reshape(B, T, C) + bx)
    log_a = -LRU_C * r * jax.nn.softplus(-lam)
    return jnp.exp(log_a), jnp.sqrt(-jnp.expm1(2 * log_a)) * (gi * v32)


def rglru_mixer(hl, hc, need_ctx, w_in, conv_w, conv_b, gate_a_w, gate_a_b, gate_x_w, gate_x_b, lam, w_out):
    dt = hl.dtype
    w_y, w_v = w_in[:, :LRU_WIDTH], w_in[:, LRU_WIDTH:]
    v_l = centred_dwconv(hl @ w_v, conv_w, conv_b)
    v_c = centred_dwconv(hc @ w_v, conv_w, conv_b)
    h_l, h_c = [], []
    for d in range(2):
        gates = (gate_a_w[d], gate_a_b[d], gate_x_w[d], gate_x_b[d], lam[d])
        a, b = rglru_coeffs(flip(v_c, d), *gates)
        s_c = linear_scan(a, b, jnp.zeros_like(a[:, 0]))
        a, b = rglru_coeffs(flip(v_l, d), *gates)
        s_l = linear_scan(a, b, s_c[:, -1])
        h_l.append(flip(s_l, d))
        h_c.append(flip(s_c, d))

    def readout(h, hs):
        return (jax.nn.gelu(h @ w_y) * (hs[0] + hs[1]).astype(dt)) @ w_out

    y_c = readout(hc, h_c) if need_ctx else None
    return readout(hl, h_l), y_c


def qshift_grid(x, rows):
    B, T, D = x.shape
    q = D // 4
    g = jnp.pad(x.reshape(B, rows, GRID_W, D), ((0, 0), (1, 1), (1, 1), (0, 0)))
    left = g[:, 1:-1, :-2, :q]
    right = g[:, 1:-1, 2:, q:2 * q]
    up = g[:, :-2, 1:-1, 2 * q:3 * q]
    down = g[:, 2:, 1:-1, 3 * q:]
    return jnp.concatenate([left, right, up, down], axis=-1).reshape(B, T, D)


def bishift_seq(x):
    h = x.shape[-1] // 2
    prev = jnp.pad(x[:, :-1, :h], ((0, 0), (1, 0), (0, 0)))
    nxt = jnp.pad(x[:, 1:, h:], ((0, 0), (0, 1), (0, 0)))
    return jnp.concatenate([prev, nxt], axis=-1)


def wkv7_scan(s0, r, decay, kk, a, k, v):
    def step(S, inp):
        r_t, w_t, kk_t, a_t, k_t, v_t = inp
        sa = jnp.einsum('bhvk,bhk->bhv', S, kk_t)
        S = S * w_t[:, :, None, :] - sa[..., None] * (kk_t * a_t)[:, :, None, :] + v_t[..., None] * k_t[:, :, None, :]
        return S, jnp.einsum('bhvk,bhk->bhv', S, r_t)

    xs = tuple(jnp.moveaxis(t, 1, 0) for t in (r, decay, kk, a, k, v))
    S, ys = lax.scan(step, s0, xs)
    return S, jnp.moveaxis(ys, 0, 1)


def rwkv7_mixer(hl, hc, rows, need_ctx, mu, w_r, w_k, w_v, w_o, w0, w1, w2, a0, a1, a2, g1, g2,
                k_k, k_a, r_k, ln_w, ln_b):
    dt = hl.dtype
    hd = (RWKV_HEADS, RWKV_HEAD)

    def heads(t):
        return t.reshape(t.shape[:2] + hd).astype(jnp.float32)

    def prep(h, shifted):
        xx = shifted - h
        xr, xw, xk, xv, xa = (h + xx * mu[n] for n in range(5))
        r, k, v = heads(xr @ w_r), heads(xk @ w_k), heads(xv @ w_v)
        kk = k * k_k.reshape(hd)
        kk = kk * lax.rsqrt(jnp.maximum(jnp.sum(kk * kk, axis=-1, keepdims=True), 1e-24))
        dirs = []
        for d in range(2):
            z = heads(w0[d] + jnp.tanh(xw @ w1[d]) @ w2[d])
            decay = jnp.exp(-jnp.exp(-jax.nn.softplus(-z) - 0.5))
            a = jax.nn.sigmoid(heads(a0[d] + (xa @ a1[d]) @ a2[d]))
            dirs.append((decay, a, k * (1 + (a - 1) * k_a.reshape(hd))))
        return r, v, kk, dirs, xx

    rl, vl, kkl, dirs_l, xxl = prep(hl, qshift_grid(hl, rows))
    rc, vc, kkc, dirs_c, xxc = prep(hc, bishift_seq(hc))
    B = hl.shape[0]
    y_l, y_c = [], []
    for d in range(2):
        s0 = jnp.zeros((B,) + hd + (RWKV_HEAD,), jnp.float32)
        dec, a, kd = dirs_c[d]
        s_c, yc = wkv7_scan(s0, *(flip(t, d) for t in (rc, dec, kkc, a, kd, vc)))
        dec, a, kd = dirs_l[d]
        _, yl = wkv7_scan(s_c, *(flip(t, d) for t in (rl, dec, kkl, a, kd, vl)))
        y_l.append(flip(yl, d))
        y_c.append(flip(yc, d))

    def readout(ys, r, v, dirs, h, xx):
        y = ys[0] + ys[1]
        mean = jnp.mean(y, axis=-1, keepdims=True)
        var = jnp.mean(jnp.square(y - mean), axis=-1, keepdims=True)
        y = (y - mean) * lax.rsqrt(var + GN_EPS) * ln_w.reshape(hd) + ln_b.reshape(hd)
        bonus = (jnp.sum(r * dirs[0][2] * r_k, axis=-1, keepdims=True)
                 + jnp.sum(r * dirs[1][2] * r_k, axis=-1, keepdims=True)) * v
        g = jax.nn.sigmoid((h + xx * mu[5]) @ g1) @ g2
        return ((y + bonus).reshape(h.shape).astype(dt) * g) @ w_o

    y_ctx = readout(y_c, rc, vc, dirs_c, hc, xxc) if need_ctx else None
    return readout(y_l, rl, vl, dirs_l, hl, xxl), y_ctx


def hgrn_lower_bounds(logits):
    cum = jnp.cumsum(jax.nn.softmax(logits.astype(jnp.float32), axis=0), axis=0)
    return cum - cum[0]


def chunk_gla(q, k, v, log_f, s0):
    B, T, H, DK = q.shape
    DV = v.shape[-1]
    n = T // CHUNK

    def blocks(t):
        return t.reshape(B, n, CHUNK, H, t.shape[-1]).transpose(1, 0, 3, 2, 4)

    incl = jnp.tril(jnp.ones((CHUNK, CHUNK), dtype=bool))[:, :, None]

    def step(S, inp):
        qc, kc, vc, gc = inp
        b = jnp.cumsum(gc, axis=2)
        rel = jnp.where(incl, b[:, :, :, None, :] - b[:, :, None, :, :], -jnp.inf)
        att = jnp.einsum('bhtk,bhtsk,bhsk->bhts', qc, jnp.exp(rel), kc)
        o = att @ vc + jnp.einsum('bhtk,bhkv->bhtv', qc * jnp.exp(b), S)
        b_end = b[:, :, -1:, :]
        S = jnp.exp(b_end[:, :, 0, :, None]) * S + jnp.einsum('bhsk,bhsv->bhkv', kc * jnp.exp(b_end - b), vc)
        return S, o

    S, o = lax.scan(step, s0, (blocks(q), blocks(k), blocks(v), blocks(log_f)))
    return S, o.transpose(1, 0, 3, 2, 4).reshape(B, T, H, DV)


def hgrn2_mixer(hl, hc, need_ctx, lb, w_in, gn_g, w_out):
    dt = hl.dtype
    lb = lb.reshape(HGRN_HEADS, HGRN_DK)
    log_lb, log_1m_lb = jnp.log(lb), jnp.log1p(-lb)

    def heads(t):
        return t.reshape(t.shape[:2] + (HGRN_HEADS, -1)).astype(jnp.float32)

    def prep(h):
        q, i, f_fw, f_bw, g = jnp.split(h @ w_in, 5, axis=-1)
        gates = []
        for f in (f_fw, f_bw):
            f = heads(f)
            gates.append(((1 - lb) * jax.nn.sigmoid(-f),
                          jnp.logaddexp(log_lb, log_1m_lb + jax.nn.log_sigmoid(f))))
        return heads(jax.nn.silu(q)), heads(i), gates, g

    ql, vl, gates_l, gl = prep(hl)
    qc, vc, gates_c, gc = prep(hc)
    B = hl.shape[0]
    o_l, o_c = [], []
    for d in range(2):
        s0 = jnp.zeros((B, HGRN_HEADS, HGRN_DK, HGRN_DV), jnp.float32)
        s_c, oc = chunk_gla(flip(qc, d), flip(gates_c[d][0], d), flip(vc, d), flip(gates_c[d][1], d), s0)
        _, ol = chunk_gla(flip(ql, d), flip(gates_l[d][0], d), flip(vl, d), flip(gates_l[d][1], d), s_c)
        o_l.append(flip(ol, d))
        o_c.append(flip(oc, d))

    def readout(o_dirs, g):
        o = o_dirs[0] + o_dirs[1]
        o = o * lax.rsqrt(jnp.mean(o * o, axis=-1, keepdims=True) + EPS) * gn_g.reshape(HGRN_HEADS, HGRN_DV)
        return (o.reshape(g.shape).astype(dt) * jax.nn.silu(g)) @ w_out

    y_c = readout(o_c, gc) if need_ctx else None
    return readout(o_l, gl), y_c


def setup_inputs(seed: int = 0) -> dict:
    key = jax.random.key(seed)
    keys = iter(jax.random.split(key, 48))
    f32 = jnp.float32

    def normal(shape, scale):
        return jax.random.normal(next(keys), shape, f32) * scale

    def dense(shape, fan_in, gain=1.0):
        return normal(shape, gain * fan_in ** -0.5)

    def near_one(shape):
        return 1.0 + normal(shape, 0.02)

    D, R, F = D_MODEL, ADA_RANK, D_FF
    DR, NH, BW = LRU_WIDTH, LRU_HEADS, LRU_BLOCK
    return {
        'x': normal((BATCH, SEQ, D), 1.0),
        'c': normal((BATCH, D), 1.0),
        'ctx': normal((BATCH, CTX_LEN, D), 1.0),
        'c_ctx': normal((D,), 1.0),
        'ada_down': dense((DEPTH, D, R), D),
        'ada_up': dense((DEPTH, R, N_MOD * D), R, 0.5),
        'ada_b': normal((DEPTH, N_MOD * D), 0.02),
        'norm_g': near_one((DEPTH, 3, D)),
        'ffn_w13': dense((DEPTH, 2, D, 2 * F), D),
        'ffn_w2': dense((DEPTH, 2, F, D), F),
        'final_g': near_one((D,)),
        'lru_w_in': dense((N_A, D, 2 * DR), D),
        'lru_conv_w': dense((N_A, CONV_W, DR), CONV_W),
        'lru_conv_b': normal((N_A, DR), 0.02),
        'lru_gate_a_w': dense((N_A, 2, NH, BW, BW), BW),
        'lru_gate_a_b': normal((N_A, 2, DR), 0.02),
        'lru_gate_x_w': dense((N_A, 2, NH, BW, BW), BW),
        'lru_gate_x_b': normal((N_A, 2, DR), 0.02),
        'lru_lam': (lambda s: jnp.log(s) - jnp.log1p(-s))(
            jax.random.uniform(next(keys), (N_A, 2, DR), f32, 0.9, 0.999) ** (1.0 / LRU_C)),
        'lru_w_out': dense((N_A, DR, D), DR),
        'rwkv_mu': jax.random.uniform(next(keys), (N_B, 6, D), f32),
        'rwkv_w_r': dense((N_B, D, D), D),
        'rwkv_w_k': dense((N_B, D, D), D),
        'rwkv_w_v': dense((N_B, D, D), D),
        'rwkv_w_o': dense((N_B, D, D), D),
        'rwkv_w0': jax.random.uniform(next(keys), (N_B, 2, D), f32, -6.0, -1.0),
        'rwkv_w1': dense((N_B, 2, D, LORA_W), D, 0.5),
        'rwkv_w2': dense((N_B, 2, LORA_W, D), LORA_W, 0.5),
        'rwkv_a0': normal((N_B, 2, D), 0.1),
        'rwkv_a1': dense((N_B, 2, D, LORA_A), D, 0.5),
        'rwkv_a2': dense((N_B, 2, LORA_A, D), LORA_A, 0.5),
        'rwkv_g1': dense((N_B, D, LORA_G), D),
        'rwkv_g2': dense((N_B, LORA_G, D), LORA_G),
        'rwkv_k_k': 0.85 + normal((N_B, D), 0.02),
        'rwkv_k_a': near_one((N_B, D)),
        'rwkv_r_k': normal((N_B, RWKV_HEADS, RWKV_HEAD), 0.1),
        'rwkv_ln_w': near_one((N_B, D)),
        'rwkv_ln_b': normal((N_B, D), 0.02),
        'hgrn_w_in': dense((N_C, D, 5 * D), D),
        'hgrn_lb_logits': normal((DEPTH, D), 0.5),
        'hgrn_gn_g': near_one((N_C, D)),
        'hgrn_w_out': dense((N_C, D, D), D),
    }


def reference(x, c, ctx, c_ctx, ada_down, ada_up, ada_b, norm_g, ffn_w13, ffn_w2, final_g,
              lru_w_in, lru_conv_w, lru_conv_b, lru_gate_a_w, lru_gate_a_b, lru_gate_x_w, lru_gate_x_b,
              lru_lam, lru_w_out,
              rwkv_mu, rwkv_w_r, rwkv_w_k, rwkv_w_v, rwkv_w_o, rwkv_w0, rwkv_w1, rwkv_w2,
              rwkv_a0, rwkv_a1, rwkv_a2, rwkv_g1, rwkv_g2, rwkv_k_k, rwkv_k_a, rwkv_r_k,
              rwkv_ln_w, rwkv_ln_b,
              hgrn_w_in, hgrn_lb_logits, hgrn_gn_g, hgrn_w_out):
    rows = x.shape[1] // GRID_W
    s_lat = jax.nn.silu(c)
    s_ctx = jax.nn.silu(c_ctx)
    lower_bounds = hgrn_lower_bounds(hgrn_lb_logits)
    xl, xc = x, ctx
    for i in range(DEPTH):
        need_ctx = i < DEPTH - 1
        kind, j = i % N_MIXERS, i // N_MIXERS
        ml = ada_modulation(s_lat, ada_down[i], ada_up[i], ada_b[i])
        mc = ada_modulation(s_ctx, ada_down[i], ada_up[i], ada_b[i])
        xl = half_ffn(xl, ml, 0, norm_g[i, 0], ffn_w13[i, 0], ffn_w2[i, 0])
        xc = half_ffn(xc, mc, 0, norm_g[i, 0], ffn_w13[i, 0], ffn_w2[i, 0])
        hl = modulate(rmsnorm(xl, norm_g[i, 1]), ml[3], ml[4])
        hc = modulate(rmsnorm(xc, norm_g[i, 1]), mc[3], mc[4])
        if kind == 0:
            yl, yc = rglru_mixer(hl, hc, need_ctx, lru_w_in[j], lru_conv_w[j], lru_conv_b[j],
                                 lru_gate_a_w[j], lru_gate_a_b[j], lru_gate_x_w[j], lru_gate_x_b[j],
                                 lru_lam[j], lru_w_out[j])
        elif kind == 1:
            yl, yc = rwkv7_mixer(hl, hc, rows, need_ctx, rwkv_mu[j], rwkv_w_r[j], rwkv_w_k[j], rwkv_w_v[j],
                                 rwkv_w_o[j], rwkv_w0[j], rwkv_w1[j], rwkv_w2[j], rwkv_a0[j], rwkv_a1[j],
                                 rwkv_a2[j], rwkv_g1[j], rwkv_g2[j], rwkv_k_k[j], rwkv_k_a[j], rwkv_r_k[j],
                                 rwkv_ln_w[j], rwkv_ln_b[j])
        else:
            yl, yc = hgrn2_mixer(hl, hc, need_ctx, lower_bounds[i], hgrn_w_in[j], hgrn_gn_g[j], hgrn_w_out[j])
        xl = xl + ml[5] * yl
        xl = half_ffn(xl, ml, 6, norm_g[i, 2], ffn_w13[i, 1], ffn_w2[i, 1])
        if need_ctx:
            xc = xc + mc[5] * yc
            xc = half_ffn(xc, mc, 6, norm_g[i, 2], ffn_w13[i, 1], ffn_w2[i, 1])
    return rmsnorm(xl, final_g)
```

```python
import functools
import math

import jax
import jax.numpy as jnp
from jax import lax
from jax.experimental import pallas as pl
from jax.experimental.pallas import tpu as pltpu

F32 = jnp.float32
BF16 = jnp.bfloat16

EPS = 1e-6
GN_EPS = 64e-5
LRU_C = 8.0
GRID_W = 64
N_MOD = 9
WKV_HEAD = 64
GLA_HEAD = 128
CHUNK = 64
SUB = 16
LANES = 128
VMEM_LIMIT_BYTES = 56 * 1024 * 1024


def _cparams(*sem):
    return pltpu.CompilerParams(dimension_semantics=sem, vmem_limit_bytes=VMEM_LIMIT_BYTES)


def _tile(n, pref):
    if n <= pref:
        return n
    for t in range(pref, 7, -1):
        if n % t == 0 and t % 8 == 0:
            return t
    return n


def _bf(x):
    return x.astype(BF16)


def _dot(a, b):
    return jnp.dot(a, b, preferred_element_type=F32)


def _dot_nt(a, b):
    return lax.dot_general(a, b, (((1,), (1,)), ((), ())), preferred_element_type=F32)


def _dot_tn(a, b):
    return lax.dot_general(a, b, (((0,), (0,)), ((), ())), preferred_element_type=F32)


def _split_terms(x, terms):
    out, rem = [], x
    for _ in range(terms):
        p = _bf(rem)
        out.append(p)
        rem = rem - p.astype(F32)
    return out


def _exact_left(m_bf, x, terms):
    acc = None
    for p in _split_terms(x, terms):
        d = _dot(m_bf, p)
        acc = d if acc is None else acc + d
    return acc


def _exact_right(x, m_bf, terms):
    acc = None
    for p in _split_terms(x, terms):
        d = _dot(p, m_bf)
        acc = d if acc is None else acc + d
    return acc


def _sigmoid(x):
    return jax.nn.sigmoid(x)


def _silu(x):
    return x * jax.nn.sigmoid(x)


def _gelu_tanh(x):
    return 0.5 * x * (1.0 + jnp.tanh(math.sqrt(2.0 / math.pi) * (x + 0.044715 * (x * x * x))))


def _softplus(x):
    return jnp.maximum(x, 0.0) + jnp.log1p(jnp.exp(-jnp.abs(x)))


def _normmod(x, g, shift, scale):
    ms = jnp.mean(x * x, axis=-1, keepdims=True)
    return (x * lax.rsqrt(ms + EPS) * g) * (1.0 + scale) + shift


def _ada_kernel(cc_ref, down_ref, up_ref, b_ref, o_ref):
    s = _silu(cc_ref[...])
    hi = lax.Precision.HIGHEST
    t = jnp.dot(s, down_ref[...], precision=hi, preferred_element_type=F32)
    o_ref[...] = jnp.dot(t, up_ref[...], precision=hi, preferred_element_type=F32) + b_ref[...]


def ada_all_layers(cc, down, up, bias):
    L, D, R = down.shape
    N = up.shape[2]
    tn = _tile(N, 4096)
    return pl.pallas_call(
        _ada_kernel,
        grid=(L, N // tn),
        in_specs=[
            pl.BlockSpec((8, D), lambda l, j: (0, 0)),
            pl.BlockSpec((None, D, R), lambda l, j: (l, 0, 0)),
            pl.BlockSpec((None, R, tn), lambda l, j: (l, 0, j)),
            pl.BlockSpec((None, 1, tn), lambda l, j: (l, 0, j)),
        ],
        out_specs=pl.BlockSpec((None, 8, tn), lambda l, j: (l, 0, j)),
        out_shape=jax.ShapeDtypeStruct((L, 8, N), F32),
        compiler_params=_cparams("parallel", "parallel"),
        name="ada",
    )(cc, down, up, bias.reshape(L, 1, N))


def _ffn_kernel(x_ref, mod_ref, w1_ref, w3_ref, w2_ref, o_ref, h_ref):
    f = pl.program_id(1)

    @pl.when(f == 0)
    def _():
        h_ref[...] = _bf(_normmod(x_ref[...], mod_ref[0:1], mod_ref[1:2], mod_ref[2:3]))
        o_ref[...] = jnp.zeros_like(o_ref)

    h = h_ref[...]
    act = _bf(_silu(_dot(h, w1_ref[...])) * _dot(h, w3_ref[...]))
    o_ref[...] += _dot(act, w2_ref[...])

    @pl.when(f == pl.num_programs(1) - 1)
    def _():
        o_ref[...] = x_ref[...] + 0.5 * mod_ref[3:4] * o_ref[...]


def half_ffn(x, mod, w13, w2, li, ki):
    M, D = x.shape
    F = w2.shape[2]
    tm = _tile(M, 512)
    tf = _tile(F, 256)
    nf = F // tf
    return pl.pallas_call(
        _ffn_kernel,
        grid=(M // tm, nf),
        in_specs=[
            pl.BlockSpec((tm, D), lambda i, f: (i, 0), pipeline_mode=pl.Buffered(1)),
            pl.BlockSpec((8, D), lambda i, f: (0, 0)),
            pl.BlockSpec((None, None, D, tf), lambda i, f: (li, ki, 0, f)),
            pl.BlockSpec((None, None, D, tf), lambda i, f: (li, ki, 0, nf + f)),
            pl.BlockSpec((None, None, tf, D), lambda i, f: (li, ki, f, 0)),
        ],
        out_specs=pl.BlockSpec((tm, D), lambda i, f: (i, 0)),
        out_shape=jax.ShapeDtypeStruct((M, D), F32),
        scratch_shapes=[pltpu.VMEM((tm, D), BF16)],
        compiler_params=_cparams("parallel", "arbitrary"),
        name="half_ffn",
    )(x, mod, w13, w13, w2)


def _proj_kernel(*refs, norm, n_gelu, residual):
    it = iter(refs)
    a_ref = next(it)
    mod_ref = next(it) if norm else None
    w_ref = next(it)
    res_ref = next(it) if residual else None
    gate_ref = next(it) if residual else None
    o_ref = next(it)
    h_ref = next(it) if norm else None
    j = pl.program_id(1)

    if norm:
        @pl.when(j == 0)
        def _():
            h_ref[...] = _bf(_normmod(a_ref[...], mod_ref[0:1], mod_ref[1:2], mod_ref[2:3]))
        lhs = h_ref[...]
    else:
        lhs = a_ref[...]
    acc = _dot(lhs, w_ref[...])
    if residual:
        o_ref[...] = res_ref[...] + gate_ref[0:1] * acc
    elif n_gelu:
        @pl.when(j < n_gelu)
        def _():
            o_ref[...] = _gelu_tanh(acc)

        @pl.when(j >= n_gelu)
        def _():
            o_ref[...] = acc
    else:
        o_ref[...] = acc


def proj(a, w, w_index, *, mod=None, gelu_cols=0, res=None, gate=None, tm_pref=512, tn_pref=512):
    M, K = a.shape
    N = w.shape[-1]
    tm = _tile(M, tm_pref)
    tn = _tile(N, tn_pref)
    norm = mod is not None
    residual = res is not None
    assert gelu_cols % tn == 0
    lead = tuple(w_index)
    in_specs = [pl.BlockSpec((tm, K), lambda i, j: (i, 0))]
    args = [a]
    if norm:
        in_specs.append(pl.BlockSpec((8, K), lambda i, j: (0, 0)))
        args.append(mod)
    in_specs.append(pl.BlockSpec((None,) * len(lead) + (K, tn), lambda i, j: lead + (0, j)))
    args.append(w)
    if residual:
        in_specs.append(pl.BlockSpec((tm, tn), lambda i, j: (i, j)))
        in_specs.append(pl.BlockSpec((8, tn), lambda i, j: (0, j)))
        args += [res, gate]
    return pl.pallas_call(
        functools.partial(_proj_kernel, norm=norm, n_gelu=gelu_cols // tn, residual=residual),
        grid=(M // tm, N // tn),
        in_specs=in_specs,
        out_specs=pl.BlockSpec((tm, tn), lambda i, j: (i, j)),
        out_shape=jax.ShapeDtypeStruct((M, N), F32),
        scratch_shapes=[pltpu.VMEM((tm, K), BF16)] if norm else [],
        compiler_params=_cparams("parallel", "arbitrary"),
        name="proj",
    )(*args)


def _rmsnorm_kernel(x_ref, g_ref, o_ref):
    x = x_ref[...]
    ms = jnp.mean(x * x, axis=-1, keepdims=True)
    o_ref[...] = x * lax.rsqrt(ms + EPS) * g_ref[...]


def final_rmsnorm(x, g):
    M, D = x.shape
    tm = _tile(M, 512)
    return pl.pallas_call(
        _rmsnorm_kernel,
        grid=(M // tm,),
        in_specs=[pl.BlockSpec((tm, D), lambda i: (i, 0)), pl.BlockSpec((1, D), lambda i: (0, 0))],
        out_specs=pl.BlockSpec((tm, D), lambda i: (i, 0)),
        out_shape=jax.ShapeDtypeStruct((M, D), F32),
        compiler_params=_cparams("parallel"),
        name="final_rmsnorm",
    )(x, g.reshape(1, D))


def _lru_kernel(ul_ref, uc_ref, cw_ref, cb_ref, wa_ref, wx_ref, ba_ref, bx_ref, lam_ref, *out_refs, R, need_ctx):
    hl_ref = out_refs[0]
    hc_ref = out_refs[1] if need_ctx else None
    T, W = ul_ref.shape
    Tc = uc_ref.shape[0]
    cw = cw_ref[...]
    cb = cb_ref[...]
    sub = lax.broadcasted_iota(jnp.int32, (R, W), 0) & 7

    def conv_chunk(u_ref, t0, n_rows):
        main = u_ref[pl.ds(t0, R), :]
        p0 = pl.multiple_of(jnp.maximum(t0 - 8, 0), 8)
        n0 = pl.multiple_of(jnp.minimum(t0 + R, n_rows - 8), 8)
        prev = jnp.where(t0 > 0, u_ref[pl.ds(p0, 8), :], 0.0)
        nxt = jnp.where(t0 + R < n_rows, u_ref[pl.ds(n0, 8), :], 0.0)
        ext = jnp.concatenate([prev, main, nxt], axis=0)
        n = R + 16
        out = cb + ext[8:8 + R] * cw[2:3]
        out = out + pltpu.roll(ext, 2, 0)[8:8 + R] * cw[0:1]
        out = out + pltpu.roll(ext, 1, 0)[8:8 + R] * cw[1:2]
        out = out + pltpu.roll(ext, n - 1, 0)[8:8 + R] * cw[3:4]
        return out

    def scan_chunk(v, d, carry, rev):
        vb = _bf(v)
        r = _sigmoid(_dot(vb, wa_ref[d]) + ba_ref[d:d + 1])
        gi = _sigmoid(_dot(vb, wx_ref[d]) + bx_ref[d:d + 1])
        log_a = -LRU_C * r * _softplus(-lam_ref[d:d + 1])
        a = jnp.exp(log_a)
        b = jnp.sqrt(-jnp.tanh(log_a) * (a * a + 1.0)) * (gi * v)
        for s in (1, 2, 4):
            if rev:
                a_s, b_s, m = pltpu.roll(a, R - s, 0), pltpu.roll(b, R - s, 0), sub < 8 - s
            else:
                a_s, b_s, m = pltpu.roll(a, s, 0), pltpu.roll(b, s, 0), sub >= s
            b = jnp.where(m, a * b_s + b, b)
            a = jnp.where(m, a * a_s, a)
        groups = R // 8
        hs = [None] * groups
        for g in (range(groups - 1, -1, -1) if rev else range(groups)):
            h = a[8 * g:8 * g + 8] * carry + b[8 * g:8 * g + 8]
            hs[g] = h
            carry = h[0:1] if rev else h[7:8]
        return jnp.concatenate(hs, axis=0), carry

    def run(u_ref, o_ref, n_rows, d, carry):
        rev = d == 1
        nchunks = n_rows // R

        def body(i, carry):
            c = nchunks - 1 - i if rev else i
            t0 = pl.multiple_of(c * R, R)
            h, carry = scan_chunk(conv_chunk(u_ref, t0, n_rows), d, carry, rev)
            if o_ref is not None:
                if d == 0:
                    o_ref[pl.ds(t0, R), :] = h
                else:
                    o_ref[pl.ds(t0, R), :] += h
            return carry

        return lax.fori_loop(0, nchunks, body, carry)

    for d in (0, 1):
        carry = run(uc_ref, hc_ref, Tc, d, jnp.zeros((1, W), F32))
        run(ul_ref, hl_ref, T, d, carry)


def lru_scan(pl_out, pc_out, conv_w, conv_b, wa, wx, ba, bx, lam, li, need_ctx):
    T = pl_out.shape[0]
    Tc = pc_out.shape[0]
    D = pl_out.shape[1] // 2
    NH, W = wa.shape[2], wa.shape[3]
    assert W % LANES == 0 and NH * W == D
    R = _tile(math.gcd(T, Tc), 256)
    col = D // W
    vec2 = pl.BlockSpec((None, 2, W), lambda h: (li, 0, h))
    out_shape = [jax.ShapeDtypeStruct((T, D), F32)]
    out_specs = [pl.BlockSpec((T, W), lambda h: (0, h))]
    if need_ctx:
        out_shape.append(jax.ShapeDtypeStruct((Tc, D), F32))
        out_specs.append(pl.BlockSpec((Tc, W), lambda h: (0, h)))
    return pl.pallas_call(
        functools.partial(_lru_kernel, R=R, need_ctx=need_ctx),
        grid=(NH,),
        in_specs=[
            pl.BlockSpec((T, W), lambda h: (0, col + h)),
            pl.BlockSpec((Tc, W), lambda h: (0, col + h)),
            pl.BlockSpec((None, 4, W), lambda h: (li, 0, h)),
            pl.BlockSpec((None, 1, W), lambda h: (li, 0, h)),
            pl.BlockSpec((None, 2, None, W, W), lambda h: (li, 0, h, 0, 0)),
            pl.BlockSpec((None, 2, None, W, W), lambda h: (li, 0, h, 0, 0)),
            vec2, vec2, vec2,
        ],
        out_specs=out_specs,
        out_shape=out_shape,
        compiler_params=_cparams("parallel"),
        name="lru_scan",
    )(pl_out, pc_out, conv_w, conv_b.reshape(conv_b.shape[0], 1, D), wa, wx, ba, bx, lam)


def _mul_kernel(a_ref, b_ref, o_ref):
    o_ref[...] = _bf(a_ref[...] * b_ref[...])


def lru_combine(p_out, hs):
    M, D = hs.shape
    tm = _tile(M, 256)
    return pl.pallas_call(
        _mul_kernel,
        grid=(M // tm,),
        in_specs=[pl.BlockSpec((tm, D), lambda i: (i, 0)), pl.BlockSpec((tm, D), lambda i: (i, 0))],
        out_specs=pl.BlockSpec((tm, D), lambda i: (i, 0)),
        out_shape=jax.ShapeDtypeStruct((M, D), BF16),
        compiler_params=_cparams("parallel"),
        name="lru_combine",
    )(p_out, hs)


def _mix_write(h, shifted, mu_ref, o_refs, cols):
    xx = shifted - h
    for n, o_ref in enumerate(o_refs):
        o_ref[:, cols] = _bf(h + xx * mu_ref[n:n + 1, cols])


def _mix_lat_kernel(xm_ref, xp_ref, xn_ref, mod_ref, mu_ref, *o_refs, T):
    tm, D = xm_ref.shape
    q = D // 4
    i = pl.program_id(0)
    g, sh, sc = mod_ref[0:1], mod_ref[1:2], mod_ref[2:3]
    hm = _normmod(xm_ref[...], g, sh, sc)
    hp = _normmod(xp_ref[...], g, sh, sc)
    hn = _normmod(xn_ref[...], g, sh, sc)
    row = lax.broadcasted_iota(jnp.int32, (tm, 1), 0)
    t = i * tm + row
    colpos = row & (GRID_W - 1)
    s0 = slice(0, q)
    left = jnp.where(colpos > 0, pltpu.roll(hm[:, s0], 1, 0), 0.0)
    _mix_write(hm[:, s0], left, mu_ref, o_refs, s0)
    s1 = slice(q, 2 * q)
    right = jnp.where(colpos < GRID_W - 1, pltpu.roll(hm[:, s1], tm - 1, 0), 0.0)
    _mix_write(hm[:, s1], right, mu_ref, o_refs, s1)
    s2 = slice(2 * q, 3 * q)
    up = jnp.concatenate([hp[:, s2], hm[:tm - GRID_W, s2]], axis=0) if tm > GRID_W else hp[:, s2]
    up = jnp.where(t >= GRID_W, up, 0.0)
    _mix_write(hm[:, s2], up, mu_ref, o_refs, s2)
    s3 = slice(3 * q, D)
    down = jnp.concatenate([hm[GRID_W:, s3], hn[:, s3]], axis=0) if tm > GRID_W else hn[:, s3]
    down = jnp.where(t < T - GRID_W, down, 0.0)
    _mix_write(hm[:, s3], down, mu_ref, o_refs, s3)


def _mix_ctx_kernel(x_ref, mod_ref, mu_ref, *o_refs):
    Tc, D = x_ref.shape
    hh = D // 2
    h = _normmod(x_ref[...], mod_ref[0:1], mod_ref[1:2], mod_ref[2:3])
    row = lax.broadcasted_iota(jnp.int32, (Tc, 1), 0)
    s0 = slice(0, hh)
    prev = jnp.where(row > 0, pltpu.roll(h[:, s0], 1, 0), 0.0)
    _mix_write(h[:, s0], prev, mu_ref, o_refs, s0)
    s1 = slice(hh, D)
    nxt = jnp.where(row < Tc - 1, pltpu.roll(h[:, s1], Tc - 1, 0), 0.0)
    _mix_write(h[:, s1], nxt, mu_ref, o_refs, s1)


def rwkv_shiftmix(x, mod, mu, li, grid_tokens):
    M, D = x.shape
    n_out = mu.shape[1]
    mu_spec_args = ((None, n_out, D),)
    out_shape = [jax.ShapeDtypeStruct((M, D), BF16)] * n_out
    if grid_tokens:
        tm = _tile(M, 256)
        assert tm % GRID_W == 0 and M % GRID_W == 0
        r = tm // GRID_W
        nb = M // GRID_W
        return pl.pallas_call(
            functools.partial(_mix_lat_kernel, T=M),
            grid=(M // tm,),
            in_specs=[
                pl.BlockSpec((tm, D), lambda i: (i, 0)),
                pl.BlockSpec((GRID_W, D), lambda i: (jnp.maximum(i * r - 1, 0), 0)),
                pl.BlockSpec((GRID_W, D), lambda i: (jnp.minimum((i + 1) * r, nb - 1), 0)),
                pl.BlockSpec((8, D), lambda i: (0, 0)),
                pl.BlockSpec(*mu_spec_args, lambda i: (li, 0, 0)),
            ],
            out_specs=[pl.BlockSpec((tm, D), lambda i: (i, 0))] * n_out,
            out_shape=out_shape,
            compiler_params=_cparams("parallel"),
            name="rwkv_shiftmix_grid",
        )(x, x, x, mod, mu)
    return pl.pallas_call(
        _mix_ctx_kernel,
        grid=(1,),
        in_specs=[
            pl.BlockSpec((M, D), lambda i: (0, 0)),
            pl.BlockSpec((8, D), lambda i: (0, 0)),
            pl.BlockSpec(*mu_spec_args, lambda i: (li, 0, 0)),
        ],
        out_specs=[pl.BlockSpec((M, D), lambda i: (0, 0))] * n_out,
        out_shape=out_shape,
        compiler_params=_cparams("arbitrary"),
        name="rwkv_shiftmix_seq",
    )(x, mod, mu)


def _lora_kernel(xw_ref, xa_ref, xg_ref, w1_ref, w2_ref, w0_ref, a1_ref, a2_ref, a0_ref, g1_ref, g2_ref,
                 lw_ref, a_ref, g_ref, tw_ref, ta_ref, tg_ref):
    @pl.when(pl.program_id(1) == 0)
    def _():
        for d in (0, 1):
            tw_ref[d] = _bf(jnp.tanh(_dot(xw_ref[...], w1_ref[d])))
            ta_ref[d] = _bf(_dot(xa_ref[...], a1_ref[d]))
        tg_ref[...] = _bf(_sigmoid(_dot(xg_ref[...], g1_ref[...])))

    for d in (0, 1):
        z = w0_ref[d:d + 1] + _dot(tw_ref[d], w2_ref[d])
        lw_ref[d] = -math.exp(-0.5) * _sigmoid(z)
        a_ref[d] = _sigmoid(a0_ref[d:d + 1] + _dot(ta_ref[d], a2_ref[d]))
    g_ref[...] = _dot(tg_ref[...], g2_ref[...])


def rwkv_lora(xw, xa, xg, w1, w2, w0, a1, a2, a0, g1, g2, li):
    M, D = xw.shape
    RW, RA, RG = w1.shape[-1], a1.shape[-1], g1.shape[-1]
    tm = _tile(M, 256)
    tn = _tile(D, 1024)
    row = pl.BlockSpec((tm, D), lambda i, j: (i, 0))
    out2 = pl.BlockSpec((2, tm, tn), lambda i, j: (0, i, j))
    return pl.pallas_call(
        _lora_kernel,
        grid=(M // tm, D // tn),
        in_specs=[
            row, row, row,
            pl.BlockSpec((None, 2, D, RW), lambda i, j: (li, 0, 0, 0)),
            pl.BlockSpec((None, 2, RW, tn), lambda i, j: (li, 0, 0, j)),
            pl.BlockSpec((None, 2, tn), lambda i, j: (li, 0, j)),
            pl.BlockSpec((None, 2, D, RA), lambda i, j: (li, 0, 0, 0)),
            pl.BlockSpec((None, 2, RA, tn), lambda i, j: (li, 0, 0, j)),
            pl.BlockSpec((None, 2, tn), lambda i, j: (li, 0, j)),
            pl.BlockSpec((None, D, RG), lambda i, j: (li, 0, 0)),
            pl.BlockSpec((None, RG, tn), lambda i, j: (li, 0, j)),
        ],
        out_specs=[out2, out2, pl.BlockSpec((tm, tn), lambda i, j: (i, j))],
        out_shape=[jax.ShapeDtypeStruct((2, M, D), F32), jax.ShapeDtypeStruct((2, M, D), F32),
                   jax.ShapeDtypeStruct((M, D), F32)],
        scratch_shapes=[pltpu.VMEM((2, tm, RW), BF16), pltpu.VMEM((2, tm, RA), BF16), pltpu.VMEM((tm, RG), BF16)],
        compiler_params=_cparams("parallel", "arbitrary"),
        name="rwkv_lora",
    )(xw, xa, xg, w1, w2, w0, a1, a2, a0, g1, g2)


def _pair_masks():
    row = lax.broadcasted_iota(jnp.int32, (LANES, LANES), 0)
    col = lax.broadcasted_iota(jnp.int32, (LANES, LANES), 1)
    same_head = (row < WKV_HEAD) == (col < WKV_HEAD)
    return row, col, same_head


def _wkv_kernel(r_ref, k_ref, v_ref, lw_ref, a_ref, par_ref, s0_ref, y_ref, z_ref, s_ref, st_ref, *, rev):
    C = CHUNK
    Tt = r_ref.shape[0]
    nchunks = Tt // C

    @pl.when(pl.program_id(1) == 0)
    def _():
        st_ref[...] = s0_ref[...]

    k_k, k_a, r_k = par_ref[0:1], par_ref[1:2], par_ref[2:3]
    row, col, same_head = _pair_masks()
    ones_head = jnp.where(same_head, 1.0, 0.0).astype(BF16)
    tr, tc = row & (C - 1), col & (C - 1)
    before = (tc > tr) if rev else (tc < tr)
    before_eq = (tc >= tr) if rev else (tc <= tr)
    r64 = lax.broadcasted_iota(jnp.int32, (C, C), 0)
    c64 = lax.broadcasted_iota(jnp.int32, (C, C), 1)
    cum_m = jnp.where((c64 >= r64) if rev else (c64 <= r64), 1.0, 0.0).astype(BF16)
    eye = jnp.where(row == col, 1.0, 0.0)
    lane = lax.broadcasted_iota(jnp.int32, (C, LANES), 1)
    head_a = lane < WKV_HEAD

    def stack2(x):
        return _bf(jnp.concatenate([jnp.where(head_a, x, 0.0), jnp.where(head_a, 0.0, x)], axis=0))

    def unstack(x2):
        return x2[0:C] + x2[C:2 * C]

    def chunk(i, S):
        c = nchunks - 1 - i if rev else i
        sl = pl.ds(pl.multiple_of(c * C, C), C)
        r, k, v, lw, a = r_ref[sl, :], k_ref[sl, :], v_ref[sl, :], lw_ref[sl, :], a_ref[sl, :]
        kk = k * k_k
        kk = kk * lax.rsqrt(jnp.maximum(_exact_right(kk * kk, ones_head, 2), 1e-24))
        kd = k * (1.0 + (a - 1.0) * k_a)
        beta = a * kk
        z_ref[sl, :] = _exact_right(r * kd * r_k, ones_head, 2) * v
        b = _exact_left(cum_m, lw, 3)
        b_tot = b[0:1] if rev else b[C - 1:C]
        e_b = jnp.exp(b)
        e_nb = jnp.exp(-b)
        e_rest = jnp.exp(b_tot - b)
        kap2 = stack2(kk * jnp.exp(b - lw))
        r2 = stack2(r * e_b)
        k2 = stack2(kd * e_nb)
        be2 = stack2(beta * e_nb)
        v2 = stack2(v)
        Sb = _bf(S)
        n = jnp.where(before, -_dot_nt(kap2, be2), 0.0)
        tinv = eye + n
        m = n
        for _ in range(5):
            mb = _bf(m)
            m = _dot(mb, mb)
            tinv = tinv + _dot(_bf(tinv), _bf(m))
        a_kk = jnp.where(before, _dot_nt(kap2, k2), 0.0)
        a_rk = jnp.where(before_eq, _dot_nt(r2, k2), 0.0)
        a_rb = jnp.where(before_eq, _dot_nt(r2, be2), 0.0)
        rhs = _dot_nt(_bf(kk * jnp.exp(b - lw)), Sb) + unstack(_dot(_bf(a_kk), v2))
        u = unstack(_dot(_bf(tinv), stack2(rhs)))
        y = _dot_nt(_bf(r * e_b), Sb) + unstack(_dot(_bf(a_rk), v2) - _dot(_bf(a_rb), stack2(u)))
        y_ref[sl, :] = y
        upd = _dot_tn(_bf(v), _bf(kd * e_rest)) - _dot_tn(_bf(u), _bf(beta * e_rest))
        return S * jnp.exp(b_tot) + jnp.where(same_head, upd, 0.0)

    S = lax.fori_loop(0, nchunks, chunk, st_ref[...])
    st_ref[...] = S

    @pl.when(pl.program_id(1) == pl.num_programs(1) - 1)
    def _():
        s_ref[...] = S


def wkv_scan(r, k, v, lw, a, par, s0, d):
    T, D = r.shape
    HP = D // LANES
    Tt = _tile(T, 256)
    NT = T // Tt
    rev = d == 1

    def tt(t):
        return NT - 1 - t if rev else t

    tok = pl.BlockSpec((Tt, LANES), lambda h, t: (tt(t), h))
    tok_d = pl.BlockSpec((None, Tt, LANES), lambda h, t: (d, tt(t), h))
    st = pl.BlockSpec((None, LANES, LANES), lambda h, t: (h, 0, 0))
    return pl.pallas_call(
        functools.partial(_wkv_kernel, rev=rev),
        grid=(HP, NT),
        in_specs=[tok, tok, tok, tok_d, tok_d, pl.BlockSpec((8, LANES), lambda h, t: (0, h)), st],
        out_specs=[tok, tok, st],
        out_shape=[jax.ShapeDtypeStruct((T, D), F32), jax.ShapeDtypeStruct((T, D), F32),
                   jax.ShapeDtypeStruct((HP, LANES, LANES), F32)],
        scratch_shapes=[pltpu.VMEM((LANES, LANES), F32)],
        compiler_params=_cparams("parallel", "arbitrary"),
        name="wkv_scan",
    )(r, k, v, lw, a, par, s0)


def _rwkv_combine_kernel(y0_ref, y1_ref, z0_ref, z1_ref, g_ref, ln_ref, o_ref):
    D = o_ref.shape[1]
    row = lax.broadcasted_iota(jnp.int32, (LANES, LANES), 0)
    col = lax.broadcasted_iota(jnp.int32, (LANES, LANES), 1)
    mean_m = jnp.where((row < WKV_HEAD) == (col < WKV_HEAD), 1.0 / WKV_HEAD, 0.0).astype(BF16)
    for c in range(D // LANES):
        cs = slice(c * LANES, (c + 1) * LANES)
        y = y0_ref[:, cs] + y1_ref[:, cs]
        yc = y - _exact_right(y, mean_m, 2)
        var = _exact_right(yc * yc, mean_m, 2)
        out = yc * lax.rsqrt(var + GN_EPS) * ln_ref[0:1, cs] + ln_ref[1:2, cs] + (z0_ref[:, cs] + z1_ref[:, cs])
        o_ref[:, cs] = _bf(out * g_ref[:, cs])


def rwkv_combine(y0, y1, z0, z1, g, ln):
    M, D = g.shape
    tm = _tile(M, 256)
    tok = pl.BlockSpec((tm, D), lambda i: (i, 0))
    return pl.pallas_call(
        _rwkv_combine_kernel,
        grid=(M // tm,),
        in_specs=[tok] * 5 + [pl.BlockSpec((8, D), lambda i: (0, 0))],
        out_specs=tok,
        out_shape=jax.ShapeDtypeStruct((M, D), BF16),
        compiler_params=_cparams("parallel"),
        name="rwkv_combine",
    )(y0, y1, z0, z1, g, ln)


def _gla_kernel(q_ref, i_ref, f_ref, lbl_ref, s0_ref, o_ref, s_ref, st_ref, *, rev, layer):
    C = CHUNK
    Tt = q_ref.shape[0]
    nchunks = Tt // C
    nsub = C // SUB

    @pl.when(pl.program_id(1) == 0)
    def _():
        st_ref[...] = s0_ref[...]

    logits = lbl_ref[...]
    e = jnp.exp(logits - jnp.max(logits, axis=0, keepdims=True))
    p = e / jnp.sum(e, axis=0, keepdims=True)
    lb = jnp.zeros((1, LANES), F32)
    for l in range(1, layer + 1):
        lb = lb + p[l:l + 1]
    log_lb = jnp.log(lb)
    log_1m = jnp.log1p(-lb)

    r64 = lax.broadcasted_iota(jnp.int32, (C, C), 0)
    c64 = lax.broadcasted_iota(jnp.int32, (C, C), 1)
    cum_m = jnp.where((c64 >= r64) if rev else (c64 <= r64), 1.0, 0.0).astype(BF16)
    rows = lax.broadcasted_iota(jnp.int32, (C, 1), 0)
    lane_s = lax.broadcasted_iota(jnp.int32, (SUB, C), 1)
    row_s = lax.broadcasted_iota(jnp.int32, (SUB, 1), 0)

    def chunk(ci, Z):
        c = nchunks - 1 - ci if rev else ci
        sl = pl.ds(pl.multiple_of(c * C, C), C)
        q = _silu(q_ref[sl, :])
        v = i_ref[sl, :]
        f = f_ref[sl, :]
        kg = (1.0 - lb) * _sigmoid(-f)
        x2 = log_1m - _softplus(-f)
        mx = jnp.maximum(log_lb, x2)
        g = mx + jnp.log1p(jnp.exp(-jnp.abs(log_lb - x2)))
        b = _exact_left(cum_m, g, 3)
        b_tot = b[0:1] if rev else b[C - 1:C]
        o = _dot_nt(_bf(q * jnp.exp(b)), _bf(Z))
        att_rows = []
        for I in range(nsub):
            lo = I * SUB
            qI, bI = q[lo:lo + SUB], b[lo:lo + SUB]
            first = (I == nsub - 1) if rev else (I == 0)
            if first:
                att = jnp.zeros((SUB, C), F32)
            else:
                ref = b[lo + SUB:lo + SUB + 1] if rev else b[lo - 1:lo]
                earlier = (rows >= lo + SUB) if rev else (rows < lo)
                kt = jnp.where(earlier, kg * jnp.exp(jnp.minimum(ref - b, 0.0)), 0.0)
                att = _dot_nt(_bf(qI * jnp.exp(bI - ref)), _bf(kt))
            for j in range(SUB):
                s = lo + j
                pj = qI * kg[s:s + 1] * jnp.exp(jnp.minimum(bI - b[s:s + 1], 0.0))
                colsum = jnp.sum(pj, axis=-1, keepdims=True)
                valid = (row_s <= j) if rev else (row_s >= j)
                att = att + jnp.where(jnp.logical_and(lane_s == s, valid), colsum, 0.0)
            att_rows.append(att)
        o = o + _dot(_bf(jnp.concatenate(att_rows, axis=0)), _bf(v))
        o_ref[sl, :] = o
        return Z * jnp.exp(b_tot) + _dot_tn(_bf(v), _bf(kg * jnp.exp(b_tot - b)))

    Z = lax.fori_loop(0, nchunks, chunk, st_ref[...])
    st_ref[...] = Z

    @pl.when(pl.program_id(1) == pl.num_programs(1) - 1)
    def _():
        s_ref[...] = Z


def gla_scan(raw, lb_logits, s0, d, layer):
    T = raw.shape[0]
    D = raw.shape[1] // 5
    H = D // LANES
    Tt = _tile(T, 256)
    NT = T // Tt
    rev = d == 1
    L = lb_logits.shape[0]

    def tt(t):
        return NT - 1 - t if rev else t

    st = pl.BlockSpec((None, LANES, LANES), lambda h, t: (h, 0, 0))
    return pl.pallas_call(
        functools.partial(_gla_kernel, rev=rev, layer=layer),
        grid=(H, NT),
        in_specs=[
            pl.BlockSpec((Tt, LANES), lambda h, t: (tt(t), h)),
            pl.BlockSpec((Tt, LANES), lambda h, t: (tt(t), H + h)),
            pl.BlockSpec((Tt, LANES), lambda h, t: (tt(t), (2 + d) * H + h)),
            pl.BlockSpec((L, LANES), lambda h, t: (0, h)),
            st,
        ],
        out_specs=[pl.BlockSpec((Tt, LANES), lambda h, t: (tt(t), h)), st],
        out_shape=[jax.ShapeDtypeStruct((T, D), F32), jax.ShapeDtypeStruct((H, LANES, LANES), F32)],
        scratch_shapes=[pltpu.VMEM((LANES, LANES), F32)],
        compiler_params=_cparams("parallel", "arbitrary"),
        name="gla_scan",
    )(raw, raw, raw, lb_logits, s0)


def _hgrn_combine_kernel(o0_ref, o1_ref, g_ref, gn_ref, o_ref):
    D = o_ref.shape[1]
    for c in range(D // LANES):
        cs = slice(c * LANES, (c + 1) * LANES)
        o = o0_ref[:, cs] + o1_ref[:, cs]
        o = o * lax.rsqrt(jnp.mean(o * o, axis=-1, keepdims=True) + EPS) * gn_ref[0:1, cs]
        o_ref[:, cs] = _bf(o * _silu(g_ref[:, cs]))


def hgrn_combine(o0, o1, raw, gn):
    M, D = o0.shape
    tm = _tile(M, 256)
    tok = pl.BlockSpec((tm, D), lambda i: (i, 0))
    return pl.pallas_call(
        _hgrn_combine_kernel,
        grid=(M // tm,),
        in_specs=[tok, tok, pl.BlockSpec((tm, D), lambda i: (i, 4)), pl.BlockSpec((1, D), lambda i: (0, 0))],
        out_specs=tok,
        out_shape=jax.ShapeDtypeStruct((M, D), BF16),
        compiler_params=_cparams("parallel"),
        name="hgrn_combine",
    )(o0, o1, raw, gn)


def _rows8(*vecs):
    D = vecs[0].shape[-1]
    rows = [v.reshape(1, D) for v in vecs]
    rows.append(jnp.zeros((8 - len(rows), D), F32))
    return jnp.concatenate(rows, axis=0)


def kernel(x, c, ctx, c_ctx, ada_down, ada_up, ada_b, norm_g, ffn_w13, ffn_w2, final_g, lru_w_in, lru_conv_w, lru_conv_b, lru_gate_a_w, lru_gate_a_b, lru_gate_x_w, lru_gate_x_b, lru_lam, lru_w_out, rwkv_mu, rwkv_w_r, rwkv_w_k, rwkv_w_v, rwkv_w_o, rwkv_w0, rwkv_w1, rwkv_w2, rwkv_a0, rwkv_a1, rwkv_a2, rwkv_g1, rwkv_g2, rwkv_k_k, rwkv_k_a, rwkv_r_k, rwkv_ln_w, rwkv_ln_b, hgrn_w_in, hgrn_lb_logits, hgrn_gn_g, hgrn_w_out):
    B, T, D = x.shape
    assert B == 1, "one sequence per call"
    depth = ada_down.shape[0]
    xl, xc = x[0], ctx[0]

    mods = ada_all_layers(_rows8(c[0], c_ctx), ada_down, ada_up, ada_b)
    mods = mods[:, :2].reshape(depth, 2, N_MOD, D)

    w13, w2 = _bf(ffn_w13), _bf(ffn_w2)
    lru_in, lru_out = _bf(lru_w_in), _bf(lru_w_out)
    lru_wa, lru_wx = _bf(lru_gate_a_w), _bf(lru_gate_x_w)
    w_r, w_k, w_v, w_o = _bf(rwkv_w_r), _bf(rwkv_w_k), _bf(rwkv_w_v), _bf(rwkv_w_o)
    lw1, lw2, la1, la2 = _bf(rwkv_w1), _bf(rwkv_w2), _bf(rwkv_a1), _bf(rwkv_a2)
    lg1, lg2 = _bf(rwkv_g1), _bf(rwkv_g2)
    h_in, h_out = _bf(hgrn_w_in), _bf(hgrn_w_out)

    for i in range(depth):
        need_ctx = i < depth - 1
        kind, j = i % 3, i // 3
        ml, mc = mods[i, 0], mods[i, 1]

        def ffn(xs, m, k, which):
            return half_ffn(xs, _rows8(norm_g[i, 2 * which], m[k], m[k + 1], m[k + 2]), w13, w2, i, which)

        xl = ffn(xl, ml, 0, 0)
        xc = ffn(xc, mc, 0, 0)
        mod_l = _rows8(norm_g[i, 1], ml[3], ml[4])
        mod_c = _rows8(norm_g[i, 1], mc[3], mc[4])
        gate_l, gate_c = _rows8(ml[5]), _rows8(mc[5])

        if kind == 0:
            p_l = proj(xl, lru_in, (j,), mod=mod_l, gelu_cols=D)
            p_c = proj(xc, lru_in, (j,), mod=mod_c, gelu_cols=D)
            hs = lru_scan(p_l, p_c, lru_conv_w, lru_conv_b, lru_wa, lru_wx, lru_gate_a_b, lru_gate_x_b, lru_lam,
                          j, need_ctx)
            a_l = lru_combine(p_l, hs[0])
            a_c = lru_combine(p_c, hs[1]) if need_ctx else None
            w_last, w_idx = lru_out, (j,)
        elif kind == 1:
            par = _rows8(rwkv_k_k[j], rwkv_k_a[j], rwkv_r_k[j].reshape(D))
            ln = _rows8(rwkv_ln_w[j], rwkv_ln_b[j])
            outs = []
            state = [jnp.zeros((D // LANES, LANES, LANES), F32)] * 2
            for xs, mod, is_grid in ((xc, mod_c, False), (xl, mod_l, True)):
                xr, xw, xk, xv, xa, xg = rwkv_shiftmix(xs, mod, rwkv_mu, j, is_grid)
                r = proj(xr, w_r, (j,))
                k = proj(xk, w_k, (j,))
                v = proj(xv, w_v, (j,))
                lw, a, g = rwkv_lora(xw, xa, xg, lw1, lw2, rwkv_w0, la1, la2, rwkv_a0, lg1, lg2, j)
                ys, zs = [], []
                for d in (0, 1):
                    y, z, state[d] = wkv_scan(r, k, v, lw, a, par, state[d], d)
                    ys.append(y)
                    zs.append(z)
                outs.append(rwkv_combine(ys[0], ys[1], zs[0], zs[1], g, ln))
            a_c, a_l = outs
            w_last, w_idx = w_o, (j,)
        else:
            outs = []
            state = [jnp.zeros((D // LANES, LANES, LANES), F32)] * 2
            for xs, mod in ((xc, mod_c), (xl, mod_l)):
                raw = proj(xs, h_in, (j,), mod=mod)
                os_ = []
                for d in (0, 1):
                    o, state[d] = gla_scan(raw, hgrn_lb_logits, state[d], d, i)
                    os_.append(o)
                outs.append(hgrn_combine(os_[0], os_[1], raw, hgrn_gn_g[j].reshape(1, D)))
            a_c, a_l = outs
            w_last, w_idx = h_out, (j,)

        xl = proj(a_l, w_last, w_idx, res=xl, gate=gate_l)
        xl = ffn(xl, ml, 6, 1)
        if need_ctx:
            xc = proj(a_c, w_last, w_idx, res=xc, gate=gate_c)
            xc = ffn(xc, mc, 6, 1)

    return final_rmsnorm(xl, final_g)[None]
```

```python
import functools
import math

import jax
import jax.numpy as jnp
from jax import lax
from jax.experimental import pallas as pl
from jax.experimental.pallas import tpu as pltpu

F32 = jnp.float32
BF16 = jnp.bfloat16

EPS = 1e-6
GN_EPS = 64e-5
LRU_C = 8.0
GRID_W = 64
N_MOD = 9
WKV_HEAD = 64
GLA_HEAD = 128
CHUNK = 64
SUB = 16
LANES = 128
VMEM_LIMIT_BYTES = 56 * 1024 * 1024


def _cparams(*sem):
    return pltpu.CompilerParams(dimension_semantics=sem, vmem_limit_bytes=VMEM_LIMIT_BYTES)


def _tile(n, pref):
    if n <= pref:
        return n
    for t in range(pref, 7, -1):
        if n % t == 0 and t % 8 == 0:
            return t
    return n


def _bf(x):
    return x.astype(BF16)


def _dot(a, b):
    return jnp.dot(a, b, preferred_element_type=F32)


def _dot_nt(a, b):
    return lax.dot_general(a, b, (((1,), (1,)), ((), ())), preferred_element_type=F32)


def _dot_tn(a, b):
    return lax.dot_general(a, b, (((0,), (0,)), ((), ())), preferred_element_type=F32)


def _split_terms(x, terms):
    out, rem = [], x
    for _ in range(terms):
        p = _bf(rem)
        out.append(p)
        rem = rem - p.astype(F32)
    return out


def _exact_left(m_bf, x, terms):
    acc = None
    for p in _split_terms(x, terms):
        d = _dot(m_bf, p)
        acc = d if acc is None else acc + d
    return acc


def _exact_right(x, m_bf, terms):
    acc = None
    for p in _split_terms(x, terms):
        d = _dot(p, m_bf)
        acc = d if acc is None else acc + d
    return acc


def _sigmoid(x):
    return jax.nn.sigmoid(x)


def _silu(x):
    return x * jax.nn.sigmoid(x)


def _gelu_tanh(x):
    return 0.5 * x * (1.0 + jnp.tanh(math.sqrt(2.0 / math.pi) * (x + 0.044715 * (x * x * x))))


def _softplus(x):
    return jnp.maximum(x, 0.0) + jnp.log1p(jnp.exp(-jnp.abs(x)))


def _normmod(x, g, shift, scale):
    ms = jnp.mean(x * x, axis=-1, keepdims=True)
    return (x * lax.rsqrt(ms + EPS) * g) * (1.0 + scale) + shift


def _ada_kernel(cc_ref, down_ref, up_ref, b_ref, o_ref):
    s = _silu(cc_ref[...])
    hi = lax.Precision.HIGHEST
    t = jnp.dot(s, down_ref[...], precision=hi, preferred_element_type=F32)
    o_ref[...] = jnp.dot(t, up_ref[...], precision=hi, preferred_element_type=F32) + b_ref[...]


def ada_all_layers(cc, down, up, bias):
    L, D, R = down.shape
    N = up.shape[2]
    tn = _tile(N, 4096)
    return pl.pallas_call(
        _ada_kernel,
        grid=(L, N // tn),
        in_specs=[
            pl.BlockSpec((8, D), lambda l, j: (0, 0)),
            pl.BlockSpec((None, D, R), lambda l, j: (l, 0, 0)),
            pl.BlockSpec((None, R, tn), lambda l, j: (l, 0, j)),
            pl.BlockSpec((None, 1, tn), lambda l, j: (l, 0, j)),
        ],
        out_specs=pl.BlockSpec((None, 8, tn), lambda l, j: (l, 0, j)),
        out_shape=jax.ShapeDtypeStruct((L, 8, N), F32),
        compiler_params=_cparams("parallel", "parallel"),
        name="ada",
    )(cc, down, up, bias.reshape(L, 1, N))


def _ffn_kernel(x_ref, mod_ref, w1_ref, w3_ref, w2_ref, o_ref, h_ref):
    f = pl.program_id(1)

    @pl.when(f == 0)
    def _():
        h_ref[...] = _bf(_normmod(x_ref[...], mod_ref[0:1], mod_ref[1:2], mod_ref[2:3]))
        o_ref[...] = jnp.zeros_like(o_ref)

    h = h_ref[...]
    act = _bf(_silu(_dot(h, w1_ref[...])) * _dot(h, w3_ref[...]))
    o_ref[...] += _dot(act, w2_ref[...])

    @pl.when(f == pl.num_programs(1) - 1)
    def _():
        o_ref[...] = x_ref[...] + 0.5 * mod_ref[3:4] * o_ref[...]


def half_ffn(x, mod, w13, w2, li, ki):
    M, D = x.shape
    F = w2.shape[2]
    tm = _tile(M, 512)
    tf = _tile(F, 256)
    nf = F // tf
    return pl.pallas_call(
        _ffn_kernel,
        grid=(M // tm, nf),
        in_specs=[
            pl.BlockSpec((tm, D), lambda i, f: (i, 0), pipeline_mode=pl.Buffered(1)),
            pl.BlockSpec((8, D), lambda i, f: (0, 0)),
            pl.BlockSpec((None, None, D, tf), lambda i, f: (li, ki, 0, f)),
            pl.BlockSpec((None, None, D, tf), lambda i, f: (li, ki, 0, nf + f)),
            pl.BlockSpec((None, None, tf, D), lambda i, f: (li, ki, f, 0)),
        ],
        out_specs=pl.BlockSpec((tm, D), lambda i, f: (i, 0)),
        out_shape=jax.ShapeDtypeStruct((M, D), F32),
        scratch_shapes=[pltpu.VMEM((tm, D), BF16)],
        compiler_params=_cparams("parallel", "arbitrary"),
        name="half_ffn",
    )(x, mod, w13, w13, w2)


def _proj_kernel(*refs, norm, n_gelu, residual):
    it = iter(refs)
    a_ref = next(it)
    mod_ref = next(it) if norm else None
    w_ref = next(it)
    res_ref = next(it) if residual else None
    gate_ref = next(it) if residual else None
    o_ref = next(it)
    h_ref = next(it) if norm else None
    j = pl.program_id(1)

    if norm:
        @pl.when(j == 0)
        def _():
            h_ref[...] = _bf(_normmod(a_ref[...], mod_ref[0:1], mod_ref[1:2], mod_ref[2:3]))
        lhs = h_ref[...]
    else:
        lhs = a_ref[...]
    acc = _dot(lhs, w_ref[...])
    if residual:
        o_ref[...] = res_ref[...] + gate_ref[0:1] * acc
    elif n_gelu:
        @pl.when(j < n_gelu)
        def _():
            o_ref[...] = _gelu_tanh(acc)

        @pl.when(j >= n_gelu)
        def _():
            o_ref[...] = acc
    else:
        o_ref[...] = acc


def proj(a, w, w_index, *, mod=None, gelu_cols=0, res=None, gate=None, tm_pref=512, tn_pref=512):
    M, K = a.shape
    N = w.shape[-1]
    tm = _tile(M, tm_pref)
    tn = _tile(N, tn_pref)
    norm = mod is not None
    residual = res is not None
    assert gelu_cols % tn == 0
    lead = tuple(w_index)
    in_specs = [pl.BlockSpec((tm, K), lambda i, j: (i, 0))]
    args = [a]
    if norm:
        in_specs.append(pl.BlockSpec((8, K), lambda i, j: (0, 0)))
        args.append(mod)
    in_specs.append(pl.BlockSpec((None,) * len(lead) + (K, tn), lambda i, j: lead + (0, j)))
    args.append(w)
    if residual:
        in_specs.append(pl.BlockSpec((tm, tn), lambda i, j: (i, j)))
        in_specs.append(pl.BlockSpec((8, tn), lambda i, j: (0, j)))
        args += [res, gate]
    return pl.pallas_call(
        functools.partial(_proj_kernel, norm=norm, n_gelu=gelu_cols // tn, residual=residual),
        grid=(M // tm, N // tn),
        in_specs=in_specs,
        out_specs=pl.BlockSpec((tm, tn), lambda i, j: (i, j)),
        out_shape=jax.ShapeDtypeStruct((M, N), F32),
        scratch_shapes=[pltpu.VMEM((tm, K), BF16)] if norm else [],
        compiler_params=_cparams("parallel", "arbitrary"),
        name="proj",
    )(*args)


def _rmsnorm_kernel(x_ref, g_ref, o_ref):
    x = x_ref[...]
    ms = jnp.mean(x * x, axis=-1, keepdims=True)
    o_ref[...] = x * lax.rsqrt(ms + EPS) * g_ref[...]


def final_rmsnorm(x, g):
    M, D = x.shape
    tm = _tile(M, 512)
    return pl.pallas_call(
        _rmsnorm_kernel,
        grid=(M // tm,),
        in_specs=[pl.BlockSpec((tm, D), lambda i: (i, 0)), pl.BlockSpec((1, D), lambda i: (0, 0))],
        out_specs=pl.BlockSpec((tm, D), lambda i: (i, 0)),
        out_shape=jax.ShapeDtypeStruct((M, D), F32),
        compiler_params=_cparams("parallel"),
        name="final_rmsnorm",
    )(x, g.reshape(1, D))


def _lru_kernel(ul_ref, uc_ref, cw_ref, cb_ref, wa_ref, wx_ref, ba_ref, bx_ref, lam_ref, *out_refs, R, need_ctx):
    hl_ref = out_refs[0]
    hc_ref = out_refs[1] if need_ctx else None
    T, W = ul_ref.shape
    Tc = uc_ref.shape[0]
    cw = cw_ref[...]
    cb = cb_ref[...]
    sub = lax.broadcasted_iota(jnp.int32, (R, W), 0) & 7

    def conv_chunk(u_ref, t0, n_rows):
        main = u_ref[pl.ds(t0, R), :]
        p0 = pl.multiple_of(jnp.maximum(t0 - 8, 0), 8)
        n0 = pl.multiple_of(jnp.minimum(t0 + R, n_rows - 8), 8)
        prev = jnp.where(t0 > 0, u_ref[pl.ds(p0, 8), :], 0.0)
        nxt = jnp.where(t0 + R < n_rows, u_ref[pl.ds(n0, 8), :], 0.0)
        ext = jnp.concatenate([prev, main, nxt], axis=0)
        n = R + 16
        out = cb + ext[8:8 + R] * cw[2:3]
        out = out + pltpu.roll(ext, 2, 0)[8:8 + R] * cw[0:1]
        out = out + pltpu.roll(ext, 1, 0)[8:8 + R] * cw[1:2]
        out = out + pltpu.roll(ext, n - 1, 0)[8:8 + R] * cw[3:4]
        return out

    def scan_chunk(v, d, carry, rev):
        vb = _bf(v)
        r = _sigmoid(_dot(vb, wa_ref[d]) + ba_ref[d:d + 1])
        gi = _sigmoid(_dot(vb, wx_ref[d]) + bx_ref[d:d + 1])
        log_a = -LRU_C * r * _softplus(-lam_ref[d:d + 1])
        a = jnp.exp(log_a)
        b = jnp.sqrt(-jnp.tanh(log_a) * (a * a + 1.0)) * (gi * v)
        for s in (1, 2, 4):
            if rev:
                a_s, b_s, m = pltpu.roll(a, R - s, 0), pltpu.roll(b, R - s, 0), sub < 8 - s
            else:
                a_s, b_s, m = pltpu.roll(a, s, 0), pltpu.roll(b, s, 0), sub >= s
            b = jnp.where(m, a * b_s + b, b)
            a = jnp.where(m, a * a_s, a)
        groups = R // 8
        hs = [None] * groups
        for g in (range(groups - 1, -1, -1) if rev else range(groups)):
            h = a[8 * g:8 * g + 8] * carry + b[8 * g:8 * g + 8]
            hs[g] = h
            carry = h[0:1] if rev else h[7:8]
        return jnp.concatenate(hs, axis=0), carry

    def run(u_ref, o_ref, n_rows, d, carry):
        rev = d == 1
        nchunks = n_rows // R

        def body(i, carry):
            c = nchunks - 1 - i if rev else i
            t0 = pl.multiple_of(c * R, R)
            h, carry = scan_chunk(conv_chunk(u_ref, t0, n_rows), d, carry, rev)
            if o_ref is not None:
                if d == 0:
                    o_ref[pl.ds(t0, R), :] = h
                else:
                    o_ref[pl.ds(t0, R), :] += h
            return carry

        return lax.fori_loop(0, nchunks, body, carry)

    for d in (0, 1):
        carry = run(uc_ref, hc_ref, Tc, d, jnp.zeros((1, W), F32))
        run(ul_ref, hl_ref, T, d, carry)


def lru_scan(pl_out, pc_out, conv_w, conv_b, wa, wx, ba, bx, lam, li, need_ctx):
    T = pl_out.shape[0]
    Tc = pc_out.shape[0]
    D = pl_out.shape[1] // 2
    NH, W = wa.shape[2], wa.shape[3]
    assert W % LANES == 0 and NH * W == D
    R = _tile(math.gcd(T, Tc), 256)
    col = D // W
    vec2 = pl.BlockSpec((None, 2, W), lambda h: (li, 0, h))
    out_shape = [jax.ShapeDtypeStruct((T, D), F32)]
    out_specs = [pl.BlockSpec((T, W), lambda h: (0, h))]
    if need_ctx:
        out_shape.append(jax.ShapeDtypeStruct((Tc, D), F32))
        out_specs.append(pl.BlockSpec((Tc, W), lambda h: (0, h)))
    return pl.pallas_call(
        functools.partial(_lru_kernel, R=R, need_ctx=need_ctx),
        grid=(NH,),
        in_specs=[
            pl.BlockSpec((T, W), lambda h: (0, col + h)),
            pl.BlockSpec((Tc, W), lambda h: (0, col + h)),
            pl.BlockSpec((None, 4, W), lambda h: (li, 0, h)),
            pl.BlockSpec((None, 1, W), lambda h: (li, 0, h)),
            pl.BlockSpec((None, 2, None, W, W), lambda h: (li, 0, h, 0, 0)),
            pl.BlockSpec((None, 2, None, W, W), lambda h: (li, 0, h, 0, 0)),
            vec2, vec2, vec2,
        ],
        out_specs=out_specs,
        out_shape=out_shape,
        compiler_params=_cparams("parallel"),
        name="lru_scan",
    )(pl_out, pc_out, conv_w, conv_b.reshape(conv_b.shape[0], 1, D), wa, wx, ba, bx, lam)


def _mul_kernel(a_ref, b_ref, o_ref):
    o_ref[...] = _bf(a_ref[...] * b_ref[...])


def lru_combine(p_out, hs):
    M, D = hs.shape
    tm = _tile(M, 256)
    return pl.pallas_call(
        _mul_kernel,
        grid=(M // tm,),
        in_specs=[pl.BlockSpec((tm, D), lambda i: (i, 0)), pl.BlockSpec((tm, D), lambda i: (i, 0))],
        out_specs=pl.BlockSpec((tm, D), lambda i: (i, 0)),
        out_shape=jax.ShapeDtypeStruct((M, D), BF16),
        compiler_params=_cparams("parallel"),
        name="lru_combine",
    )(p_out, hs)


def _mix_write(h, shifted, mu_ref, o_refs, cols):
    xx = shifted - h
    for n, o_ref in enumerate(o_refs):
        o_ref[:, cols] = _bf(h + xx * mu_ref[n:n + 1, cols])


def _mix_lat_kernel(xm_ref, xp_ref, xn_ref, mod_ref, mu_ref, *o_refs, T):
    tm, D = xm_ref.shape
    q = D // 4
    i = pl.program_id(0)
    g, sh, sc = mod_ref[0:1], mod_ref[1:2], mod_ref[2:3]
    hm = _normmod(xm_ref[...], g, sh, sc)
    hp = _normmod(xp_ref[...], g, sh, sc)
    hn = _normmod(xn_ref[...], g, sh, sc)
    row = lax.broadcasted_iota(jnp.int32, (tm, 1), 0)
    t = i * tm + row
    colpos = row & (GRID_W - 1)
    s0 = slice(0, q)
    left = jnp.where(colpos > 0, pltpu.roll(hm[:, s0], 1, 0), 0.0)
    _mix_write(hm[:, s0], left, mu_ref, o_refs, s0)
    s1 = slice(q, 2 * q)
    right = jnp.where(colpos < GRID_W - 1, pltpu.roll(hm[:, s1], tm - 1, 0), 0.0)
    _mix_write(hm[:, s1], right, mu_ref, o_refs, s1)
    s2 = slice(2 * q, 3 * q)
    up = jnp.concatenate([hp[:, s2], hm[:tm - GRID_W, s2]], axis=0) if tm > GRID_W else hp[:, s2]
    up = jnp.where(t >= GRID_W, up, 0.0)
    _mix_write(hm[:, s2], up, mu_ref, o_refs, s2)
    s3 = slice(3 * q, D)
    down = jnp.concatenate([hm[GRID_W:, s3], hn[:, s3]], axis=0) if tm > GRID_W else hn[:, s3]
    down = jnp.where(t < T - GRID_W, down, 0.0)
    _mix_write(hm[:, s3], down, mu_ref, o_refs, s3)


def _mix_ctx_kernel(x_ref, mod_ref, mu_ref, *o_refs):
    Tc, D = x_ref.shape
    hh = D // 2
    h = _normmod(x_ref[...], mod_ref[0:1], mod_ref[1:2], mod_ref[2:3])
    row = lax.broadcasted_iota(jnp.int32, (Tc, 1), 0)
    s0 = slice(0, hh)
    prev = jnp.where(row > 0, pltpu.roll(h[:, s0], 1, 0), 0.0)
    _mix_write(h[:, s0], prev, mu_ref, o_refs, s0)
    s1 = slice(hh, D)
    nxt = jnp.where(row < Tc - 1, pltpu.roll(h[:, s1], Tc - 1, 0), 0.0)
    _mix_write(h[:, s1], nxt, mu_ref, o_refs, s1)


def rwkv_shiftmix(x, mod, mu, li, grid_tokens):
    M, D = x.shape
    n_out = mu.shape[1]
    mu_spec_args = ((None, n_out, D),)
    out_shape = [jax.ShapeDtypeStruct((M, D), BF16)] * n_out
    if grid_tokens:
        tm = _tile(M, 256)
        assert tm % GRID_W == 0 and M % GRID_W == 0
        r = tm // GRID_W
        nb = M // GRID_W
        return pl.pallas_call(
            functools.partial(_mix_lat_kernel, T=M),
            grid=(M // tm,),
            in_specs=[
                pl.BlockSpec((tm, D), lambda i: (i, 0)),
                pl.BlockSpec((GRID_W, D), lambda i: (jnp.maximum(i * r - 1, 0), 0)),
                pl.BlockSpec((GRID_W, D), lambda i: (jnp.minimum((i + 1) * r, nb - 1), 0)),
                pl.BlockSpec((8, D), lambda i: (0, 0)),
                pl.BlockSpec(*mu_spec_args, lambda i: (li, 0, 0)),
            ],
            out_specs=[pl.BlockSpec((tm, D), lambda i: (i, 0))] * n_out,
            out_shape=out_shape,
            compiler_params=_cparams("parallel"),
            name="rwkv_shiftmix_grid",
        )(x, x, x, mod, mu)
    return pl.pallas_call(
        _mix_ctx_kernel,
        grid=(1,),
        in_specs=[
            pl.BlockSpec((M, D), lambda i: (0, 0)),
            pl.BlockSpec((8, D), lambda i: (0, 0)),
            pl.BlockSpec(*mu_spec_args, lambda i: (li, 0, 0)),
        ],
        out_specs=[pl.BlockSpec((M, D), lambda i: (0, 0))] * n_out,
        out_shape=out_shape,
        compiler_params=_cparams("arbitrary"),
        name="rwkv_shiftmix_seq",
    )(x, mod, mu)


def _lora_kernel(xw_ref, xa_ref, xg_ref, w1_ref, w2_ref, w0_ref, a1_ref, a2_ref, a0_ref, g1_ref, g2_ref,
                 lw_ref, a_ref, g_ref, tw_ref, ta_ref, tg_ref):
    @pl.when(pl.program_id(1) == 0)
    def _():
        for d in (0, 1):
            tw_ref[d] = _bf(jnp.tanh(_dot(xw_ref[...], w1_ref[d])))
            ta_ref[d] = _bf(_dot(xa_ref[...], a1_ref[d]))
        tg_ref[...] = _bf(_sigmoid(_dot(xg_ref[...], g1_ref[...])))

    for d in (0, 1):
        z = w0_ref[d:d + 1] + _dot(tw_ref[d], w2_ref[d])
        lw_ref[d] = -math.exp(-0.5) * _sigmoid(z)
        a_ref[d] = _sigmoid(a0_ref[d:d + 1] + _dot(ta_ref[d], a2_ref[d]))
    g_ref[...] = _dot(tg_ref[...], g2_ref[...])


def rwkv_lora(xw, xa, xg, w1, w2, w0, a1, a2, a0, g1, g2, li):
    M, D = xw.shape
    RW, RA, RG = w1.shape[-1], a1.shape[-1], g1.shape[-1]
    tm = _tile(M, 256)
    tn = _tile(D, 1024)
    row = pl.BlockSpec((tm, D), lambda i, j: (i, 0))
    out2 = pl.BlockSpec((2, tm, tn), lambda i, j: (0, i, j))
    return pl.pallas_call(
        _lora_kernel,
        grid=(M // tm, D // tn),
        in_specs=[
            row, row, row,
            pl.BlockSpec((None, 2, D, RW), lambda i, j: (li, 0, 0, 0)),
            pl.BlockSpec((None, 2, RW, tn), lambda i, j: (li, 0, 0, j)),
            pl.BlockSpec((None, 2, tn), lambda i, j: (li, 0, j)),
            pl.BlockSpec((None, 2, D, RA), lambda i, j: (li, 0, 0, 0)),
            pl.BlockSpec((None, 2, RA, tn), lambda i, j: (li, 0, 0, j)),
            pl.BlockSpec((None, 2, tn), lambda i, j: (li, 0, j)),
            pl.BlockSpec((None, D, RG), lambda i, j: (li, 0, 0)),
            pl.BlockSpec((None, RG, tn), lambda i, j: (li, 0, j)),
        ],
        out_specs=[out2, out2, pl.BlockSpec((tm, tn), lambda i, j: (i, j))],
        out_shape=[jax.ShapeDtypeStruct((2, M, D), F32), jax.ShapeDtypeStruct((2, M, D), F32),
                   jax.ShapeDtypeStruct((M, D), F32)],
        scratch_shapes=[pltpu.VMEM((2, tm, RW), BF16), pltpu.VMEM((2, tm, RA), BF16), pltpu.VMEM((tm, RG), BF16)],
        compiler_params=_cparams("parallel", "arbitrary"),
        name="rwkv_lora",
    )(xw, xa, xg, w1, w2, w0, a1, a2, a0, g1, g2)


def _pair_masks():
    row = lax.broadcasted_iota(jnp.int32, (LANES, LANES), 0)
    col = lax.broadcasted_iota(jnp.int32, (LANES, LANES), 1)
    same_head = (row < WKV_HEAD) == (col < WKV_HEAD)
    return row, col, same_head


def _wkv_kernel(r_ref, k_ref, v_ref, lw_ref, a_ref, par_ref, s0_ref, y_ref, z_ref, s_ref,
                st_ref, rh_ref, y0_ref, g_ref, j_ref, et_ref, *, rev):
    C = CHUNK
    Tt = r_ref.shape[0]
    nchunks = Tt // C

    @pl.when(pl.program_id(1) == 0)
    def _():
        st_ref[...] = s0_ref[...]

    k_k, k_a, r_k = par_ref[0:1], par_ref[1:2], par_ref[2:3]
    row, col, same_head = _pair_masks()
    ones_head = jnp.where(same_head, 1.0, 0.0).astype(BF16)
    tr, tc = row & (C - 1), col & (C - 1)
    before = (tc > tr) if rev else (tc < tr)
    before_eq = (tc >= tr) if rev else (tc <= tr)
    r64 = lax.broadcasted_iota(jnp.int32, (C, C), 0)
    c64 = lax.broadcasted_iota(jnp.int32, (C, C), 1)
    cum_m = jnp.where((c64 >= r64) if rev else (c64 <= r64), 1.0, 0.0).astype(BF16)
    eye = jnp.where(row == col, 1.0, 0.0)
    lane = lax.broadcasted_iota(jnp.int32, (C, LANES), 1)
    head_a = lane < WKV_HEAD

    def stack2f(x):
        return jnp.concatenate([jnp.where(head_a, x, 0.0), jnp.where(head_a, 0.0, x)], axis=0)

    def unstack(x2):
        return x2[0:C] + x2[C:2 * C]

    blk16 = (row >> 4) == (col >> 4)
    blk32 = (row >> 5) == (col >> 5)

    def each(f, *lists):
        return [f(*xs) for xs in zip(*lists)]

    def tri_inverse(a_kb):
        m = each(lambda x: jnp.where(blk16, -x, 0.0), a_kb)
        t = each(lambda x: eye + x, m)
        for _ in range(3):
            m = each(lambda x: _dot(_bf(x), _bf(x)), m)
            t = each(lambda x, y: x + _dot(_bf(x), _bf(y)), t, m)
        for inside, outside in ((blk32, blk16), (same_head, blk32)):
            sel = jnp.logical_and(inside, jnp.logical_not(outside))
            tb = each(_bf, t)
            lt = each(lambda x, y: _bf(_dot(_bf(jnp.where(sel, x, 0.0)), y)), a_kb, tb)
            t = each(lambda x, y, w: x - _dot(y, w), t, tb, lt)
        return each(_bf, t)

    sls = [pl.ds(c * C, C) for c in range(nchunks)]
    r, k, v, lw, a = ([ref[sl, :] for sl in sls] for ref in (r_ref, k_ref, v_ref, lw_ref, a_ref))
    kk = each(lambda x: x * k_k, k)
    nrm = each(lambda x: _exact_right(x * x, ones_head, 2), kk)
    kk = each(lambda x, y: x * lax.rsqrt(jnp.maximum(y, 1e-24)), kk, nrm)
    kd = each(lambda x, y: x * (1.0 + (y - 1.0) * k_a), k, a)
    beta = each(lambda x, y: x * y, a, kk)
    bonus = each(lambda x, y: _exact_right(x * y * r_k, ones_head, 2), r, kd)
    for sl, x, y in zip(sls, bonus, v):
        z_ref[sl, :] = x * y
    b = each(lambda x: _exact_left(cum_m, x, 3), lw)
    b_tot = each(lambda x: x[0:1] if rev else x[C - 1:C], b)
    e_b = each(jnp.exp, b)
    e_nb = each(lambda x: jnp.exp(-x), b)
    e_rest = each(lambda x, y: jnp.exp(y - x), b, b_tot)
    r2f = each(lambda x, y: stack2f(x * y), r, e_b)
    r2 = each(_bf, r2f)
    kap2 = each(lambda x, y, w: _bf(stack2f(x * jnp.exp(y - w))), kk, b, lw)
    k2 = each(lambda x, y: _bf(stack2f(x * y)), kd, e_nb)
    be2 = each(lambda x, y: _bf(stack2f(x * y)), beta, e_nb)
    v2 = each(lambda x: _bf(stack2f(x)), v)
    kc2 = each(lambda x, y: _bf(stack2f(x * y)), kd, e_rest)
    bc2 = each(lambda x, y: _bf(stack2f(x * y)), beta, e_rest)
    a_kb = each(lambda x, y: jnp.where(before, _dot_nt(x, y), 0.0), kap2, be2)
    a_kk = each(lambda x, y: _bf(jnp.where(before, _dot_nt(x, y), 0.0)), kap2, k2)
    a_rk = each(lambda x, y: _bf(jnp.where(before_eq, _dot_nt(x, y), 0.0)), r2, k2)
    a_rb = each(lambda x, y: _bf(jnp.where(before_eq, _dot_nt(x, y), 0.0)), r2, be2)
    tinv = tri_inverse(a_kb)
    w0 = each(lambda x, y: _bf(_dot(x, y)), a_kk, v2)
    kh = each(lambda x, y: _bf(_dot(x, y)), tinv, kap2)
    u0 = each(lambda x, y: _bf(_dot(x, y)), tinv, w0)
    rh = each(lambda x, y, w: _bf(x - _dot(y, w)), r2f, a_rb, kh)
    y0 = each(lambda x, y, w, q: _dot(x, y) - _dot(w, q), a_rk, v2, a_rb, u0)
    jj = each(lambda x, y, w, q: _dot_tn(x, y) - _dot_tn(w, q), v2, kc2, u0, bc2)
    gg = each(lambda x, y: _bf(-_dot_tn(x, y)), kh, bc2)
    for c in range(nchunks):
        rh_ref[c] = rh[c]
        y0_ref[c] = y0[c]
        j_ref[c] = jj[c]
        g_ref[c] = gg[c]
        et_ref[c] = jnp.broadcast_to(jnp.exp(b_tot[c]), (8, LANES))

    S = st_ref[...]
    for c in (range(nchunks - 1, -1, -1) if rev else range(nchunks)):
        Sb = _bf(S)
        y_ref[pl.ds(c * C, C), :] = unstack(_dot_nt(rh_ref[c], Sb) + y0_ref[c])
        S = S * et_ref[c, 0:1, :] + _dot(Sb, g_ref[c]) + j_ref[c]
    st_ref[...] = S

    @pl.when(pl.program_id(1) == pl.num_programs(1) - 1)
    def _():
        s_ref[...] = S


def wkv_scan(r, k, v, lw, a, par, s0, d):
    T, D = r.shape
    HP = D // LANES
    Tt = _tile(T, 512)
    NT = T // Tt
    nck = Tt // CHUNK
    rev = d == 1

    def tt(t):
        return NT - 1 - t if rev else t

    tok = pl.BlockSpec((Tt, LANES), lambda h, t: (tt(t), h))
    tok_d = pl.BlockSpec((None, Tt, LANES), lambda h, t: (d, tt(t), h))
    st = pl.BlockSpec((None, LANES, LANES), lambda h, t: (h, 0, 0))
    return pl.pallas_call(
        functools.partial(_wkv_kernel, rev=rev),
        grid=(HP, NT),
        in_specs=[tok, tok, tok, tok_d, tok_d, pl.BlockSpec((8, LANES), lambda h, t: (0, h)), st],
        out_specs=[tok, tok, st],
        out_shape=[jax.ShapeDtypeStruct((T, D), F32), jax.ShapeDtypeStruct((T, D), F32),
                   jax.ShapeDtypeStruct((HP, LANES, LANES), F32)],
        scratch_shapes=[
            pltpu.VMEM((LANES, LANES), F32),
            pltpu.VMEM((nck, LANES, LANES), BF16),
            pltpu.VMEM((nck, LANES, LANES), F32),
            pltpu.VMEM((nck, LANES, LANES), BF16),
            pltpu.VMEM((nck, LANES, LANES), F32),
            pltpu.VMEM((nck, 8, LANES), F32),
        ],
        compiler_params=_cparams("parallel", "arbitrary"),
        name="wkv_scan",
    )(r, k, v, lw, a, par, s0)


def _rwkv_combine_kernel(y0_ref, y1_ref, z0_ref, z1_ref, g_ref, ln_ref, o_ref):
    D = o_ref.shape[1]
    row = lax.broadcasted_iota(jnp.int32, (LANES, LANES), 0)
    col = lax.broadcasted_iota(jnp.int32, (LANES, LANES), 1)
    mean_m = jnp.where((row < WKV_HEAD) == (col < WKV_HEAD), 1.0 / WKV_HEAD, 0.0).astype(BF16)
    for c in range(D // LANES):
        cs = slice(c * LANES, (c + 1) * LANES)
        y = y0_ref[:, cs] + y1_ref[:, cs]
        yc = y - _exact_right(y, mean_m, 2)
        var = _exact_right(yc * yc, mean_m, 2)
        out = yc * lax.rsqrt(var + GN_EPS) * ln_ref[0:1, cs] + ln_ref[1:2, cs] + (z0_ref[:, cs] + z1_ref[:, cs])
        o_ref[:, cs] = _bf(out * g_ref[:, cs])


def rwkv_combine(y0, y1, z0, z1, g, ln):
    M, D = g.shape
    tm = _tile(M, 256)
    tok = pl.BlockSpec((tm, D), lambda i: (i, 0))
    return pl.pallas_call(
        _rwkv_combine_kernel,
        grid=(M // tm,),
        in_specs=[tok] * 5 + [pl.BlockSpec((8, D), lambda i: (0, 0))],
        out_specs=tok,
        out_shape=jax.ShapeDtypeStruct((M, D), BF16),
        compiler_params=_cparams("parallel"),
        name="rwkv_combine",
    )(y0, y1, z0, z1, g, ln)


def _gla_kernel(q_ref, i_ref, f_ref, lbl_ref, s0_ref, o_ref, s_ref, st_ref, qe_ref, j_ref, et_ref, *, rev, layer):
    C = CHUNK
    Tt = q_ref.shape[0]
    nchunks = Tt // C
    nsub = C // SUB

    @pl.when(pl.program_id(1) == 0)
    def _():
        st_ref[...] = s0_ref[...]

    logits = lbl_ref[...]
    e = jnp.exp(logits - jnp.max(logits, axis=0, keepdims=True))
    p = e / jnp.sum(e, axis=0, keepdims=True)
    lb = jnp.zeros((1, LANES), F32)
    for l in range(1, layer + 1):
        lb = lb + p[l:l + 1]
    log_lb = jnp.log(lb)
    log_1m = jnp.log1p(-lb)

    r64 = lax.broadcasted_iota(jnp.int32, (C, C), 0)
    c64 = lax.broadcasted_iota(jnp.int32, (C, C), 1)
    cum_m = jnp.where((c64 >= r64) if rev else (c64 <= r64), 1.0, 0.0).astype(BF16)
    rows = lax.broadcasted_iota(jnp.int32, (C, 1), 0)
    lane_s = lax.broadcasted_iota(jnp.int32, (SUB, C), 1)
    row_s = lax.broadcasted_iota(jnp.int32, (SUB, 1), 0)

    def each(f, *lists):
        return [f(*xs) for xs in zip(*lists)]

    sls = [pl.ds(c * C, C) for c in range(nchunks)]
    q = [_silu(q_ref[sl, :]) for sl in sls]
    v = [i_ref[sl, :] for sl in sls]
    f = [f_ref[sl, :] for sl in sls]
    kg = each(lambda x: (1.0 - lb) * _sigmoid(-x), f)
    x2 = each(lambda x: log_1m - _softplus(-x), f)
    g = each(lambda x: jnp.maximum(log_lb, x) + jnp.log1p(jnp.exp(-jnp.abs(log_lb - x))), x2)
    b = each(lambda x: _exact_left(cum_m, x, 3), g)
    b_tot = each(lambda x: x[0:1] if rev else x[C - 1:C], b)
    vb = each(_bf, v)
    att_rows = [[] for _ in range(nchunks)]
    for I in range(nsub):
        lo = I * SUB
        qI = each(lambda x: x[lo:lo + SUB], q)
        bI = each(lambda x: x[lo:lo + SUB], b)
        first = (I == nsub - 1) if rev else (I == 0)
        if first:
            att = [jnp.zeros((SUB, C), F32)] * nchunks
        else:
            ref = each(lambda x: x[lo + SUB:lo + SUB + 1] if rev else x[lo - 1:lo], b)
            earlier = (rows >= lo + SUB) if rev else (rows < lo)
            kt = each(lambda x, y, w: _bf(jnp.where(earlier, x * jnp.exp(w - y), 0.0)), kg, b, ref)
            qt = each(lambda x, y, w: _bf(x * jnp.exp(y - w)), qI, bI, ref)
            att = each(_dot_nt, qt, kt)
        for j in range(SUB):
            s = lo + j
            place = jnp.logical_and(lane_s == s, (row_s <= j) if rev else (row_s >= j))
            pj = each(lambda x, y, w, u: x * y[s:s + 1] * jnp.exp(w - u[s:s + 1]), qI, kg, bI, b)
            att = each(lambda x, y: x + jnp.where(place, jnp.sum(y, axis=-1, keepdims=True), 0.0), att, pj)
        for c in range(nchunks):
            att_rows[c].append(att[c])
    o_in = each(lambda x, y: _dot(_bf(jnp.concatenate(x, axis=0)), y), att_rows, vb)
    for c in range(nchunks):
        o_ref[sls[c], :] = o_in[c]
        qe_ref[c] = _bf(q[c] * jnp.exp(b[c]))
        j_ref[c] = _dot_tn(vb[c], _bf(kg[c] * jnp.exp(b_tot[c] - b[c])))
        et_ref[c] = jnp.broadcast_to(jnp.exp(b_tot[c]), (8, LANES))

    Z = st_ref[...]
    for c in (range(nchunks - 1, -1, -1) if rev else range(nchunks)):
        o_ref[sls[c], :] += _dot_nt(qe_ref[c], _bf(Z))
        Z = Z * et_ref[c, 0:1, :] + j_ref[c]
    st_ref[...] = Z

    @pl.when(pl.program_id(1) == pl.num_programs(1) - 1)
    def _():
        s_ref[...] = Z


def gla_scan(raw, lb_logits, s0, d, layer):
    T = raw.shape[0]
    D = raw.shape[1] // 5
    H = D // LANES
    Tt = _tile(T, 256)
    NT = T // Tt
    nck = Tt // CHUNK
    rev = d == 1
    L = lb_logits.shape[0]

    def tt(t):
        return NT - 1 - t if rev else t

    st = pl.BlockSpec((None, LANES, LANES), lambda h, t: (h, 0, 0))
    return pl.pallas_call(
        functools.partial(_gla_kernel, rev=rev, layer=layer),
        grid=(H, NT),
        in_specs=[
            pl.BlockSpec((Tt, LANES), lambda h, t: (tt(t), h)),
            pl.BlockSpec((Tt, LANES), lambda h, t: (tt(t), H + h)),
            pl.BlockSpec((Tt, LANES), lambda h, t: (tt(t), (2 + d) * H + h)),
            pl.BlockSpec((L, LANES), lambda h, t: (0, h)),
            st,
        ],
        out_specs=[pl.BlockSpec((Tt, LANES), lambda h, t: (tt(t), h)), st],
        out_shape=[jax.ShapeDtypeStruct((T, D), F32), jax.ShapeDtypeStruct((H, LANES, LANES), F32)],
        scratch_shapes=[
            pltpu.VMEM((LANES, LANES), F32),
            pltpu.VMEM((nck, CHUNK, LANES), BF16),
            pltpu.VMEM((nck, LANES, LANES), F32),
            pltpu.VMEM((nck, 8, LANES), F32),
        ],
        compiler_params=_cparams("parallel", "arbitrary"),
        name="gla_scan",
    )(raw, raw, raw, lb_logits, s0)


def _hgrn_combine_kernel(o0_ref, o1_ref, g_ref, gn_ref, o_ref):
    D = o_ref.shape[1]
    for c in range(D // LANES):
        cs = slice(c * LANES, (c + 1) * LANES)
        o = o0_ref[:, cs] + o1_ref[:, cs]
        o = o * lax.rsqrt(jnp.mean(o * o, axis=-1, keepdims=True) + EPS) * gn_ref[0:1, cs]
        o_ref[:, cs] = _bf(o * _silu(g_ref[:, cs]))


def hgrn_combine(o0, o1, raw, gn):
    M, D = o0.shape
    tm = _tile(M, 256)
    tok = pl.BlockSpec((tm, D), lambda i: (i, 0))
    return pl.pallas_call(
        _hgrn_combine_kernel,
        grid=(M // tm,),
        in_specs=[tok, tok, pl.BlockSpec((tm, D), lambda i: (i, 4)), pl.BlockSpec((1, D), lambda i: (0, 0))],
        out_specs=tok,
        out_shape=jax.ShapeDtypeStruct((M, D), BF16),
        compiler_params=_cparams("parallel"),
        name="hgrn_combine",
    )(o0, o1, raw, gn)


def _rows8(*vecs):
    D = vecs[0].shape[-1]
    rows = [v.reshape(1, D) for v in vecs]
    rows.append(jnp.zeros((8 - len(rows), D), F32))
    return jnp.concatenate(rows, axis=0)


def kernel(x, c, ctx, c_ctx, ada_down, ada_up, ada_b, norm_g, ffn_w13, ffn_w2, final_g, lru_w_in, lru_conv_w, lru_conv_b, lru_gate_a_w, lru_gate_a_b, lru_gate_x_w, lru_gate_x_b, lru_lam, lru_w_out, rwkv_mu, rwkv_w_r, rwkv_w_k, rwkv_w_v, rwkv_w_o, rwkv_w0, rwkv_w1, rwkv_w2, rwkv_a0, rwkv_a1, rwkv_a2, rwkv_g1, rwkv_g2, rwkv_k_k, rwkv_k_a, rwkv_r_k, rwkv_ln_w, rwkv_ln_b, hgrn_w_in, hgrn_lb_logits, hgrn_gn_g, hgrn_w_out):
    B, T, D = x.shape
    assert B == 1, "one sequence per call"
    depth = ada_down.shape[0]
    xl, xc = x[0], ctx[0]

    mods = ada_all_layers(_rows8(c[0], c_ctx), ada_down, ada_up, ada_b)
    mods = mods[:, :2].reshape(depth, 2, N_MOD, D)

    w13, w2 = _bf(ffn_w13), _bf(ffn_w2)
    lru_in, lru_out = _bf(lru_w_in), _bf(lru_w_out)
    lru_wa, lru_wx = _bf(lru_gate_a_w), _bf(lru_gate_x_w)
    w_r, w_k, w_v, w_o = _bf(rwkv_w_r), _bf(rwkv_w_k), _bf(rwkv_w_v), _bf(rwkv_w_o)
    lw1, lw2, la1, la2 = _bf(rwkv_w1), _bf(rwkv_w2), _bf(rwkv_a1), _bf(rwkv_a2)
    lg1, lg2 = _bf(rwkv_g1), _bf(rwkv_g2)
    h_in, h_out = _bf(hgrn_w_in), _bf(hgrn_w_out)

    for i in range(depth):
        need_ctx = i < depth - 1
        kind, j = i % 3, i // 3
        ml, mc = mods[i, 0], mods[i, 1]

        def ffn(xs, m, k, which):
            return half_ffn(xs, _rows8(norm_g[i, 2 * which], m[k], m[k + 1], m[k + 2]), w13, w2, i, which)

        xl = ffn(xl, ml, 0, 0)
        xc = ffn(xc, mc, 0, 0)
        mod_l = _rows8(norm_g[i, 1], ml[3], ml[4])
        mod_c = _rows8(norm_g[i, 1], mc[3], mc[4])
        gate_l, gate_c = _rows8(ml[5]), _rows8(mc[5])

        if kind == 0:
            p_l = proj(xl, lru_in, (j,), mod=mod_l, gelu_cols=D)
            p_c = proj(xc, lru_in, (j,), mod=mod_c, gelu_cols=D)
            hs = lru_scan(p_l, p_c, lru_conv_w, lru_conv_b, lru_wa, lru_wx, lru_gate_a_b, lru_gate_x_b, lru_lam,
                          j, need_ctx)
            a_l = lru_combine(p_l, hs[0])
            a_c = lru_combine(p_c, hs[1]) if need_ctx else None
            w_last, w_idx = lru_out, (j,)
        elif kind == 1:
            par = _rows8(rwkv_k_k[j], rwkv_k_a[j], rwkv_r_k[j].reshape(D))
            ln = _rows8(rwkv_ln_w[j], rwkv_ln_b[j])
            outs = []
            state = [jnp.zeros((D // LANES, LANES, LANES), F32)] * 2
            for xs, mod, is_grid in ((xc, mod_c, False), (xl, mod_l, True)):
                xr, xw, xk, xv, xa, xg = rwkv_shiftmix(xs, mod, rwkv_mu, j, is_grid)
                r = proj(xr, w_r, (j,))
                k = proj(xk, w_k, (j,))
                v = proj(xv, w_v, (j,))
                lw, a, g = rwkv_lora(xw, xa, xg, lw1, lw2, rwkv_w0, la1, la2, rwkv_a0, lg1, lg2, j)
                ys, zs = [], []
                for d in (0, 1):
                    y, z, state[d] = wkv_scan(r, k, v, lw, a, par, state[d], d)
                    ys.append(y)
                    zs.append(z)
                outs.append(rwkv_combine(ys[0], ys[1], zs[0], zs[1], g, ln))
            a_c, a_l = outs
            w_last, w_idx = w_o, (j,)
        else:
            outs = []
            state = [jnp.zeros((D // LANES, LANES, LANES), F32)] * 2
            for xs, mod in ((xc, mod_c), (xl, mod_l)):
                raw = proj(xs, h_in, (j,), mod=mod)
                os_ = []
                for d in (0, 1):
                    o, state[d] = gla_scan(raw, hgrn_lb_logits, state[d], d, i)
                    os_.append(o)
                outs.append(hgrn_combine(os_[0], os_[1], raw, hgrn_gn_g[j].reshape(1, D)))
            a_c, a_l = outs
            w_last, w_idx = h_out, (j,)

        xl = proj(a_l, w_last, w_idx, res=xl, gate=gate_l)
        xl = ffn(xl, ml, 6, 1)
        if need_ctx:
            xc = proj(a_c, w_last, w_idx, res=xc, gate=gate_c)
            xc = ffn(xc, mc, 6, 1)

    return final_rmsnorm(xl, final_g)[None]
```

```python
import functools
import math

import jax
import jax.numpy as jnp
from jax import lax
from jax.experimental import pallas as pl
from jax.experimental.pallas import tpu as pltpu

F32 = jnp.float32
BF16 = jnp.bfloat16

EPS = 1e-6
GN_EPS = 64e-5
LRU_C = 8.0
GRID_W = 64
N_MOD = 9
WKV_HEAD = 64
GLA_HEAD = 128
CHUNK = 64
SUB = 16
LANES = 128
VMEM_LIMIT_BYTES = 56 * 1024 * 1024


def _cparams(*sem):
    return pltpu.CompilerParams(dimension_semantics=sem, vmem_limit_bytes=VMEM_LIMIT_BYTES)


def _tile(n, pref):
    if n <= pref:
        return n
    for t in range(pref, 7, -1):
        if n % t == 0 and t % 8 == 0:
            return t
    return n


def _bf(x):
    return x.astype(BF16)


def _dot(a, b):
    return jnp.dot(a, b, preferred_element_type=F32)


def _dot_nt(a, b):
    return lax.dot_general(a, b, (((1,), (1,)), ((), ())), preferred_element_type=F32)


def _dot_tn(a, b):
    return lax.dot_general(a, b, (((0,), (0,)), ((), ())), preferred_element_type=F32)


def _split_terms(x, terms):
    out, rem = [], x
    for _ in range(terms):
        p = _bf(rem)
        out.append(p)
        rem = rem - p.astype(F32)
    return out


def _exact_left(m_bf, x, terms):
    acc = None
    for p in _split_terms(x, terms):
        d = _dot(m_bf, p)
        acc = d if acc is None else acc + d
    return acc


def _exact_right(x, m_bf, terms):
    acc = None
    for p in _split_terms(x, terms):
        d = _dot(p, m_bf)
        acc = d if acc is None else acc + d
    return acc


def _sigmoid(x):
    return 0.5 * jnp.tanh(0.5 * x) + 0.5


def _silu(x):
    return x * _sigmoid(x)


def _gelu_tanh(x):
    return 0.5 * x * (1.0 + jnp.tanh(math.sqrt(2.0 / math.pi) * (x + 0.044715 * (x * x * x))))


def _softplus(x):
    return jnp.maximum(x, 0.0) + jnp.log1p(jnp.exp(-jnp.abs(x)))


def _normmod(x, g, shift, scale):
    ms = jnp.mean(x * x, axis=-1, keepdims=True)
    return (x * lax.rsqrt(ms + EPS) * g) * (1.0 + scale) + shift


def _ada_kernel(cc_ref, down_ref, up_ref, b_ref, o_ref, t_ref):
    hi = lax.Precision.HIGHEST

    @pl.when(pl.program_id(1) == 0)
    def _():
        t_ref[...] = jnp.dot(_silu(cc_ref[...]), down_ref[...], precision=hi, preferred_element_type=F32)

    o_ref[...] = jnp.dot(t_ref[...], up_ref[...], precision=hi, preferred_element_type=F32) + b_ref[...]


def ada_all_layers(cc, down, up, bias):
    L, D, R = down.shape
    N = up.shape[2]
    tn = _tile(N, 4096)
    return pl.pallas_call(
        _ada_kernel,
        grid=(L, N // tn),
        in_specs=[
            pl.BlockSpec((8, D), lambda l, j: (0, 0)),
            pl.BlockSpec((None, D, R), lambda l, j: (l, 0, 0)),
            pl.BlockSpec((None, R, tn), lambda l, j: (l, 0, j)),
            pl.BlockSpec((None, 1, tn), lambda l, j: (l, 0, j)),
        ],
        out_specs=pl.BlockSpec((None, 8, tn), lambda l, j: (l, 0, j)),
        out_shape=jax.ShapeDtypeStruct((L, 8, N), F32),
        scratch_shapes=[pltpu.VMEM((8, R), F32)],
        compiler_params=_cparams("parallel", "arbitrary"),
        name="ada",
    )(cc, down, up, bias.reshape(L, 1, N))


def _ffn_kernel(x_ref, mod_ref, w1_ref, w3_ref, w2_ref, o_ref, h_ref):
    f = pl.program_id(1)

    @pl.when(f == 0)
    def _():
        h_ref[...] = _bf(_normmod(x_ref[...], mod_ref[0:1], mod_ref[1:2], mod_ref[2:3]))
        o_ref[...] = jnp.zeros_like(o_ref)

    h = h_ref[...]
    act = _bf(_silu(_dot(h, w1_ref[...])) * _dot(h, w3_ref[...]))
    o_ref[...] += _dot(act, w2_ref[...])

    @pl.when(f == pl.num_programs(1) - 1)
    def _():
        o_ref[...] = x_ref[...] + 0.5 * mod_ref[3:4] * o_ref[...]


def half_ffn(x, mod, w13, w2, li, ki):
    M, D = x.shape
    F = w2.shape[2]
    tm = _tile(M, 512)
    tf = _tile(F, 256)
    nf = F // tf
    return pl.pallas_call(
        _ffn_kernel,
        grid=(M // tm, nf),
        in_specs=[
            pl.BlockSpec((tm, D), lambda i, f: (i, 0), pipeline_mode=pl.Buffered(1)),
            pl.BlockSpec((8, D), lambda i, f: (0, 0)),
            pl.BlockSpec((None, None, D, tf), lambda i, f: (li, ki, 0, f)),
            pl.BlockSpec((None, None, D, tf), lambda i, f: (li, ki, 0, nf + f)),
            pl.BlockSpec((None, None, tf, D), lambda i, f: (li, ki, f, 0)),
        ],
        out_specs=pl.BlockSpec((tm, D), lambda i, f: (i, 0)),
        out_shape=jax.ShapeDtypeStruct((M, D), F32),
        scratch_shapes=[pltpu.VMEM((tm, D), BF16)],
        compiler_params=_cparams("parallel", "arbitrary"),
        name="half_ffn",
    )(x, mod, w13, w13, w2)


def _proj_kernel(*refs, norm, n_gelu, residual):
    it = iter(refs)
    a_ref = next(it)
    mod_ref = next(it) if norm else None
    w_ref = next(it)
    res_ref = next(it) if residual else None
    gate_ref = next(it) if residual else None
    o_ref = next(it)
    h_ref = next(it) if norm else None
    j = pl.program_id(1)

    if norm:
        @pl.when(j == 0)
        def _():
            h_ref[...] = _bf(_normmod(a_ref[...], mod_ref[0:1], mod_ref[1:2], mod_ref[2:3]))
        lhs = h_ref[...]
    else:
        lhs = a_ref[...]
    acc = _dot(lhs, w_ref[...])
    if residual:
        o_ref[...] = res_ref[...] + gate_ref[0:1] * acc
    elif n_gelu:
        @pl.when(j < n_gelu)
        def _():
            o_ref[...] = _gelu_tanh(acc)

        @pl.when(j >= n_gelu)
        def _():
            o_ref[...] = acc
    else:
        o_ref[...] = acc


def proj(a, w, w_index, *, mod=None, gelu_cols=0, res=None, gate=None, tm_pref=512, tn_pref=1024):
    M, K = a.shape
    N = w.shape[-1]
    tm = _tile(M, tm_pref)
    tn = _tile(math.gcd(N, gelu_cols) if gelu_cols else N, tn_pref)
    norm = mod is not None
    residual = res is not None
    assert gelu_cols % tn == 0
    lead = tuple(w_index)
    in_specs = [pl.BlockSpec((tm, K), lambda i, j: (i, 0))]
    args = [a]
    if norm:
        in_specs.append(pl.BlockSpec((8, K), lambda i, j: (0, 0)))
        args.append(mod)
    in_specs.append(pl.BlockSpec((None,) * len(lead) + (K, tn), lambda i, j: lead + (0, j)))
    args.append(w)
    if residual:
        in_specs.append(pl.BlockSpec((tm, tn), lambda i, j: (i, j)))
        in_specs.append(pl.BlockSpec((8, tn), lambda i, j: (0, j)))
        args += [res, gate]
    return pl.pallas_call(
        functools.partial(_proj_kernel, norm=norm, n_gelu=gelu_cols // tn, residual=residual),
        grid=(M // tm, N // tn),
        in_specs=in_specs,
        out_specs=pl.BlockSpec((tm, tn), lambda i, j: (i, j)),
        out_shape=jax.ShapeDtypeStruct((M, N), F32),
        scratch_shapes=[pltpu.VMEM((tm, K), BF16)] if norm else [],
        compiler_params=_cparams("parallel", "arbitrary"),
        name="proj",
    )(*args)


def _rmsnorm_kernel(x_ref, g_ref, o_ref):
    x = x_ref[...]
    ms = jnp.mean(x * x, axis=-1, keepdims=True)
    o_ref[...] = x * lax.rsqrt(ms + EPS) * g_ref[...]


def final_rmsnorm(x, g):
    M, D = x.shape
    tm = _tile(M, 512)
    return pl.pallas_call(
        _rmsnorm_kernel,
        grid=(M // tm,),
        in_specs=[pl.BlockSpec((tm, D), lambda i: (i, 0)), pl.BlockSpec((1, D), lambda i: (0, 0))],
        out_specs=pl.BlockSpec((tm, D), lambda i: (i, 0)),
        out_shape=jax.ShapeDtypeStruct((M, D), F32),
        compiler_params=_cparams("parallel"),
        name="final_rmsnorm",
    )(x, g.reshape(1, D))


def _lru_kernel(ul_ref, uc_ref, cw_ref, cb_ref, wa_ref, wx_ref, ba_ref, bx_ref, lam_ref, *refs, R, need_ctx):
    hl_ref = refs[0]
    hc_ref = refs[1] if need_ctx else None
    v_ref, hs_ref = refs[-2:]
    T, W = ul_ref.shape
    NL = W // LANES
    Tc = uc_ref.shape[0]
    cw = cw_ref[...]
    cb = cb_ref[...]

    def conv_chunk(u_ref, t0, n_rows):
        main = u_ref[pl.ds(t0, R), :]
        p0 = pl.multiple_of(jnp.maximum(t0 - 8, 0), 8)
        n0 = pl.multiple_of(jnp.minimum(t0 + R, n_rows - 8), 8)
        prev = jnp.where(t0 > 0, u_ref[pl.ds(p0, 8), :], 0.0)
        nxt = jnp.where(t0 + R < n_rows, u_ref[pl.ds(n0, 8), :], 0.0)
        ext = jnp.concatenate([prev, main, nxt], axis=0)
        n = R + 16
        out = cb + ext[8:8 + R] * cw[2:3]
        out = out + pltpu.roll(ext, 2, 0)[8:8 + R] * cw[0:1]
        out = out + pltpu.roll(ext, 1, 0)[8:8 + R] * cw[1:2]
        out = out + pltpu.roll(ext, n - 1, 0)[8:8 + R] * cw[3:4]
        return out

    G = R // 8

    def scan_chunk(d, carry, rev):
        vp = jnp.concatenate(
            [jnp.concatenate([v_ref[j, pl.ds(g, 8, stride=G), :] for j in range(NL)], axis=1) for g in range(G)],
            axis=0)
        vb = _bf(vp)
        r = _sigmoid(_dot(vb, wa_ref[d]) + ba_ref[d:d + 1])
        gi = _sigmoid(_dot(vb, wx_ref[d]) + bx_ref[d:d + 1])
        log_a = -LRU_C * r * _softplus(-lam_ref[d:d + 1])
        a = jnp.exp(log_a)
        b = jnp.sqrt(-jnp.tanh(log_a) * (a * a + 1.0)) * (gi * vp)
        h = jnp.zeros((8, W), F32)
        p = jnp.ones((8, W), F32)
        hs, ps = [None] * G, [None] * G
        for g in (range(G - 1, -1, -1) if rev else range(G)):
            ag = a[8 * g:8 * g + 8]
            h = ag * h + b[8 * g:8 * g + 8]
            p = ag * p
            hs[g], ps[g] = h, p
        starts = [None] * 8
        for s in (range(7, -1, -1) if rev else range(8)):
            starts[s] = carry
            carry = h[s:s + 1] + p[s:s + 1] * carry
        start = jnp.concatenate(starts, axis=0)
        return [hs[g] + ps[g] * start for g in range(G)], carry

    def run(u_ref, o_ref, n_rows, d, carry):
        rev = d == 1
        nchunks = n_rows // R

        def body(i, carry):
            c = nchunks - 1 - i if rev else i
            t0 = pl.multiple_of(c * R, R)
            v = conv_chunk(u_ref, t0, n_rows)
            for j in range(NL):
                v_ref[j] = v[:, j * LANES:(j + 1) * LANES]
            hs, carry = scan_chunk(d, carry, rev)
            if o_ref is not None:
                for g in range(G):
                    for j in range(NL):
                        hs_ref[j, pl.ds(g, 8, stride=G), :] = hs[g][:, j * LANES:(j + 1) * LANES]
                h = jnp.concatenate([hs_ref[j] for j in range(NL)], axis=1)
                if d == 0:
                    o_ref[pl.ds(t0, R), :] = h
                else:
                    o_ref[pl.ds(t0, R), :] += h
            return carry

        return lax.fori_loop(0, nchunks, body, carry)

    for d in (0, 1):
        carry = run(uc_ref, hc_ref, Tc, d, jnp.zeros((1, W), F32))
        run(ul_ref, hl_ref, T, d, carry)


def lru_scan(pl_out, pc_out, conv_w, conv_b, wa, wx, ba, bx, lam, li, need_ctx):
    T = pl_out.shape[0]
    Tc = pc_out.shape[0]
    D = pl_out.shape[1] // 2
    NH, W = wa.shape[2], wa.shape[3]
    assert W % LANES == 0 and NH * W == D
    R = _tile(math.gcd(T, Tc), 256)
    col = D // W
    vec2 = pl.BlockSpec((None, 2, W), lambda h: (li, 0, h))
    out_shape = [jax.ShapeDtypeStruct((T, D), F32)]
    out_specs = [pl.BlockSpec((T, W), lambda h: (0, h))]
    if need_ctx:
        out_shape.append(jax.ShapeDtypeStruct((Tc, D), F32))
        out_specs.append(pl.BlockSpec((Tc, W), lambda h: (0, h)))
    return pl.pallas_call(
        functools.partial(_lru_kernel, R=R, need_ctx=need_ctx),
        grid=(NH,),
        in_specs=[
            pl.BlockSpec((T, W), lambda h: (0, col + h)),
            pl.BlockSpec((Tc, W), lambda h: (0, col + h)),
            pl.BlockSpec((None, 4, W), lambda h: (li, 0, h)),
            pl.BlockSpec((None, 1, W), lambda h: (li, 0, h)),
            pl.BlockSpec((None, 2, None, W, W), lambda h: (li, 0, h, 0, 0)),
            pl.BlockSpec((None, 2, None, W, W), lambda h: (li, 0, h, 0, 0)),
            vec2, vec2, vec2,
        ],
        out_specs=out_specs,
        out_shape=out_shape,
        scratch_shapes=[pltpu.VMEM((W // LANES, R, LANES), F32)] * 2,
        compiler_params=_cparams("parallel"),
        name="lru_scan",
    )(pl_out, pc_out, conv_w, conv_b.reshape(conv_b.shape[0], 1, D), wa, wx, ba, bx, lam)


def _mul_kernel(a_ref, b_ref, o_ref):
    o_ref[...] = _bf(a_ref[...] * b_ref[...])


def lru_combine(p_out, hs):
    M, D = hs.shape
    tm = _tile(M, 256)
    return pl.pallas_call(
        _mul_kernel,
        grid=(M // tm,),
        in_specs=[pl.BlockSpec((tm, D), lambda i: (i, 0)), pl.BlockSpec((tm, D), lambda i: (i, 0))],
        out_specs=pl.BlockSpec((tm, D), lambda i: (i, 0)),
        out_shape=jax.ShapeDtypeStruct((M, D), BF16),
        compiler_params=_cparams("parallel"),
        name="lru_combine",
    )(p_out, hs)


def _mix_write(h, shifted, mu_ref, o_refs, cols):
    xx = shifted - h
    for n, o_ref in enumerate(o_refs):
        o_ref[:, cols] = _bf(h + xx * mu_ref[n:n + 1, cols])


def _mix_lat_kernel(xm_ref, xp_ref, xn_ref, mod_ref, mu_ref, *o_refs, T):
    tm, D = xm_ref.shape
    q = D // 4
    i = pl.program_id(0)
    g, sh, sc = mod_ref[0:1], mod_ref[1:2], mod_ref[2:3]
    hm = _normmod(xm_ref[...], g, sh, sc)
    hp = _normmod(xp_ref[...], g, sh, sc)
    hn = _normmod(xn_ref[...], g, sh, sc)
    row = lax.broadcasted_iota(jnp.int32, (tm, 1), 0)
    t = i * tm + row
    colpos = row & (GRID_W - 1)
    s0 = slice(0, q)
    left = jnp.where(colpos > 0, pltpu.roll(hm[:, s0], 1, 0), 0.0)
    _mix_write(hm[:, s0], left, mu_ref, o_refs, s0)
    s1 = slice(q, 2 * q)
    right = jnp.where(colpos < GRID_W - 1, pltpu.roll(hm[:, s1], tm - 1, 0), 0.0)
    _mix_write(hm[:, s1], right, mu_ref, o_refs, s1)
    s2 = slice(2 * q, 3 * q)
    up = jnp.concatenate([hp[:, s2], hm[:tm - GRID_W, s2]], axis=0) if tm > GRID_W else hp[:, s2]
    up = jnp.where(t >= GRID_W, up, 0.0)
    _mix_write(hm[:, s2], up, mu_ref, o_refs, s2)
    s3 = slice(3 * q, D)
    down = jnp.concatenate([hm[GRID_W:, s3], hn[:, s3]], axis=0) if tm > GRID_W else hn[:, s3]
    down = jnp.where(t < T - GRID_W, down, 0.0)
    _mix_write(hm[:, s3], down, mu_ref, o_refs, s3)


def _mix_ctx_kernel(x_ref, mod_ref, mu_ref, *o_refs):
    Tc, D = x_ref.shape
    hh = D // 2
    h = _normmod(x_ref[...], mod_ref[0:1], mod_ref[1:2], mod_ref[2:3])
    row = lax.broadcasted_iota(jnp.int32, (Tc, 1), 0)
    s0 = slice(0, hh)
    prev = jnp.where(row > 0, pltpu.roll(h[:, s0], 1, 0), 0.0)
    _mix_write(h[:, s0], prev, mu_ref, o_refs, s0)
    s1 = slice(hh, D)
    nxt = jnp.where(row < Tc - 1, pltpu.roll(h[:, s1], Tc - 1, 0), 0.0)
    _mix_write(h[:, s1], nxt, mu_ref, o_refs, s1)


def rwkv_shiftmix(x, mod, mu, li, grid_tokens):
    M, D = x.shape
    n_out = mu.shape[1]
    mu_spec_args = ((None, n_out, D),)
    out_shape = [jax.ShapeDtypeStruct((M, D), BF16)] * n_out
    if grid_tokens:
        tm = _tile(M, 256)
        assert tm % GRID_W == 0 and M % GRID_W == 0
        r = tm // GRID_W
        nb = M // GRID_W
        return pl.pallas_call(
            functools.partial(_mix_lat_kernel, T=M),
            grid=(M // tm,),
            in_specs=[
                pl.BlockSpec((tm, D), lambda i: (i, 0)),
                pl.BlockSpec((GRID_W, D), lambda i: (jnp.maximum(i * r - 1, 0), 0)),
                pl.BlockSpec((GRID_W, D), lambda i: (jnp.minimum((i + 1) * r, nb - 1), 0)),
                pl.BlockSpec((8, D), lambda i: (0, 0)),
                pl.BlockSpec(*mu_spec_args, lambda i: (li, 0, 0)),
            ],
            out_specs=[pl.BlockSpec((tm, D), lambda i: (i, 0))] * n_out,
            out_shape=out_shape,
            compiler_params=_cparams("parallel"),
            name="rwkv_shiftmix_grid",
        )(x, x, x, mod, mu)
    return pl.pallas_call(
        _mix_ctx_kernel,
        grid=(1,),
        in_specs=[
            pl.BlockSpec((M, D), lambda i: (0, 0)),
            pl.BlockSpec((8, D), lambda i: (0, 0)),
            pl.BlockSpec(*mu_spec_args, lambda i: (li, 0, 0)),
        ],
        out_specs=[pl.BlockSpec((M, D), lambda i: (0, 0))] * n_out,
        out_shape=out_shape,
        compiler_params=_cparams("arbitrary"),
        name="rwkv_shiftmix_seq",
    )(x, mod, mu)


def _lora_kernel(xw_ref, xa_ref, xg_ref, w1_ref, w2_ref, w0_ref, a1_ref, a2_ref, a0_ref, g1_ref, g2_ref,
                 lw_ref, a_ref, g_ref, tw_ref, ta_ref, tg_ref):
    @pl.when(pl.program_id(1) == 0)
    def _():
        for d in (0, 1):
            tw_ref[d] = _bf(jnp.tanh(_dot(xw_ref[...], w1_ref[d])))
            ta_ref[d] = _bf(_dot(xa_ref[...], a1_ref[d]))
        tg_ref[...] = _bf(_sigmoid(_dot(xg_ref[...], g1_ref[...])))

    for d in (0, 1):
        z = w0_ref[d:d + 1] + _dot(tw_ref[d], w2_ref[d])
        lw_ref[d] = -math.exp(-0.5) * _sigmoid(z)
        a_ref[d] = _sigmoid(a0_ref[d:d + 1] + _dot(ta_ref[d], a2_ref[d]))
    g_ref[...] = _dot(tg_ref[...], g2_ref[...])


def rwkv_lora(xw, xa, xg, w1, w2, w0, a1, a2, a0, g1, g2, li):
    M, D = xw.shape
    RW, RA, RG = w1.shape[-1], a1.shape[-1], g1.shape[-1]
    tm = _tile(M, 256)
    tn = _tile(D, 1024)
    row = pl.BlockSpec((tm, D), lambda i, j: (i, 0))
    out2 = pl.BlockSpec((2, tm, tn), lambda i, j: (0, i, j))
    return pl.pallas_call(
        _lora_kernel,
        grid=(M // tm, D // tn),
        in_specs=[
            row, row, row,
            pl.BlockSpec((None, 2, D, RW), lambda i, j: (li, 0, 0, 0)),
            pl.BlockSpec((None, 2, RW, tn), lambda i, j: (li, 0, 0, j)),
            pl.BlockSpec((None, 2, tn), lambda i, j: (li, 0, j)),
            pl.BlockSpec((None, 2, D, RA), lambda i, j: (li, 0, 0, 0)),
            pl.BlockSpec((None, 2, RA, tn), lambda i, j: (li, 0, 0, j)),
            pl.BlockSpec((None, 2, tn), lambda i, j: (li, 0, j)),
            pl.BlockSpec((None, D, RG), lambda i, j: (li, 0, 0)),
            pl.BlockSpec((None, RG, tn), lambda i, j: (li, 0, j)),
        ],
        out_specs=[out2, out2, pl.BlockSpec((tm, tn), lambda i, j: (i, j))],
        out_shape=[jax.ShapeDtypeStruct((2, M, D), F32), jax.ShapeDtypeStruct((2, M, D), F32),
                   jax.ShapeDtypeStruct((M, D), F32)],
        scratch_shapes=[pltpu.VMEM((2, tm, RW), BF16), pltpu.VMEM((2, tm, RA), BF16), pltpu.VMEM((tm, RG), BF16)],
        compiler_params=_cparams("parallel", "arbitrary"),
        name="rwkv_lora",
    )(xw, xa, xg, w1, w2, w0, a1, a2, a0, g1, g2)


def _pair_masks():
    row = lax.broadcasted_iota(jnp.int32, (LANES, LANES), 0)
    col = lax.broadcasted_iota(jnp.int32, (LANES, LANES), 1)
    same_head = (row < WKV_HEAD) == (col < WKV_HEAD)
    return row, col, same_head


def _wkv_kernel(r_ref, k_ref, v_ref, lw_ref, a_ref, par_ref, s0_ref, y_ref, z_ref, s_ref,
                st_ref, rh_ref, y0_ref, g_ref, j_ref, et_ref, *, rev):
    C = CHUNK
    Tt = r_ref.shape[0]
    nchunks = Tt // C

    @pl.when(pl.program_id(1) == 0)
    def _():
        st_ref[...] = s0_ref[...]

    k_k, k_a, r_k = par_ref[0:1], par_ref[1:2], par_ref[2:3]
    row, col, same_head = _pair_masks()
    ones_head = jnp.where(same_head, 1.0, 0.0).astype(BF16)
    tr, tc = row & (C - 1), col & (C - 1)
    before = (tc > tr) if rev else (tc < tr)
    before_eq = (tc >= tr) if rev else (tc <= tr)
    r64 = lax.broadcasted_iota(jnp.int32, (C, C), 0)
    c64 = lax.broadcasted_iota(jnp.int32, (C, C), 1)
    cum_m = jnp.where((c64 >= r64) if rev else (c64 <= r64), 1.0, 0.0).astype(BF16)
    eye = jnp.where(row == col, 1.0, 0.0)
    lane = lax.broadcasted_iota(jnp.int32, (C, LANES), 1)
    head_a = lane < WKV_HEAD

    def stack2f(x):
        return jnp.concatenate([jnp.where(head_a, x, 0.0), jnp.where(head_a, 0.0, x)], axis=0)

    def unstack(x2):
        return x2[0:C] + x2[C:2 * C]

    blk16 = (row >> 4) == (col >> 4)
    blk32 = (row >> 5) == (col >> 5)

    def each(f, *lists):
        return [f(*xs) for xs in zip(*lists)]

    def tri_inverse(a_kb):
        m = each(lambda x: jnp.where(blk16, -x, 0.0), a_kb)
        t = each(lambda x: eye + x, m)
        for _ in range(3):
            m = each(lambda x: _dot(_bf(x), _bf(x)), m)
            t = each(lambda x, y: x + _dot(_bf(x), _bf(y)), t, m)
        for inside, outside in ((blk32, blk16), (same_head, blk32)):
            sel = jnp.logical_and(inside, jnp.logical_not(outside))
            tb = each(_bf, t)
            lt = each(lambda x, y: _bf(_dot(_bf(jnp.where(sel, x, 0.0)), y)), a_kb, tb)
            t = each(lambda x, y, w: x - _dot(y, w), t, tb, lt)
        return each(_bf, t)

    sls = [pl.ds(c * C, C) for c in range(nchunks)]
    r, k, v, lw, a = ([ref[sl, :] for sl in sls] for ref in (r_ref, k_ref, v_ref, lw_ref, a_ref))
    kk = each(lambda x: x * k_k, k)
    nrm = each(lambda x: _exact_right(x * x, ones_head, 2), kk)
    kk = each(lambda x, y: x * lax.rsqrt(jnp.maximum(y, 1e-24)), kk, nrm)
    kd = each(lambda x, y: x * (1.0 + (y - 1.0) * k_a), k, a)
    beta = each(lambda x, y: x * y, a, kk)
    bonus = each(lambda x, y: _exact_right(x * y * r_k, ones_head, 2), r, kd)
    for sl, x, y in zip(sls, bonus, v):
        z_ref[sl, :] = x * y
    b = each(lambda x: _exact_left(cum_m, x, 3), lw)
    b_tot = each(lambda x: x[0:1] if rev else x[C - 1:C], b)
    e_b = each(jnp.exp, b)
    e_nb = each(lambda x: jnp.exp(-x), b)
    e_rest = each(lambda x, y: jnp.exp(y - x), b, b_tot)
    r2f = each(lambda x, y: stack2f(x * y), r, e_b)
    r2 = each(_bf, r2f)
    kap2 = each(lambda x, y, w: _bf(stack2f(x * jnp.exp(y - w))), kk, b, lw)
    k2 = each(lambda x, y: _bf(stack2f(x * y)), kd, e_nb)
    be2 = each(lambda x, y: _bf(stack2f(x * y)), beta, e_nb)
    v2 = each(lambda x: _bf(stack2f(x)), v)
    kc2 = each(lambda x, y: _bf(stack2f(x * y)), kd, e_rest)
    bc2 = each(lambda x, y: _bf(stack2f(x * y)), beta, e_rest)
    a_kb = each(lambda x, y: jnp.where(before, _dot_nt(x, y), 0.0), kap2, be2)
    a_kk = each(lambda x, y: _bf(jnp.where(before, _dot_nt(x, y), 0.0)), kap2, k2)
    a_rk = each(lambda x, y: _bf(jnp.where(before_eq, _dot_nt(x, y), 0.0)), r2, k2)
    a_rb = each(lambda x, y: _bf(jnp.where(before_eq, _dot_nt(x, y), 0.0)), r2, be2)
    tinv = tri_inverse(a_kb)
    w0 = each(lambda x, y: _bf(_dot(x, y)), a_kk, v2)
    kh = each(lambda x, y: _bf(_dot(x, y)), tinv, kap2)
    u0 = each(lambda x, y: _bf(_dot(x, y)), tinv, w0)
    rh = each(lambda x, y, w: _bf(x - _dot(y, w)), r2f, a_rb, kh)
    y0 = each(lambda x, y, w, q: _dot(x, y) - _dot(w, q), a_rk, v2, a_rb, u0)
    jj = each(lambda x, y, w, q: _dot_tn(x, y) - _dot_tn(w, q), v2, kc2, u0, bc2)
    gg = each(lambda x, y: _bf(-_dot_tn(x, y)), kh, bc2)
    for c in range(nchunks):
        rh_ref[c] = rh[c]
        y0_ref[c] = y0[c]
        j_ref[c] = jj[c]
        g_ref[c] = gg[c]
        et_ref[c] = jnp.broadcast_to(jnp.exp(b_tot[c]), (8, LANES))

    S = st_ref[...]
    for c in (range(nchunks - 1, -1, -1) if rev else range(nchunks)):
        Sb = _bf(S)
        y_ref[pl.ds(c * C, C), :] = unstack(_dot_nt(rh_ref[c], Sb) + y0_ref[c])
        S = S * et_ref[c, 0:1, :] + _dot(Sb, g_ref[c]) + j_ref[c]
    st_ref[...] = S

    @pl.when(pl.program_id(1) == pl.num_programs(1) - 1)
    def _():
        s_ref[...] = S


def wkv_scan(r, k, v, lw, a, par, s0, d):
    T, D = r.shape
    HP = D // LANES
    Tt = _tile(T, 512)
    NT = T // Tt
    nck = Tt // CHUNK
    rev = d == 1

    def tt(t):
        return NT - 1 - t if rev else t

    tok = pl.BlockSpec((Tt, LANES), lambda h, t: (tt(t), h))
    tok_d = pl.BlockSpec((None, Tt, LANES), lambda h, t: (d, tt(t), h))
    st = pl.BlockSpec((None, LANES, LANES), lambda h, t: (h, 0, 0))
    return pl.pallas_call(
        functools.partial(_wkv_kernel, rev=rev),
        grid=(HP, NT),
        in_specs=[tok, tok, tok, tok_d, tok_d, pl.BlockSpec((8, LANES), lambda h, t: (0, h)), st],
        out_specs=[tok, tok, st],
        out_shape=[jax.ShapeDtypeStruct((T, D), F32), jax.ShapeDtypeStruct((T, D), F32),
                   jax.ShapeDtypeStruct((HP, LANES, LANES), F32)],
        scratch_shapes=[
            pltpu.VMEM((LANES, LANES), F32),
            pltpu.VMEM((nck, LANES, LANES), BF16),
            pltpu.VMEM((nck, LANES, LANES), F32),
            pltpu.VMEM((nck, LANES, LANES), BF16),
            pltpu.VMEM((nck, LANES, LANES), F32),
            pltpu.VMEM((nck, 8, LANES), F32),
        ],
        compiler_params=_cparams("parallel", "arbitrary"),
        name="wkv_scan",
    )(r, k, v, lw, a, par, s0)


def _rwkv_combine_kernel(y0_ref, y1_ref, z0_ref, z1_ref, g_ref, ln_ref, o_ref):
    D = o_ref.shape[1]
    row = lax.broadcasted_iota(jnp.int32, (LANES, LANES), 0)
    col = lax.broadcasted_iota(jnp.int32, (LANES, LANES), 1)
    mean_m = jnp.where((row < WKV_HEAD) == (col < WKV_HEAD), 1.0 / WKV_HEAD, 0.0).astype(BF16)
    for c in range(D // LANES):
        cs = slice(c * LANES, (c + 1) * LANES)
        y = y0_ref[:, cs] + y1_ref[:, cs]
        yc = y - _exact_right(y, mean_m, 2)
        var = _exact_right(yc * yc, mean_m, 2)
        out = yc * lax.rsqrt(var + GN_EPS) * ln_ref[0:1, cs] + ln_ref[1:2, cs] + (z0_ref[:, cs] + z1_ref[:, cs])
        o_ref[:, cs] = _bf(out * g_ref[:, cs])


def rwkv_combine(y0, y1, z0, z1, g, ln):
    M, D = g.shape
    tm = _tile(M, 256)
    tok = pl.BlockSpec((tm, D), lambda i: (i, 0))
    return pl.pallas_call(
        _rwkv_combine_kernel,
        grid=(M // tm,),
        in_specs=[tok] * 5 + [pl.BlockSpec((8, D), lambda i: (0, 0))],
        out_specs=tok,
        out_shape=jax.ShapeDtypeStruct((M, D), BF16),
        compiler_params=_cparams("parallel"),
        name="rwkv_combine",
    )(y0, y1, z0, z1, g, ln)


def _gla_kernel(q_ref, i_ref, f_ref, lbl_ref, s0_ref, o_ref, s_ref, st_ref, qe_ref, j_ref, et_ref, *, rev, layer):
    C = CHUNK
    Tt = q_ref.shape[0]
    nchunks = Tt // C
    nsub = C // SUB

    @pl.when(pl.program_id(1) == 0)
    def _():
        st_ref[...] = s0_ref[...]

    logits = lbl_ref[...]
    e = jnp.exp(logits - jnp.max(logits, axis=0, keepdims=True))
    p = e / jnp.sum(e, axis=0, keepdims=True)
    lb = jnp.zeros((1, LANES), F32)
    for l in range(1, layer + 1):
        lb = lb + p[l:l + 1]
    log_lb = jnp.log(lb)
    log_1m = jnp.log1p(-lb)

    r64 = lax.broadcasted_iota(jnp.int32, (C, C), 0)
    c64 = lax.broadcasted_iota(jnp.int32, (C, C), 1)
    cum_m = jnp.where((c64 >= r64) if rev else (c64 <= r64), 1.0, 0.0).astype(BF16)
    rows = lax.broadcasted_iota(jnp.int32, (C, 1), 0)
    lane_s = lax.broadcasted_iota(jnp.int32, (SUB, C), 1)
    row_s = lax.broadcasted_iota(jnp.int32, (SUB, 1), 0)

    def each(f, *lists):
        return [f(*xs) for xs in zip(*lists)]

    sls = [pl.ds(c * C, C) for c in range(nchunks)]
    q = [_silu(q_ref[sl, :]) for sl in sls]
    v = [i_ref[sl, :] for sl in sls]
    f = [f_ref[sl, :] for sl in sls]
    kg = each(lambda x: (1.0 - lb) * _sigmoid(-x), f)
    x2 = each(lambda x: log_1m - _softplus(-x), f)
    g = each(lambda x: jnp.maximum(log_lb, x) + jnp.log1p(jnp.exp(-jnp.abs(log_lb - x))), x2)
    b = each(lambda x: _exact_left(cum_m, x, 3), g)
    b_tot = each(lambda x: x[0:1] if rev else x[C - 1:C], b)
    vb = each(_bf, v)
    att_rows = [[] for _ in range(nchunks)]
    for I in range(nsub):
        lo = I * SUB
        qI = each(lambda x: x[lo:lo + SUB], q)
        bI = each(lambda x: x[lo:lo + SUB], b)
        first = (I == nsub - 1) if rev else (I == 0)
        if first:
            att = [jnp.zeros((SUB, C), F32)] * nchunks
        else:
            ref = each(lambda x: x[lo + SUB:lo + SUB + 1] if rev else x[lo - 1:lo], b)
            earlier = (rows >= lo + SUB) if rev else (rows < lo)
            kt = each(lambda x, y, w: _bf(jnp.where(earlier, x * jnp.exp(w - y), 0.0)), kg, b, ref)
            qt = each(lambda x, y, w: _bf(x * jnp.exp(y - w)), qI, bI, ref)
            att = each(_dot_nt, qt, kt)
        for j in range(SUB):
            s = lo + j
            place = jnp.logical_and(lane_s == s, (row_s <= j) if rev else (row_s >= j))
            pj = each(lambda x, y, w, u: x * y[s:s + 1] * jnp.exp(w - u[s:s + 1]), qI, kg, bI, b)
            att = each(lambda x, y: x + jnp.where(place, jnp.sum(y, axis=-1, keepdims=True), 0.0), att, pj)
        for c in range(nchunks):
            att_rows[c].append(att[c])
    o_in = each(lambda x, y: _dot(_bf(jnp.concatenate(x, axis=0)), y), att_rows, vb)
    for c in range(nchunks):
        o_ref[sls[c], :] = o_in[c]
        qe_ref[c] = _bf(q[c] * jnp.exp(b[c]))
        j_ref[c] = _dot_tn(vb[c], _bf(kg[c] * jnp.exp(b_tot[c] - b[c])))
        et_ref[c] = jnp.broadcast_to(jnp.exp(b_tot[c]), (8, LANES))

    Z = st_ref[...]
    for c in (range(nchunks - 1, -1, -1) if rev else range(nchunks)):
        o_ref[sls[c], :] += _dot_nt(qe_ref[c], _bf(Z))
        Z = Z * et_ref[c, 0:1, :] + j_ref[c]
    st_ref[...] = Z

    @pl.when(pl.program_id(1) == pl.num_programs(1) - 1)
    def _():
        s_ref[...] = Z


def gla_scan(raw, lb_logits, s0, d, layer):
    T = raw.shape[0]
    D = raw.shape[1] // 5
    H = D // LANES
    Tt = _tile(T, 256)
    NT = T // Tt
    nck = Tt // CHUNK
    rev = d == 1
    L = lb_logits.shape[0]

    def tt(t):
        return NT - 1 - t if rev else t

    st = pl.BlockSpec((None, LANES, LANES), lambda h, t: (h, 0, 0))
    return pl.pallas_call(
        functools.partial(_gla_kernel, rev=rev, layer=layer),
        grid=(H, NT),
        in_specs=[
            pl.BlockSpec((Tt, LANES), lambda h, t: (tt(t), h)),
            pl.BlockSpec((Tt, LANES), lambda h, t: (tt(t), H + h)),
            pl.BlockSpec((Tt, LANES), lambda h, t: (tt(t), (2 + d) * H + h)),
            pl.BlockSpec((L, LANES), lambda h, t: (0, h)),
            st,
        ],
        out_specs=[pl.BlockSpec((Tt, LANES), lambda h, t: (tt(t), h)), st],
        out_shape=[jax.ShapeDtypeStruct((T, D), F32), jax.ShapeDtypeStruct((H, LANES, LANES), F32)],
        scratch_shapes=[
            pltpu.VMEM((LANES, LANES), F32),
            pltpu.VMEM((nck, CHUNK, LANES), BF16),
            pltpu.VMEM((nck, LANES, LANES), F32),
            pltpu.VMEM((nck, 8, LANES), F32),
        ],
        compiler_params=_cparams("parallel", "arbitrary"),
        name="gla_scan",
    )(raw, raw, raw, lb_logits, s0)


def _hgrn_combine_kernel(o0_ref, o1_ref, g_ref, gn_ref, o_ref):
    D = o_ref.shape[1]
    for c in range(D // LANES):
        cs = slice(c * LANES, (c + 1) * LANES)
        o = o0_ref[:, cs] + o1_ref[:, cs]
        o = o * lax.rsqrt(jnp.mean(o * o, axis=-1, keepdims=True) + EPS) * gn_ref[0:1, cs]
        o_ref[:, cs] = _bf(o * _silu(g_ref[:, cs]))


def hgrn_combine(o0, o1, raw, gn):
    M, D = o0.shape
    tm = _tile(M, 256)
    tok = pl.BlockSpec((tm, D), lambda i: (i, 0))
    return pl.pallas_call(
        _hgrn_combine_kernel,
        grid=(M // tm,),
        in_specs=[tok, tok, pl.BlockSpec((tm, D), lambda i: (i, 4)), pl.BlockSpec((1, D), lambda i: (0, 0))],
        out_specs=tok,
        out_shape=jax.ShapeDtypeStruct((M, D), BF16),
        compiler_params=_cparams("parallel"),
        name="hgrn_combine",
    )(o0, o1, raw, gn)


def _rows8(*vecs):
    D = vecs[0].shape[-1]
    rows = [v.reshape(1, D) for v in vecs]
    rows.append(jnp.zeros((8 - len(rows), D), F32))
    return jnp.concatenate(rows, axis=0)


def kernel(x, c, ctx, c_ctx, ada_down, ada_up, ada_b, norm_g, ffn_w13, ffn_w2, final_g, lru_w_in, lru_conv_w, lru_conv_b, lru_gate_a_w, lru_gate_a_b, lru_gate_x_w, lru_gate_x_b, lru_lam, lru_w_out, rwkv_mu, rwkv_w_r, rwkv_w_k, rwkv_w_v, rwkv_w_o, rwkv_w0, rwkv_w1, rwkv_w2, rwkv_a0, rwkv_a1, rwkv_a2, rwkv_g1, rwkv_g2, rwkv_k_k, rwkv_k_a, rwkv_r_k, rwkv_ln_w, rwkv_ln_b, hgrn_w_in, hgrn_lb_logits, hgrn_gn_g, hgrn_w_out):
    B, T, D = x.shape
    assert B == 1, "one sequence per call"
    depth = ada_down.shape[0]
    xl, xc = x[0], ctx[0]

    mods = ada_all_layers(_rows8(c[0], c_ctx), ada_down, ada_up, ada_b)
    mods = mods[:, :2].reshape(depth, 2, N_MOD, D)

    w13, w2 = _bf(ffn_w13), _bf(ffn_w2)
    lru_in, lru_out = _bf(lru_w_in), _bf(lru_w_out)
    lru_wa, lru_wx = _bf(lru_gate_a_w), _bf(lru_gate_x_w)
    w_r, w_k, w_v, w_o = _bf(rwkv_w_r), _bf(rwkv_w_k), _bf(rwkv_w_v), _bf(rwkv_w_o)
    lw1, lw2, la1, la2 = _bf(rwkv_w1), _bf(rwkv_w2), _bf(rwkv_a1), _bf(rwkv_a2)
    lg1, lg2 = _bf(rwkv_g1), _bf(rwkv_g2)
    h_in, h_out = _bf(hgrn_w_in), _bf(hgrn_w_out)

    for i in range(depth):
        need_ctx = i < depth - 1
        kind, j = i % 3, i // 3
        ml, mc = mods[i, 0], mods[i, 1]

        def ffn(xs, m, k, which):
            return half_ffn(xs, _rows8(norm_g[i, 2 * which], m[k], m[k + 1], m[k + 2]), w13, w2, i, which)

        xl = ffn(xl, ml, 0, 0)
        xc = ffn(xc, mc, 0, 0)
        mod_l = _rows8(norm_g[i, 1], ml[3], ml[4])
        mod_c = _rows8(norm_g[i, 1], mc[3], mc[4])
        gate_l, gate_c = _rows8(ml[5]), _rows8(mc[5])

        if kind == 0:
            p_l = proj(xl, lru_in, (j,), mod=mod_l, gelu_cols=D)
            p_c = proj(xc, lru_in, (j,), mod=mod_c, gelu_cols=D)
            hs = lru_scan(p_l, p_c, lru_conv_w, lru_conv_b, lru_wa, lru_wx, lru_gate_a_b, lru_gate_x_b, lru_lam,
                          j, need_ctx)
            a_l = lru_combine(p_l, hs[0])
            a_c = lru_combine(p_c, hs[1]) if need_ctx else None
            w_last, w_idx = lru_out, (j,)
        elif kind == 1:
            par = _rows8(rwkv_k_k[j], rwkv_k_a[j], rwkv_r_k[j].reshape(D))
            ln = _rows8(rwkv_ln_w[j], rwkv_ln_b[j])
            outs = []
            state = [jnp.zeros((D // LANES, LANES, LANES), F32)] * 2
            for xs, mod, is_grid in ((xc, mod_c, False), (xl, mod_l, True)):
                xr, xw, xk, xv, xa, xg = rwkv_shiftmix(xs, mod, rwkv_mu, j, is_grid)
                r = proj(xr, w_r, (j,))
                k = proj(xk, w_k, (j,))
                v = proj(xv, w_v, (j,))
                lw, a, g = rwkv_lora(xw, xa, xg, lw1, lw2, rwkv_w0, la1, la2, rwkv_a0, lg1, lg2, j)
                ys, zs = [], []
                for d in (0, 1):
                    y, z, state[d] = wkv_scan(r, k, v, lw, a, par, state[d], d)
                    ys.append(y)
                    zs.append(z)
                outs.append(rwkv_combine(ys[0], ys[1], zs[0], zs[1], g, ln))
            a_c, a_l = outs
            w_last, w_idx = w_o, (j,)
        else:
            outs = []
            state = [jnp.zeros((D // LANES, LANES, LANES), F32)] * 2
            for xs, mod in ((xc, mod_c), (xl, mod_l)):
                raw = proj(xs, h_in, (j,), mod=mod)
                os_ = []
                for d in (0, 1):
                    o, state[d] = gla_scan(raw, hgrn_lb_logits, state[d], d, i)
                    os_.append(o)
                outs.append(hgrn_combine(os_[0], os_[1], raw, hgrn_gn_g[j].reshape(1, D)))
            a_c, a_l = outs
            w_last, w_idx = h_out, (j,)

        xl = proj(a_l, w_last, w_idx, res=xl, gate=gate_l)
        xl = ffn(xl, ml, 6, 1)
        if need_ctx:
            xc = proj(a_c, w_last, w_idx, res=xc, gate=gate_c)
            xc = ffn(xc, mc, 6, 1)

    return final_rmsnorm(xl, final_g)[None]
```

```python
import functools
import math

import jax
import jax.numpy as jnp
from jax import lax
from jax.experimental import pallas as pl
from jax.experimental.pallas import tpu as pltpu

F32 = jnp.float32
BF16 = jnp.bfloat16

EPS = 1e-6
GN_EPS = 64e-5
LRU_C = 8.0
GRID_W = 64
N_MOD = 9
WKV_HEAD = 64
GLA_HEAD = 128
CHUNK = 64
SUB = 16
LANES = 128
VMEM_LIMIT_BYTES = 56 * 1024 * 1024


def _cparams(*sem):
    return pltpu.CompilerParams(dimension_semantics=sem, vmem_limit_bytes=VMEM_LIMIT_BYTES)


def _tile(n, pref):
    if n <= pref:
        return n
    for t in range(pref, 7, -1):
        if n % t == 0 and t % 8 == 0:
            return t
    return n


def _bf(x):
    return x.astype(BF16)


def _dot(a, b):
    return jnp.dot(a, b, preferred_element_type=F32)


def _dot_nt(a, b):
    return lax.dot_general(a, b, (((1,), (1,)), ((), ())), preferred_element_type=F32)


def _dot_tn(a, b):
    return lax.dot_general(a, b, (((0,), (0,)), ((), ())), preferred_element_type=F32)


def _split_terms(x, terms):
    out, rem = [], x
    for _ in range(terms):
        p = _bf(rem)
        out.append(p)
        rem = rem - p.astype(F32)
    return out


def _exact_left(m_bf, x, terms):
    acc = None
    for p in _split_terms(x, terms):
        d = _dot(m_bf, p)
        acc = d if acc is None else acc + d
    return acc


def _exact_right(x, m_bf, terms):
    acc = None
    for p in _split_terms(x, terms):
        d = _dot(p, m_bf)
        acc = d if acc is None else acc + d
    return acc


def _sigmoid(x):
    return 0.5 * jnp.tanh(0.5 * x) + 0.5


def _silu(x):
    return x * _sigmoid(x)


def _gelu_tanh(x):
    return 0.5 * x * (1.0 + jnp.tanh(math.sqrt(2.0 / math.pi) * (x + 0.044715 * (x * x * x))))


def _softplus(x):
    return jnp.maximum(x, 0.0) + jnp.log1p(jnp.exp(-jnp.abs(x)))


def _normmod(x, g, shift, scale):
    ms = jnp.mean(x * x, axis=-1, keepdims=True)
    return (x * lax.rsqrt(ms + EPS) * g) * (1.0 + scale) + shift


def _ada_kernel(cc_ref, down_ref, up_ref, b_ref, o_ref, t_ref):
    hi = lax.Precision.HIGHEST

    @pl.when(pl.program_id(1) == 0)
    def _():
        t_ref[...] = jnp.dot(_silu(cc_ref[...]), down_ref[...], precision=hi, preferred_element_type=F32)

    o_ref[...] = jnp.dot(t_ref[...], up_ref[...], precision=hi, preferred_element_type=F32) + b_ref[...]


def ada_all_layers(cc, down, up, bias):
    L, D, R = down.shape
    N = up.shape[2]
    tn = _tile(N, 4096)
    return pl.pallas_call(
        _ada_kernel,
        grid=(L, N // tn),
        in_specs=[
            pl.BlockSpec((8, D), lambda l, j: (0, 0)),
            pl.BlockSpec((None, D, R), lambda l, j: (l, 0, 0)),
            pl.BlockSpec((None, R, tn), lambda l, j: (l, 0, j)),
            pl.BlockSpec((None, 1, tn), lambda l, j: (l, 0, j)),
        ],
        out_specs=pl.BlockSpec((None, 8, tn), lambda l, j: (l, 0, j)),
        out_shape=jax.ShapeDtypeStruct((L, 8, N), F32),
        scratch_shapes=[pltpu.VMEM((8, R), F32)],
        compiler_params=_cparams("parallel", "arbitrary"),
        name="ada",
    )(cc, down, up, bias.reshape(L, 1, N))


def _ffn_kernel(x_ref, mod_ref, w1_ref, w3_ref, w2_ref, o_ref, *rest, emit_weights):
    h_ref = rest[-1]
    f = pl.program_id(1)

    @pl.when(f == 0)
    def _():
        h_ref[...] = _bf(_normmod(x_ref[...], mod_ref[0:1], mod_ref[1:2], mod_ref[2:3]))
        o_ref[...] = jnp.zeros_like(o_ref)

    w1, w3, w2 = _bf(w1_ref[...]), _bf(w3_ref[...]), _bf(w2_ref[...])
    if emit_weights:
        for ref, w in zip(rest[:3], (w1, w3, w2)):
            ref[...] = w
    h = h_ref[...]
    act = _bf(_silu(_dot(h, w1)) * _dot(h, w3))
    o_ref[...] += _dot(act, w2)

    @pl.when(f == pl.num_programs(1) - 1)
    def _():
        o_ref[...] = x_ref[...] + 0.5 * mod_ref[3:4] * o_ref[...]


def half_ffn(x, mod, weights, index=None):
    M, D = x.shape
    emit = index is not None
    F = weights[-1].shape[-2]
    tm = _tile(M, 512)
    tf = _tile(F, 256)
    nf = F // tf
    if emit:
        assert M == tm
        w13, w2 = weights
        lead = tuple(index)
        w_specs = [
            pl.BlockSpec((None, None, D, tf), lambda i, f: lead + (0, f)),
            pl.BlockSpec((None, None, D, tf), lambda i, f: lead + (0, nf + f)),
            pl.BlockSpec((None, None, tf, D), lambda i, f: lead + (f, 0)),
        ]
        w_args = (w13, w13, w2)
    else:
        w_specs = [
            pl.BlockSpec((D, tf), lambda i, f: (0, f)),
            pl.BlockSpec((D, tf), lambda i, f: (0, f)),
            pl.BlockSpec((tf, D), lambda i, f: (f, 0)),
        ]
        w_args = tuple(weights)
    out_specs = [pl.BlockSpec((tm, D), lambda i, f: (i, 0))]
    out_shape = [jax.ShapeDtypeStruct((M, D), F32)]
    if emit:
        out_specs += [
            pl.BlockSpec((D, tf), lambda i, f: (0, f)),
            pl.BlockSpec((D, tf), lambda i, f: (0, f)),
            pl.BlockSpec((tf, D), lambda i, f: (f, 0)),
        ]
        out_shape += [jax.ShapeDtypeStruct((D, F), BF16), jax.ShapeDtypeStruct((D, F), BF16),
                      jax.ShapeDtypeStruct((F, D), BF16)]
    out = pl.pallas_call(
        functools.partial(_ffn_kernel, emit_weights=emit),
        grid=(M // tm, nf),
        in_specs=[
            pl.BlockSpec((tm, D), lambda i, f: (i, 0), pipeline_mode=pl.Buffered(1)),
            pl.BlockSpec((8, D), lambda i, f: (0, 0)),
        ] + w_specs,
        out_specs=out_specs,
        out_shape=out_shape,
        scratch_shapes=[pltpu.VMEM((tm, D), BF16)],
        compiler_params=_cparams("parallel", "arbitrary"),
        name="half_ffn",
    )(x, mod, *w_args)
    return out if emit else out[0]


def _proj_kernel(*refs, norm, n_gelu, residual):
    it = iter(refs)
    a_ref = next(it)
    mod_ref = next(it) if norm else None
    w_ref = next(it)
    res_ref = next(it) if residual else None
    gate_ref = next(it) if residual else None
    o_ref = next(it)
    h_ref = next(it) if norm else None
    j = pl.program_id(1)

    if norm:
        @pl.when(j == 0)
        def _():
            h_ref[...] = _bf(_normmod(a_ref[...], mod_ref[0:1], mod_ref[1:2], mod_ref[2:3]))
        lhs = h_ref[...]
    else:
        lhs = a_ref[...]
    acc = _dot(lhs, w_ref[...])
    if residual:
        o_ref[...] = res_ref[...] + gate_ref[0:1] * acc
    elif n_gelu:
        @pl.when(j < n_gelu)
        def _():
            o_ref[...] = _gelu_tanh(acc)

        @pl.when(j >= n_gelu)
        def _():
            o_ref[...] = acc
    else:
        o_ref[...] = acc


def proj(a, w, w_index, *, mod=None, gelu_cols=0, res=None, gate=None, tm_pref=512, tn_pref=1024):
    M, K = a.shape
    N = w.shape[-1]
    tm = _tile(M, tm_pref)
    tn = _tile(math.gcd(N, gelu_cols) if gelu_cols else N, tn_pref)
    norm = mod is not None
    residual = res is not None
    assert gelu_cols % tn == 0
    lead = tuple(w_index)
    in_specs = [pl.BlockSpec((tm, K), lambda i, j: (i, 0))]
    args = [a]
    if norm:
        in_specs.append(pl.BlockSpec((8, K), lambda i, j: (0, 0)))
        args.append(mod)
    in_specs.append(pl.BlockSpec((None,) * len(lead) + (K, tn), lambda i, j: lead + (0, j)))
    args.append(w)
    if residual:
        in_specs.append(pl.BlockSpec((tm, tn), lambda i, j: (i, j)))
        in_specs.append(pl.BlockSpec((8, tn), lambda i, j: (0, j)))
        args += [res, gate]
    return pl.pallas_call(
        functools.partial(_proj_kernel, norm=norm, n_gelu=gelu_cols // tn, residual=residual),
        grid=(M // tm, N // tn),
        in_specs=in_specs,
        out_specs=pl.BlockSpec((tm, tn), lambda i, j: (i, j)),
        out_shape=jax.ShapeDtypeStruct((M, N), F32),
        scratch_shapes=[pltpu.VMEM((tm, K), BF16)] if norm else [],
        compiler_params=_cparams("parallel", "arbitrary"),
        name="proj",
    )(*args)


def _rmsnorm_kernel(x_ref, g_ref, o_ref):
    x = x_ref[...]
    ms = jnp.mean(x * x, axis=-1, keepdims=True)
    o_ref[...] = x * lax.rsqrt(ms + EPS) * g_ref[...]


def final_rmsnorm(x, g):
    M, D = x.shape
    tm = _tile(M, 512)
    return pl.pallas_call(
        _rmsnorm_kernel,
        grid=(M // tm,),
        in_specs=[pl.BlockSpec((tm, D), lambda i: (i, 0)), pl.BlockSpec((1, D), lambda i: (0, 0))],
        out_specs=pl.BlockSpec((tm, D), lambda i: (i, 0)),
        out_shape=jax.ShapeDtypeStruct((M, D), F32),
        compiler_params=_cparams("parallel"),
        name="final_rmsnorm",
    )(x, g.reshape(1, D))


def _lru_kernel(ul_ref, uc_ref, cw_ref, cb_ref, wa_ref, wx_ref, ba_ref, bx_ref, lam_ref, *refs, R, need_ctx):
    hl_ref = refs[0]
    hc_ref = refs[1] if need_ctx else None
    v_ref, hs_ref = refs[-2:]
    T, W = ul_ref.shape
    NL = W // LANES
    Tc = uc_ref.shape[0]
    cw = cw_ref[...]
    cb = cb_ref[...]

    def conv_chunk(u_ref, t0, n_rows):
        main = u_ref[pl.ds(t0, R), :]
        p0 = pl.multiple_of(jnp.maximum(t0 - 8, 0), 8)
        n0 = pl.multiple_of(jnp.minimum(t0 + R, n_rows - 8), 8)
        prev = jnp.where(t0 > 0, u_ref[pl.ds(p0, 8), :], 0.0)
        nxt = jnp.where(t0 + R < n_rows, u_ref[pl.ds(n0, 8), :], 0.0)
        ext = jnp.concatenate([prev, main, nxt], axis=0)
        n = R + 16
        out = cb + ext[8:8 + R] * cw[2:3]
        out = out + pltpu.roll(ext, 2, 0)[8:8 + R] * cw[0:1]
        out = out + pltpu.roll(ext, 1, 0)[8:8 + R] * cw[1:2]
        out = out + pltpu.roll(ext, n - 1, 0)[8:8 + R] * cw[3:4]
        return out

    G = R // 8

    def scan_chunk(d, carry, rev):
        vp = jnp.concatenate(
            [jnp.concatenate([v_ref[j, pl.ds(g, 8, stride=G), :] for j in range(NL)], axis=1) for g in range(G)],
            axis=0)
        vb = _bf(vp)
        r = _sigmoid(_dot(vb, wa_ref[d]) + ba_ref[d:d + 1])
        gi = _sigmoid(_dot(vb, wx_ref[d]) + bx_ref[d:d + 1])
        log_a = -LRU_C * r * _softplus(-lam_ref[d:d + 1])
        a = jnp.exp(log_a)
        b = jnp.sqrt(-jnp.tanh(log_a) * (a * a + 1.0)) * (gi * vp)
        h = jnp.zeros((8, W), F32)
        p = jnp.ones((8, W), F32)
        hs, ps = [None] * G, [None] * G
        for g in (range(G - 1, -1, -1) if rev else range(G)):
            ag = a[8 * g:8 * g + 8]
            h = ag * h + b[8 * g:8 * g + 8]
            p = ag * p
            hs[g], ps[g] = h, p
        starts = [None] * 8
        for s in (range(7, -1, -1) if rev else range(8)):
            starts[s] = carry
            carry = h[s:s + 1] + p[s:s + 1] * carry
        start = jnp.concatenate(starts, axis=0)
        return [hs[g] + ps[g] * start for g in range(G)], carry

    def run(u_ref, o_ref, n_rows, d, carry):
        rev = d == 1
        nchunks = n_rows // R

        def body(i, carry):
            c = nchunks - 1 - i if rev else i
            t0 = pl.multiple_of(c * R, R)
            v = conv_chunk(u_ref, t0, n_rows)
            for j in range(NL):
                v_ref[j] = v[:, j * LANES:(j + 1) * LANES]
            hs, carry = scan_chunk(d, carry, rev)
            if o_ref is not None:
                for g in range(G):
                    for j in range(NL):
                        hs_ref[j, pl.ds(g, 8, stride=G), :] = hs[g][:, j * LANES:(j + 1) * LANES]
                h = jnp.concatenate([hs_ref[j] for j in range(NL)], axis=1)
                if d == 0:
                    o_ref[pl.ds(t0, R), :] = h
                else:
                    o_ref[pl.ds(t0, R), :] += h
            return carry

        return lax.fori_loop(0, nchunks, body, carry)

    for d in (0, 1):
        carry = run(uc_ref, hc_ref, Tc, d, jnp.zeros((1, W), F32))
        run(ul_ref, hl_ref, T, d, carry)


def lru_scan(pl_out, pc_out, conv_w, conv_b, wa, wx, ba, bx, lam, li, need_ctx):
    T = pl_out.shape[0]
    Tc = pc_out.shape[0]
    D = pl_out.shape[1] // 2
    NH, W = wa.shape[2], wa.shape[3]
    assert W % LANES == 0 and NH * W == D
    R = _tile(math.gcd(T, Tc), 256)
    col = D // W
    vec2 = pl.BlockSpec((None, 2, W), lambda h: (li, 0, h))
    out_shape = [jax.ShapeDtypeStruct((T, D), F32)]
    out_specs = [pl.BlockSpec((T, W), lambda h: (0, h))]
    if need_ctx:
        out_shape.append(jax.ShapeDtypeStruct((Tc, D), F32))
        out_specs.append(pl.BlockSpec((Tc, W), lambda h: (0, h)))
    return pl.pallas_call(
        functools.partial(_lru_kernel, R=R, need_ctx=need_ctx),
        grid=(NH,),
        in_specs=[
            pl.BlockSpec((T, W), lambda h: (0, col + h)),
            pl.BlockSpec((Tc, W), lambda h: (0, col + h)),
            pl.BlockSpec((None, 4, W), lambda h: (li, 0, h)),
            pl.BlockSpec((None, 1, W), lambda h: (li, 0, h)),
            pl.BlockSpec((None, 2, None, W, W), lambda h: (li, 0, h, 0, 0)),
            pl.BlockSpec((None, 2, None, W, W), lambda h: (li, 0, h, 0, 0)),
            vec2, vec2, vec2,
        ],
        out_specs=out_specs,
        out_shape=out_shape,
        scratch_shapes=[pltpu.VMEM((W // LANES, R, LANES), F32)] * 2,
        compiler_params=_cparams("parallel"),
        name="lru_scan",
    )(pl_out, pc_out, conv_w, conv_b.reshape(conv_b.shape[0], 1, D), wa, wx, ba, bx, lam)


def _mul_kernel(a_ref, b_ref, o_ref):
    o_ref[...] = _bf(a_ref[...] * b_ref[...])


def lru_combine(p_out, hs):
    M, D = hs.shape
    tm = _tile(M, 256)
    return pl.pallas_call(
        _mul_kernel,
        grid=(M // tm,),
        in_specs=[pl.BlockSpec((tm, D), lambda i: (i, 0)), pl.BlockSpec((tm, D), lambda i: (i, 0))],
        out_specs=pl.BlockSpec((tm, D), lambda i: (i, 0)),
        out_shape=jax.ShapeDtypeStruct((M, D), BF16),
        compiler_params=_cparams("parallel"),
        name="lru_combine",
    )(p_out, hs)


def _mix_write(h, shifted, mu_ref, o_refs, cols):
    xx = shifted - h
    for n, o_ref in enumerate(o_refs):
        o_ref[:, cols] = _bf(h + xx * mu_ref[n:n + 1, cols])


def _mix_lat_kernel(xm_ref, xp_ref, xn_ref, mod_ref, mu_ref, *o_refs, T):
    tm, D = xm_ref.shape
    q = D // 4
    i = pl.program_id(0)
    g, sh, sc = mod_ref[0:1], mod_ref[1:2], mod_ref[2:3]
    hm = _normmod(xm_ref[...], g, sh, sc)
    hp = _normmod(xp_ref[...], g, sh, sc)
    hn = _normmod(xn_ref[...], g, sh, sc)
    row = lax.broadcasted_iota(jnp.int32, (tm, 1), 0)
    t = i * tm + row
    colpos = row & (GRID_W - 1)
    s0 = slice(0, q)
    left = jnp.where(colpos > 0, pltpu.roll(hm[:, s0], 1, 0), 0.0)
    _mix_write(hm[:, s0], left, mu_ref, o_refs, s0)
    s1 = slice(q, 2 * q)
    right = jnp.where(colpos < GRID_W - 1, pltpu.roll(hm[:, s1], tm - 1, 0), 0.0)
    _mix_write(hm[:, s1], right, mu_ref, o_refs, s1)
    s2 = slice(2 * q, 3 * q)
    up = jnp.concatenate([hp[:, s2], hm[:tm - GRID_W, s2]], axis=0) if tm > GRID_W else hp[:, s2]
    up = jnp.where(t >= GRID_W, up, 0.0)
    _mix_write(hm[:, s2], up, mu_ref, o_refs, s2)
    s3 = slice(3 * q, D)
    down = jnp.concatenate([hm[GRID_W:, s3], hn[:, s3]], axis=0) if tm > GRID_W else hn[:, s3]
    down = jnp.where(t < T - GRID_W, down, 0.0)
    _mix_write(hm[:, s3], down, mu_ref, o_refs, s3)


def _mix_ctx_kernel(x_ref, mod_ref, mu_ref, *o_refs):
    Tc, D = x_ref.shape
    hh = D // 2
    h = _normmod(x_ref[...], mod_ref[0:1], mod_ref[1:2], mod_ref[2:3])
    row = lax.broadcasted_iota(jnp.int32, (Tc, 1), 0)
    s0 = slice(0, hh)
    prev = jnp.where(row > 0, pltpu.roll(h[:, s0], 1, 0), 0.0)
    _mix_write(h[:, s0], prev, mu_ref, o_refs, s0)
    s1 = slice(hh, D)
    nxt = jnp.where(row < Tc - 1, pltpu.roll(h[:, s1], Tc - 1, 0), 0.0)
    _mix_write(h[:, s1], nxt, mu_ref, o_refs, s1)


def rwkv_shiftmix(x, mod, mu, li, grid_tokens):
    M, D = x.shape
    n_out = mu.shape[1]
    mu_spec_args = ((None, n_out, D),)
    out_shape = [jax.ShapeDtypeStruct((M, D), BF16)] * n_out
    if grid_tokens:
        tm = _tile(M, 256)
        assert tm % GRID_W == 0 and M % GRID_W == 0
        r = tm // GRID_W
        nb = M // GRID_W
        return pl.pallas_call(
            functools.partial(_mix_lat_kernel, T=M),
            grid=(M // tm,),
            in_specs=[
                pl.BlockSpec((tm, D), lambda i: (i, 0)),
                pl.BlockSpec((GRID_W, D), lambda i: (jnp.maximum(i * r - 1, 0), 0)),
                pl.BlockSpec((GRID_W, D), lambda i: (jnp.minimum((i + 1) * r, nb - 1), 0)),
                pl.BlockSpec((8, D), lambda i: (0, 0)),
                pl.BlockSpec(*mu_spec_args, lambda i: (li, 0, 0)),
            ],
            out_specs=[pl.BlockSpec((tm, D), lambda i: (i, 0))] * n_out,
            out_shape=out_shape,
            compiler_params=_cparams("parallel"),
            name="rwkv_shiftmix_grid",
        )(x, x, x, mod, mu)
    return pl.pallas_call(
        _mix_ctx_kernel,
        grid=(1,),
        in_specs=[
            pl.BlockSpec((M, D), lambda i: (0, 0)),
            pl.BlockSpec((8, D), lambda i: (0, 0)),
            pl.BlockSpec(*mu_spec_args, lambda i: (li, 0, 0)),
        ],
        out_specs=[pl.BlockSpec((M, D), lambda i: (0, 0))] * n_out,
        out_shape=out_shape,
        compiler_params=_cparams("arbitrary"),
        name="rwkv_shiftmix_seq",
    )(x, mod, mu)


def _lora_kernel(xw_ref, xa_ref, xg_ref, w1_ref, w2_ref, w0_ref, a1_ref, a2_ref, a0_ref, g1_ref, g2_ref,
                 lw_ref, a_ref, g_ref, tw_ref, ta_ref, tg_ref):
    @pl.when(pl.program_id(1) == 0)
    def _():
        for d in (0, 1):
            tw_ref[d] = _bf(jnp.tanh(_dot(xw_ref[...], w1_ref[d])))
            ta_ref[d] = _bf(_dot(xa_ref[...], a1_ref[d]))
        tg_ref[...] = _bf(_sigmoid(_dot(xg_ref[...], g1_ref[...])))

    for d in (0, 1):
        z = w0_ref[d:d + 1] + _dot(tw_ref[d], w2_ref[d])
        lw_ref[d] = -math.exp(-0.5) * _sigmoid(z)
        a_ref[d] = _sigmoid(a0_ref[d:d + 1] + _dot(ta_ref[d], a2_ref[d]))
    g_ref[...] = _dot(tg_ref[...], g2_ref[...])


def rwkv_lora(xw, xa, xg, w1, w2, w0, a1, a2, a0, g1, g2, li):
    M, D = xw.shape
    RW, RA, RG = w1.shape[-1], a1.shape[-1], g1.shape[-1]
    tm = _tile(M, 256)
    tn = _tile(D, 1024)
    row = pl.BlockSpec((tm, D), lambda i, j: (i, 0))
    out2 = pl.BlockSpec((2, tm, tn), lambda i, j: (0, i, j))
    return pl.pallas_call(
        _lora_kernel,
        grid=(M // tm, D // tn),
        in_specs=[
            row, row, row,
            pl.BlockSpec((None, 2, D, RW), lambda i, j: (li, 0, 0, 0)),
            pl.BlockSpec((None, 2, RW, tn), lambda i, j: (li, 0, 0, j)),
            pl.BlockSpec((None, 2, tn), lambda i, j: (li, 0, j)),
            pl.BlockSpec((None, 2, D, RA), lambda i, j: (li, 0, 0, 0)),
            pl.BlockSpec((None, 2, RA, tn), lambda i, j: (li, 0, 0, j)),
            pl.BlockSpec((None, 2, tn), lambda i, j: (li, 0, j)),
            pl.BlockSpec((None, D, RG), lambda i, j: (li, 0, 0)),
            pl.BlockSpec((None, RG, tn), lambda i, j: (li, 0, j)),
        ],
        out_specs=[out2, out2, pl.BlockSpec((tm, tn), lambda i, j: (i, j))],
        out_shape=[jax.ShapeDtypeStruct((2, M, D), F32), jax.ShapeDtypeStruct((2, M, D), F32),
                   jax.ShapeDtypeStruct((M, D), F32)],
        scratch_shapes=[pltpu.VMEM((2, tm, RW), BF16), pltpu.VMEM((2, tm, RA), BF16), pltpu.VMEM((tm, RG), BF16)],
        compiler_params=_cparams("parallel", "arbitrary"),
        name="rwkv_lora",
    )(xw, xa, xg, w1, w2, w0, a1, a2, a0, g1, g2)


def _pair_masks():
    row = lax.broadcasted_iota(jnp.int32, (LANES, LANES), 0)
    col = lax.broadcasted_iota(jnp.int32, (LANES, LANES), 1)
    same_head = (row < WKV_HEAD) == (col < WKV_HEAD)
    return row, col, same_head


def _wkv_kernel(r_ref, k_ref, v_ref, lw_ref, a_ref, par_ref, s0_ref, y_ref, z_ref, s_ref,
                st_ref, rh_ref, y0_ref, g_ref, j_ref, et_ref, *, rev):
    C = CHUNK
    Tt = r_ref.shape[0]
    nchunks = Tt // C

    @pl.when(pl.program_id(1) == 0)
    def _():
        st_ref[...] = s0_ref[...]

    k_k, k_a, r_k = par_ref[0:1], par_ref[1:2], par_ref[2:3]
    row, col, same_head = _pair_masks()
    ones_head = jnp.where(same_head, 1.0, 0.0).astype(BF16)
    tr, tc = row & (C - 1), col & (C - 1)
    before = (tc > tr) if rev else (tc < tr)
    before_eq = (tc >= tr) if rev else (tc <= tr)
    r64 = lax.broadcasted_iota(jnp.int32, (C, C), 0)
    c64 = lax.broadcasted_iota(jnp.int32, (C, C), 1)
    cum_m = jnp.where((c64 >= r64) if rev else (c64 <= r64), 1.0, 0.0).astype(BF16)
    eye = jnp.where(row == col, 1.0, 0.0)
    lane = lax.broadcasted_iota(jnp.int32, (C, LANES), 1)
    head_a = lane < WKV_HEAD

    def stack2f(x):
        return jnp.concatenate([jnp.where(head_a, x, 0.0), jnp.where(head_a, 0.0, x)], axis=0)

    def unstack(x2):
        return x2[0:C] + x2[C:2 * C]

    blk16 = (row >> 4) == (col >> 4)
    blk32 = (row >> 5) == (col >> 5)

    def each(f, *lists):
        return [f(*xs) for xs in zip(*lists)]

    def square(m):
        return each(lambda x: _dot(_bf(x), _bf(x)), m)

    def times_one_plus(t, m):
        return each(lambda x, y: x + _dot(_bf(x), _bf(y)), t, m)

    def merge(t, a_kb, inside, outside):
        sel = jnp.logical_and(inside, jnp.logical_not(outside))
        tb = each(_bf, t)
        lt = each(lambda x, y: _bf(_dot(_bf(jnp.where(sel, x, 0.0)), y)), a_kb, tb)
        return each(lambda x, y, w: x - _dot(y, w), t, tb, lt)

    sls = [pl.ds(c * C, C) for c in range(nchunks)]
    r, k, v, lw, a = ([ref[sl, :] for sl in sls] for ref in (r_ref, k_ref, v_ref, lw_ref, a_ref))
    kk = each(lambda x: x * k_k, k)
    nrm = each(lambda x: _exact_right(x * x, ones_head, 2), kk)
    kk = each(lambda x, y: x * lax.rsqrt(jnp.maximum(y, 1e-24)), kk, nrm)
    kd = each(lambda x, y: x * (1.0 + (y - 1.0) * k_a), k, a)
    beta = each(lambda x, y: x * y, a, kk)
    bonus = each(lambda x, y: _exact_right(x * y * r_k, ones_head, 2), r, kd)
    for sl, x, y in zip(sls, bonus, v):
        z_ref[sl, :] = x * y
    b = each(lambda x: _exact_left(cum_m, x, 3), lw)
    b_tot = each(lambda x: x[0:1] if rev else x[C - 1:C], b)
    e_b = each(jnp.exp, b)
    e_nb = each(lambda x: jnp.exp(-x), b)
    e_rest = each(lambda x, y: jnp.exp(y - x), b, b_tot)
    r2f = each(lambda x, y: stack2f(x * y), r, e_b)
    r2 = each(_bf, r2f)
    kap2 = each(lambda x, y, w: _bf(stack2f(x * jnp.exp(y - w))), kk, b, lw)
    k2 = each(lambda x, y: _bf(stack2f(x * y)), kd, e_nb)
    be2 = each(lambda x, y: _bf(stack2f(x * y)), beta, e_nb)
    v2 = each(lambda x: _bf(stack2f(x)), v)
    kc2 = each(lambda x, y: _bf(stack2f(x * y)), kd, e_rest)
    bc2 = each(lambda x, y: _bf(stack2f(x * y)), beta, e_rest)
    kb2 = each(lambda x, y: jnp.concatenate([x, y], axis=0), k2, be2)
    g_kap = each(_dot_nt, kap2, kb2)
    a_kk = each(lambda x: _bf(jnp.where(before, x[:, :LANES], 0.0)), g_kap)
    a_kb = each(lambda x: jnp.where(before, x[:, LANES:], 0.0), g_kap)
    m1 = each(lambda x: jnp.where(blk16, -x, 0.0), a_kb)
    m2 = square(m1)
    g_r = each(_dot_nt, r2, kb2)
    m4 = square(m2)
    t = times_one_plus(each(lambda x: eye + x, m1), m2)
    a_rk = each(lambda x: _bf(jnp.where(before_eq, x[:, :LANES], 0.0)), g_r)
    a_rb = each(lambda x: _bf(jnp.where(before_eq, x[:, LANES:], 0.0)), g_r)
    m8 = square(m4)
    t = times_one_plus(t, m4)
    w0 = each(lambda x, y: _bf(_dot(x, y)), a_kk, v2)
    t = times_one_plus(t, m8)
    t = merge(t, a_kb, blk32, blk16)
    tinv = each(_bf, merge(t, a_kb, same_head, blk32))
    khu = each(lambda x, y, w: _bf(_dot(x, jnp.concatenate([y, w], axis=1))), tinv, kap2, w0)
    kh = each(lambda x: x[:, :LANES], khu)
    u0 = each(lambda x: x[:, LANES:], khu)
    vu = each(lambda x, y: jnp.concatenate([x, y], axis=0), v2, u0)
    rh = each(lambda x, y, w: _bf(x - _dot(y, w)), r2f, a_rb, kh)
    y0 = each(lambda x, y, w: _dot(jnp.concatenate([x, -y], axis=1), w), a_rk, a_rb, vu)
    jj = each(lambda x, y, w: _dot_tn(x, jnp.concatenate([y, -w], axis=0)), vu, kc2, bc2)
    gg = each(lambda x, y: _bf(-_dot_tn(x, y)), kh, bc2)
    for c in range(nchunks):
        rh_ref[c] = rh[c]
        y0_ref[c] = y0[c]
        j_ref[c] = jj[c]
        g_ref[c] = gg[c]
        et_ref[c] = jnp.broadcast_to(jnp.exp(b_tot[c]), (8, LANES))

    S = st_ref[...]
    for c in (range(nchunks - 1, -1, -1) if rev else range(nchunks)):
        Sb = _bf(S)
        y_ref[pl.ds(c * C, C), :] = unstack(_dot_nt(rh_ref[c], Sb) + y0_ref[c])
        S = S * et_ref[c, 0:1, :] + _dot(Sb, g_ref[c]) + j_ref[c]
    st_ref[...] = S

    @pl.when(pl.program_id(1) == pl.num_programs(1) - 1)
    def _():
        s_ref[...] = S


def wkv_scan(r, k, v, lw, a, par, s0, d):
    T, D = r.shape
    HP = D // LANES
    Tt = _tile(T, 512)
    NT = T // Tt
    nck = Tt // CHUNK
    rev = d == 1

    def tt(t):
        return NT - 1 - t if rev else t

    tok = pl.BlockSpec((Tt, LANES), lambda h, t: (tt(t), h))
    tok_d = pl.BlockSpec((None, Tt, LANES), lambda h, t: (d, tt(t), h))
    st = pl.BlockSpec((None, LANES, LANES), lambda h, t: (h, 0, 0))
    return pl.pallas_call(
        functools.partial(_wkv_kernel, rev=rev),
        grid=(HP, NT),
        in_specs=[tok, tok, tok, tok_d, tok_d, pl.BlockSpec((8, LANES), lambda h, t: (0, h)), st],
        out_specs=[tok, tok, st],
        out_shape=[jax.ShapeDtypeStruct((T, D), F32), jax.ShapeDtypeStruct((T, D), F32),
                   jax.ShapeDtypeStruct((HP, LANES, LANES), F32)],
        scratch_shapes=[
            pltpu.VMEM((LANES, LANES), F32),
            pltpu.VMEM((nck, LANES, LANES), BF16),
            pltpu.VMEM((nck, LANES, LANES), F32),
            pltpu.VMEM((nck, LANES, LANES), BF16),
            pltpu.VMEM((nck, LANES, LANES), F32),
            pltpu.VMEM((nck, 8, LANES), F32),
        ],
        compiler_params=_cparams("parallel", "arbitrary"),
        name="wkv_scan",
    )(r, k, v, lw, a, par, s0)


def _rwkv_combine_kernel(y0_ref, y1_ref, z0_ref, z1_ref, g_ref, ln_ref, o_ref):
    D = o_ref.shape[1]
    row = lax.broadcasted_iota(jnp.int32, (LANES, LANES), 0)
    col = lax.broadcasted_iota(jnp.int32, (LANES, LANES), 1)
    mean_m = jnp.where((row < WKV_HEAD) == (col < WKV_HEAD), 1.0 / WKV_HEAD, 0.0).astype(BF16)
    for c in range(D // LANES):
        cs = slice(c * LANES, (c + 1) * LANES)
        y = y0_ref[:, cs] + y1_ref[:, cs]
        yc = y - _exact_right(y, mean_m, 2)
        var = _exact_right(yc * yc, mean_m, 2)
        out = yc * lax.rsqrt(var + GN_EPS) * ln_ref[0:1, cs] + ln_ref[1:2, cs] + (z0_ref[:, cs] + z1_ref[:, cs])
        o_ref[:, cs] = _bf(out * g_ref[:, cs])


def rwkv_combine(y0, y1, z0, z1, g, ln):
    M, D = g.shape
    tm = _tile(M, 256)
    tok = pl.BlockSpec((tm, D), lambda i: (i, 0))
    return pl.pallas_call(
        _rwkv_combine_kernel,
        grid=(M // tm,),
        in_specs=[tok] * 5 + [pl.BlockSpec((8, D), lambda i: (0, 0))],
        out_specs=tok,
        out_shape=jax.ShapeDtypeStruct((M, D), BF16),
        compiler_params=_cparams("parallel"),
        name="rwkv_combine",
    )(y0, y1, z0, z1, g, ln)


def _gla_kernel(q_ref, i_ref, f_ref, lbl_ref, s0_ref, o_ref, s_ref, st_ref, qe_ref, j_ref, et_ref, *, rev, layer):
    C = CHUNK
    Tt = q_ref.shape[0]
    nchunks = Tt // C
    nsub = C // SUB

    @pl.when(pl.program_id(1) == 0)
    def _():
        st_ref[...] = s0_ref[...]

    logits = lbl_ref[...]
    e = jnp.exp(logits - jnp.max(logits, axis=0, keepdims=True))
    p = e / jnp.sum(e, axis=0, keepdims=True)
    lb = jnp.zeros((1, LANES), F32)
    for l in range(1, layer + 1):
        lb = lb + p[l:l + 1]
    log_lb = jnp.log(lb)
    log_1m = jnp.log1p(-lb)

    r64 = lax.broadcasted_iota(jnp.int32, (C, C), 0)
    c64 = lax.broadcasted_iota(jnp.int32, (C, C), 1)
    cum_m = jnp.where((c64 >= r64) if rev else (c64 <= r64), 1.0, 0.0).astype(BF16)
    rows = lax.broadcasted_iota(jnp.int32, (C, 1), 0)
    lane_s = lax.broadcasted_iota(jnp.int32, (SUB, C), 1)
    row_s = lax.broadcasted_iota(jnp.int32, (SUB, 1), 0)

    def each(f, *lists):
        return [f(*xs) for xs in zip(*lists)]

    sls = [pl.ds(c * C, C) for c in range(nchunks)]
    q = [_silu(q_ref[sl, :]) for sl in sls]
    v = [i_ref[sl, :] for sl in sls]
    f = [f_ref[sl, :] for sl in sls]
    kg = each(lambda x: (1.0 - lb) * _sigmoid(-x), f)
    x2 = each(lambda x: log_1m - _softplus(-x), f)
    g = each(lambda x: jnp.maximum(log_lb, x) + jnp.log1p(jnp.exp(-jnp.abs(log_lb - x))), x2)
    b = each(lambda x: _exact_left(cum_m, x, 3), g)
    b_tot = each(lambda x: x[0:1] if rev else x[C - 1:C], b)
    vb = each(_bf, v)
    att_rows = [[] for _ in range(nchunks)]
    for I in range(nsub):
        lo = I * SUB
        qI = each(lambda x: x[lo:lo + SUB], q)
        bI = each(lambda x: x[lo:lo + SUB], b)
        first = (I == nsub - 1) if rev else (I == 0)
        if first:
            att = [jnp.zeros((SUB, C), F32)] * nchunks
        else:
            ref = each(lambda x: x[lo + SUB:lo + SUB + 1] if rev else x[lo - 1:lo], b)
            earlier = (rows >= lo + SUB) if rev else (rows < lo)
            kt = each(lambda x, y, w: _bf(jnp.where(earlier, x * jnp.exp(w - y), 0.0)), kg, b, ref)
            qt = each(lambda x, y, w: _bf(x * jnp.exp(y - w)), qI, bI, ref)
            att = each(_dot_nt, qt, kt)
        for j in range(SUB):
            s = lo + j
            place = jnp.logical_and(lane_s == s, (row_s <= j) if rev else (row_s >= j))
            pj = each(lambda x, y, w, u: x * y[s:s + 1] * jnp.exp(w - u[s:s + 1]), qI, kg, bI, b)
            att = each(lambda x, y: x + jnp.where(place, jnp.sum(y, axis=-1, keepdims=True), 0.0), att, pj)
        for c in range(nchunks):
            att_rows[c].append(att[c])
    o_in = each(lambda x, y: _dot(_bf(jnp.concatenate(x, axis=0)), y), att_rows, vb)
    for c in range(nchunks):
        o_ref[sls[c], :] = o_in[c]
        qe_ref[c] = _bf(q[c] * jnp.exp(b[c]))
        j_ref[c] = _dot_tn(vb[c], _bf(kg[c] * jnp.exp(b_tot[c] - b[c])))
        et_ref[c] = jnp.broadcast_to(jnp.exp(b_tot[c]), (8, LANES))

    Z = st_ref[...]
    for c in (range(nchunks - 1, -1, -1) if rev else range(nchunks)):
        o_ref[sls[c], :] += _dot_nt(qe_ref[c], _bf(Z))
        Z = Z * et_ref[c, 0:1, :] + j_ref[c]
    st_ref[...] = Z

    @pl.when(pl.program_id(1) == pl.num_programs(1) - 1)
    def _():
        s_ref[...] = Z


def gla_scan(raw, lb_logits, s0, d, layer):
    T = raw.shape[0]
    D = raw.shape[1] // 5
    H = D // LANES
    Tt = _tile(T, 256)
    NT = T // Tt
    nck = Tt // CHUNK
    rev = d == 1
    L = lb_logits.shape[0]

    def tt(t):
        return NT - 1 - t if rev else t

    st = pl.BlockSpec((None, LANES, LANES), lambda h, t: (h, 0, 0))
    return pl.pallas_call(
        functools.partial(_gla_kernel, rev=rev, layer=layer),
        grid=(H, NT),
        in_specs=[
            pl.BlockSpec((Tt, LANES), lambda h, t: (tt(t), h)),
            pl.BlockSpec((Tt, LANES), lambda h, t: (tt(t), H + h)),
            pl.BlockSpec((Tt, LANES), lambda h, t: (tt(t), (2 + d) * H + h)),
            pl.BlockSpec((L, LANES), lambda h, t: (0, h)),
            st,
        ],
        out_specs=[pl.BlockSpec((Tt, LANES), lambda h, t: (tt(t), h)), st],
        out_shape=[jax.ShapeDtypeStruct((T, D), F32), jax.ShapeDtypeStruct((H, LANES, LANES), F32)],
        scratch_shapes=[
            pltpu.VMEM((LANES, LANES), F32),
            pltpu.VMEM((nck, CHUNK, LANES), BF16),
            pltpu.VMEM((nck, LANES, LANES), F32),
            pltpu.VMEM((nck, 8, LANES), F32),
        ],
        compiler_params=_cparams("parallel", "arbitrary"),
        name="gla_scan",
    )(raw, raw, raw, lb_logits, s0)


def _hgrn_combine_kernel(o0_ref, o1_ref, g_ref, gn_ref, o_ref):
    D = o_ref.shape[1]
    for c in range(D // LANES):
        cs = slice(c * LANES, (c + 1) * LANES)
        o = o0_ref[:, cs] + o1_ref[:, cs]
        o = o * lax.rsqrt(jnp.mean(o * o, axis=-1, keepdims=True) + EPS) * gn_ref[0:1, cs]
        o_ref[:, cs] = _bf(o * _silu(g_ref[:, cs]))


def hgrn_combine(o0, o1, raw, gn):
    M, D = o0.shape
    tm = _tile(M, 256)
    tok = pl.BlockSpec((tm, D), lambda i: (i, 0))
    return pl.pallas_call(
        _hgrn_combine_kernel,
        grid=(M // tm,),
        in_specs=[tok, tok, pl.BlockSpec((tm, D), lambda i: (i, 4)), pl.BlockSpec((1, D), lambda i: (0, 0))],
        out_specs=tok,
        out_shape=jax.ShapeDtypeStruct((M, D), BF16),
        compiler_params=_cparams("parallel"),
        name="hgrn_combine",
    )(o0, o1, raw, gn)


def _rows8(*vecs):
    D = vecs[0].shape[-1]
    rows = [v.reshape(1, D) for v in vecs]
    rows.append(jnp.zeros((8 - len(rows), D), F32))
    return jnp.concatenate(rows, axis=0)


def kernel(x, c, ctx, c_ctx, ada_down, ada_up, ada_b, norm_g, ffn_w13, ffn_w2, final_g, lru_w_in, lru_conv_w, lru_conv_b, lru_gate_a_w, lru_gate_a_b, lru_gate_x_w, lru_gate_x_b, lru_lam, lru_w_out, rwkv_mu, rwkv_w_r, rwkv_w_k, rwkv_w_v, rwkv_w_o, rwkv_w0, rwkv_w1, rwkv_w2, rwkv_a0, rwkv_a1, rwkv_a2, rwkv_g1, rwkv_g2, rwkv_k_k, rwkv_k_a, rwkv_r_k, rwkv_ln_w, rwkv_ln_b, hgrn_w_in, hgrn_lb_logits, hgrn_gn_g, hgrn_w_out):
    B, T, D = x.shape
    assert B == 1, "one sequence per call"
    depth = ada_down.shape[0]
    xl, xc = x[0], ctx[0]

    mods = ada_all_layers(_rows8(c[0], c_ctx), ada_down, ada_up, ada_b)
    mods = mods[:, :2].reshape(depth, 2, N_MOD, D)

    lru_in, lru_out = _bf(lru_w_in), _bf(lru_w_out)
    lru_wa, lru_wx = _bf(lru_gate_a_w), _bf(lru_gate_x_w)
    w_r, w_k, w_v, w_o = _bf(rwkv_w_r), _bf(rwkv_w_k), _bf(rwkv_w_v), _bf(rwkv_w_o)
    lw1, lw2, la1, la2 = _bf(rwkv_w1), _bf(rwkv_w2), _bf(rwkv_a1), _bf(rwkv_a2)
    lg1, lg2 = _bf(rwkv_g1), _bf(rwkv_g2)
    h_in, h_out = _bf(hgrn_w_in), _bf(hgrn_w_out)

    for i in range(depth):
        need_ctx = i < depth - 1
        kind, j = i % 3, i // 3
        ml, mc = mods[i, 0], mods[i, 1]

        def ffn_mod(m, k, which):
            return _rows8(norm_g[i, 2 * which], m[k], m[k + 1], m[k + 2])

        xc, *wb = half_ffn(xc, ffn_mod(mc, 0, 0), (ffn_w13, ffn_w2), (i, 0))
        xl = half_ffn(xl, ffn_mod(ml, 0, 0), wb)
        mod_l = _rows8(norm_g[i, 1], ml[3], ml[4])
        mod_c = _rows8(norm_g[i, 1], mc[3], mc[4])
        gate_l, gate_c = _rows8(ml[5]), _rows8(mc[5])

        if kind == 0:
            p_l = proj(xl, lru_in, (j,), mod=mod_l, gelu_cols=D)
            p_c = proj(xc, lru_in, (j,), mod=mod_c, gelu_cols=D)
            hs = lru_scan(p_l, p_c, lru_conv_w, lru_conv_b, lru_wa, lru_wx, lru_gate_a_b, lru_gate_x_b, lru_lam,
                          j, need_ctx)
            a_l = lru_combine(p_l, hs[0])
            a_c = lru_combine(p_c, hs[1]) if need_ctx else None
            w_last, w_idx = lru_out, (j,)
        elif kind == 1:
            par = _rows8(rwkv_k_k[j], rwkv_k_a[j], rwkv_r_k[j].reshape(D))
            ln = _rows8(rwkv_ln_w[j], rwkv_ln_b[j])
            outs = []
            state = [jnp.zeros((D // LANES, LANES, LANES), F32)] * 2
            for xs, mod, is_grid in ((xc, mod_c, False), (xl, mod_l, True)):
                xr, xw, xk, xv, xa, xg = rwkv_shiftmix(xs, mod, rwkv_mu, j, is_grid)
                r = proj(xr, w_r, (j,))
                k = proj(xk, w_k, (j,))
                v = proj(xv, w_v, (j,))
                lw, a, g = rwkv_lora(xw, xa, xg, lw1, lw2, rwkv_w0, la1, la2, rwkv_a0, lg1, lg2, j)
                ys, zs = [], []
                for d in (0, 1):
                    y, z, state[d] = wkv_scan(r, k, v, lw, a, par, state[d], d)
                    ys.append(y)
                    zs.append(z)
                outs.append(rwkv_combine(ys[0], ys[1], zs[0], zs[1], g, ln))
            a_c, a_l = outs
            w_last, w_idx = w_o, (j,)
        else:
            outs = []
            state = [jnp.zeros((D // LANES, LANES, LANES), F32)] * 2
            for xs, mod in ((xc, mod_c), (xl, mod_l)):
                raw = proj(xs, h_in, (j,), mod=mod)
                os_ = []
                for d in (0, 1):
                    o, state[d] = gla_scan(raw, hgrn_lb_logits, state[d], d, i)
                    os_.append(o)
                outs.append(hgrn_combine(os_[0], os_[1], raw, hgrn_gn_g[j].reshape(1, D)))
            a_c, a_l = outs
            w_last, w_idx = h_out, (j,)

        xl = proj(a_l, w_last, w_idx, res=xl, gate=gate_l)
        if need_ctx:
            xc = proj(a_c, w_last, w_idx, res=xc, gate=gate_c)
            xc, *wb = half_ffn(xc, ffn_mod(mc, 6, 1), (ffn_w13, ffn_w2), (i, 1))
        else:
            F = ffn_w2.shape[2]
            wb = (_bf(ffn_w13[i, 1, :, :F]), _bf(ffn_w13[i, 1, :, F:]), _bf(ffn_w2[i, 1]))
        xl = half_ffn(xl, ffn_mod(ml, 6, 1), wb)

    return final_rmsnorm(xl, final_g)[None]
```

```python
import functools
import math

import jax
import jax.numpy as jnp
from jax import lax
from jax.experimental import pallas as pl
from jax.experimental.pallas import tpu as pltpu

F32 = jnp.float32
BF16 = jnp.bfloat16

EPS = 1e-6
GN_EPS = 64e-5
LRU_C = 8.0
GRID_W = 64
N_MOD = 9
WKV_HEAD = 64
GLA_HEAD = 128
CHUNK = 64
SUB = 16
LANES = 128
VMEM_LIMIT_BYTES = 56 * 1024 * 1024


def _cparams(*sem):
    return pltpu.CompilerParams(dimension_semantics=sem, vmem_limit_bytes=VMEM_LIMIT_BYTES)


def _tile(n, pref):
    if n <= pref:
        return n
    for t in range(pref, 7, -1):
        if n % t == 0 and t % 8 == 0:
            return t
    return n


def _bf(x):
    return x.astype(BF16)


def _dot(a, b):
    return jnp.dot(a, b, preferred_element_type=F32)


def _dot_nt(a, b):
    return lax.dot_general(a, b, (((1,), (1,)), ((), ())), preferred_element_type=F32)


def _dot_tn(a, b):
    return lax.dot_general(a, b, (((0,), (0,)), ((), ())), preferred_element_type=F32)


def _split_terms(x, terms):
    out, rem = [], x
    for _ in range(terms):
        p = _bf(rem)
        out.append(p)
        rem = rem - p.astype(F32)
    return out


def _exact_left(m_bf, x, terms):
    acc = None
    for p in _split_terms(x, terms):
        d = _dot(m_bf, p)
        acc = d if acc is None else acc + d
    return acc


def _exact_right(x, m_bf, terms):
    acc = None
    for p in _split_terms(x, terms):
        d = _dot(p, m_bf)
        acc = d if acc is None else acc + d
    return acc


def _sigmoid(x):
    return 0.5 * jnp.tanh(0.5 * x) + 0.5


def _silu(x):
    return x * _sigmoid(x)


def _gelu_tanh(x):
    return 0.5 * x * (1.0 + jnp.tanh(math.sqrt(2.0 / math.pi) * (x + 0.044715 * (x * x * x))))


def _softplus(x):
    return jnp.maximum(x, 0.0) + jnp.log1p(jnp.exp(-jnp.abs(x)))


def _normmod(x, g, shift, scale):
    ms = jnp.mean(x * x, axis=-1, keepdims=True)
    return (x * lax.rsqrt(ms + EPS) * g) * (1.0 + scale) + shift


def _ada_kernel(cc_ref, down_ref, up_ref, b_ref, o_ref, t_ref):
    hi = lax.Precision.HIGHEST

    @pl.when(pl.program_id(1) == 0)
    def _():
        t_ref[...] = jnp.dot(_silu(cc_ref[...]), down_ref[...], precision=hi, preferred_element_type=F32)

    o_ref[...] = jnp.dot(t_ref[...], up_ref[...], precision=hi, preferred_element_type=F32) + b_ref[...]


def ada_all_layers(cc, down, up, bias):
    L, D, R = down.shape
    N = up.shape[2]
    tn = _tile(N, 4096)
    return pl.pallas_call(
        _ada_kernel,
        grid=(L, N // tn),
        in_specs=[
            pl.BlockSpec((8, D), lambda l, j: (0, 0)),
            pl.BlockSpec((None, D, R), lambda l, j: (l, 0, 0)),
            pl.BlockSpec((None, R, tn), lambda l, j: (l, 0, j)),
            pl.BlockSpec((None, 1, tn), lambda l, j: (l, 0, j)),
        ],
        out_specs=pl.BlockSpec((None, 8, tn), lambda l, j: (l, 0, j)),
        out_shape=jax.ShapeDtypeStruct((L, 8, N), F32),
        scratch_shapes=[pltpu.VMEM((8, R), F32)],
        compiler_params=_cparams("parallel", "arbitrary"),
        name="ada",
    )(cc, down, up, bias.reshape(L, 1, N))


def _ffn_kernel(x_ref, mod_ref, w1_ref, w3_ref, w2_ref, o_ref, *rest, emit_weights):
    h_ref = rest[-1]
    f = pl.program_id(1)

    @pl.when(f == 0)
    def _():
        h_ref[...] = _bf(_normmod(x_ref[...], mod_ref[0:1], mod_ref[1:2], mod_ref[2:3]))
        o_ref[...] = jnp.zeros_like(o_ref)

    w1, w3, w2 = _bf(w1_ref[...]), _bf(w3_ref[...]), _bf(w2_ref[...])
    if emit_weights:
        for ref, w in zip(rest[:3], (w1, w3, w2)):
            ref[...] = w
    h = h_ref[...]
    act = _bf(_silu(_dot(h, w1)) * _dot(h, w3))
    o_ref[...] += _dot(act, w2)

    @pl.when(f == pl.num_programs(1) - 1)
    def _():
        o_ref[...] = x_ref[...] + 0.5 * mod_ref[3:4] * o_ref[...]


def half_ffn(x, mod, weights, index=None):
    M, D = x.shape
    emit = index is not None
    F = weights[-1].shape[-2]
    tm = _tile(M, 512)
    tf = _tile(F, 256)
    nf = F // tf
    if emit:
        assert M == tm
        w13, w2 = weights
        lead = tuple(index)
        w_specs = [
            pl.BlockSpec((None, None, D, tf), lambda i, f: lead + (0, f)),
            pl.BlockSpec((None, None, D, tf), lambda i, f: lead + (0, nf + f)),
            pl.BlockSpec((None, None, tf, D), lambda i, f: lead + (f, 0)),
        ]
        w_args = (w13, w13, w2)
    else:
        w_specs = [
            pl.BlockSpec((D, tf), lambda i, f: (0, f)),
            pl.BlockSpec((D, tf), lambda i, f: (0, f)),
            pl.BlockSpec((tf, D), lambda i, f: (f, 0)),
        ]
        w_args = tuple(weights)
    out_specs = [pl.BlockSpec((tm, D), lambda i, f: (i, 0))]
    out_shape = [jax.ShapeDtypeStruct((M, D), F32)]
    if emit:
        out_specs += [
            pl.BlockSpec((D, tf), lambda i, f: (0, f)),
            pl.BlockSpec((D, tf), lambda i, f: (0, f)),
            pl.BlockSpec((tf, D), lambda i, f: (f, 0)),
        ]
        out_shape += [jax.ShapeDtypeStruct((D, F), BF16), jax.ShapeDtypeStruct((D, F), BF16),
                      jax.ShapeDtypeStruct((F, D), BF16)]
    out = pl.pallas_call(
        functools.partial(_ffn_kernel, emit_weights=emit),
        grid=(M // tm, nf),
        in_specs=[
            pl.BlockSpec((tm, D), lambda i, f: (i, 0), pipeline_mode=pl.Buffered(1)),
            pl.BlockSpec((8, D), lambda i, f: (0, 0)),
        ] + w_specs,
        out_specs=out_specs,
        out_shape=out_shape,
        scratch_shapes=[pltpu.VMEM((tm, D), BF16)],
        compiler_params=_cparams("parallel", "arbitrary"),
        name="half_ffn",
    )(x, mod, *w_args)
    return out if emit else out[0]


def _round_weights_kernel(w1_ref, w3_ref, w2_ref, w1b_ref, w3b_ref, w2b_ref):
    w1b_ref[...] = _bf(w1_ref[...])
    w3b_ref[...] = _bf(w3_ref[...])
    w2b_ref[...] = _bf(w2_ref[...])


def round_ffn_weights(w13, w2, index):
    D, F = w2.shape[-1], w2.shape[-2]
    tf = _tile(F, 256)
    nf = F // tf
    lead = tuple(index)
    return pl.pallas_call(
        _round_weights_kernel,
        grid=(nf,),
        in_specs=[
            pl.BlockSpec((None, None, D, tf), lambda f: lead + (0, f)),
            pl.BlockSpec((None, None, D, tf), lambda f: lead + (0, nf + f)),
            pl.BlockSpec((None, None, tf, D), lambda f: lead + (f, 0)),
        ],
        out_specs=[
            pl.BlockSpec((D, tf), lambda f: (0, f)),
            pl.BlockSpec((D, tf), lambda f: (0, f)),
            pl.BlockSpec((tf, D), lambda f: (f, 0)),
        ],
        out_shape=[jax.ShapeDtypeStruct((D, F), BF16), jax.ShapeDtypeStruct((D, F), BF16),
                   jax.ShapeDtypeStruct((F, D), BF16)],
        compiler_params=_cparams("parallel"),
        name="round_ffn_weights",
    )(w13, w13, w2)


def _proj_kernel(*refs, norm, n_gelu, residual, emit_weight):
    it = iter(refs)
    a_ref = next(it)
    mod_ref = next(it) if norm else None
    w_ref = next(it)
    res_ref = next(it) if residual else None
    gate_ref = next(it) if residual else None
    o_ref = next(it)
    wb_ref = next(it) if emit_weight else None
    h_ref = next(it) if norm else None
    j = pl.program_id(1)

    if norm:
        @pl.when(j == 0)
        def _():
            h_ref[...] = _bf(_normmod(a_ref[...], mod_ref[0:1], mod_ref[1:2], mod_ref[2:3]))
        lhs = h_ref[...]
    else:
        lhs = a_ref[...]
    w = _bf(w_ref[...])
    if emit_weight:
        wb_ref[...] = w
    acc = _dot(lhs, w)
    if residual:
        o_ref[...] = res_ref[...] + gate_ref[0:1] * acc
    elif n_gelu:
        @pl.when(j < n_gelu)
        def _():
            o_ref[...] = _gelu_tanh(acc)

        @pl.when(j >= n_gelu)
        def _():
            o_ref[...] = acc
    else:
        o_ref[...] = acc


def proj(a, w, w_index=(), *, mod=None, gelu_cols=0, res=None, gate=None, emit_weight=False):
    M, K = a.shape
    N = w.shape[-1]
    tm = _tile(M, 512)
    tn = _tile(math.gcd(N, gelu_cols) if gelu_cols else N, 512 if emit_weight else 1024)
    norm = mod is not None
    residual = res is not None
    assert gelu_cols % tn == 0 and (M == tm or not emit_weight)
    lead = tuple(w_index)
    in_specs = [pl.BlockSpec((tm, K), lambda i, j: (i, 0))]
    args = [a]
    if norm:
        in_specs.append(pl.BlockSpec((8, K), lambda i, j: (0, 0)))
        args.append(mod)
    in_specs.append(pl.BlockSpec((None,) * len(lead) + (K, tn), lambda i, j: lead + (0, j)))
    args.append(w)
    if residual:
        in_specs.append(pl.BlockSpec((tm, tn), lambda i, j: (i, j)))
        in_specs.append(pl.BlockSpec((8, tn), lambda i, j: (0, j)))
        args += [res, gate]
    out_specs = [pl.BlockSpec((tm, tn), lambda i, j: (i, j))]
    out_shape = [jax.ShapeDtypeStruct((M, N), F32)]
    if emit_weight:
        out_specs.append(pl.BlockSpec((K, tn), lambda i, j: (0, j)))
        out_shape.append(jax.ShapeDtypeStruct((K, N), BF16))
    out = pl.pallas_call(
        functools.partial(_proj_kernel, norm=norm, n_gelu=gelu_cols // tn, residual=residual,
                          emit_weight=emit_weight),
        grid=(M // tm, N // tn),
        in_specs=in_specs,
        out_specs=out_specs,
        out_shape=out_shape,
        scratch_shapes=[pltpu.VMEM((tm, K), BF16)] if norm else [],
        compiler_params=_cparams("parallel", "arbitrary"),
        name="proj",
    )(*args)
    return out if emit_weight else out[0]


def _rmsnorm_kernel(x_ref, g_ref, o_ref):
    x = x_ref[...]
    ms = jnp.mean(x * x, axis=-1, keepdims=True)
    o_ref[...] = x * lax.rsqrt(ms + EPS) * g_ref[...]


def final_rmsnorm(x, g):
    M, D = x.shape
    tm = _tile(M, 512)
    return pl.pallas_call(
        _rmsnorm_kernel,
        grid=(M // tm,),
        in_specs=[pl.BlockSpec((tm, D), lambda i: (i, 0)), pl.BlockSpec((1, D), lambda i: (0, 0))],
        out_specs=pl.BlockSpec((tm, D), lambda i: (i, 0)),
        out_shape=jax.ShapeDtypeStruct((M, D), F32),
        compiler_params=_cparams("parallel"),
        name="final_rmsnorm",
    )(x, g.reshape(1, D))


def _lru_kernel(ul_ref, uc_ref, cw_ref, cb_ref, wa_ref, wx_ref, ba_ref, bx_ref, lam_ref, *refs, R, need_ctx):
    hl_ref = refs[0]
    hc_ref = refs[1] if need_ctx else None
    v_ref, hs_ref = refs[-2:]
    T, W = ul_ref.shape
    NL = W // LANES
    Tc = uc_ref.shape[0]
    cw = cw_ref[...]
    cb = cb_ref[...]

    def conv_chunk(u_ref, t0, n_rows):
        main = u_ref[pl.ds(t0, R), :]
        p0 = pl.multiple_of(jnp.maximum(t0 - 8, 0), 8)
        n0 = pl.multiple_of(jnp.minimum(t0 + R, n_rows - 8), 8)
        prev = jnp.where(t0 > 0, u_ref[pl.ds(p0, 8), :], 0.0)
        nxt = jnp.where(t0 + R < n_rows, u_ref[pl.ds(n0, 8), :], 0.0)
        ext = jnp.concatenate([prev, main, nxt], axis=0)
        n = R + 16
        out = cb + ext[8:8 + R] * cw[2:3]
        out = out + pltpu.roll(ext, 2, 0)[8:8 + R] * cw[0:1]
        out = out + pltpu.roll(ext, 1, 0)[8:8 + R] * cw[1:2]
        out = out + pltpu.roll(ext, n - 1, 0)[8:8 + R] * cw[3:4]
        return out

    G = R // 8

    def scan_chunk(d, carry, rev):
        vp = jnp.concatenate(
            [jnp.concatenate([v_ref[j, pl.ds(g, 8, stride=G), :] for j in range(NL)], axis=1) for g in range(G)],
            axis=0)
        vb = _bf(vp)
        r = _sigmoid(_dot(vb, wa_ref[d]) + ba_ref[d:d + 1])
        gi = _sigmoid(_dot(vb, wx_ref[d]) + bx_ref[d:d + 1])
        log_a = -LRU_C * r * _softplus(-lam_ref[d:d + 1])
        a = jnp.exp(log_a)
        b = jnp.sqrt(-jnp.tanh(log_a) * (a * a + 1.0)) * (gi * vp)
        h = jnp.zeros((8, W), F32)
        p = jnp.ones((8, W), F32)
        hs, ps = [None] * G, [None] * G
        for g in (range(G - 1, -1, -1) if rev else range(G)):
            ag = a[8 * g:8 * g + 8]
            h = ag * h + b[8 * g:8 * g + 8]
            p = ag * p
            hs[g], ps[g] = h, p
        starts = [None] * 8
        for s in (range(7, -1, -1) if rev else range(8)):
            starts[s] = carry
            carry = h[s:s + 1] + p[s:s + 1] * carry
        start = jnp.concatenate(starts, axis=0)
        return [hs[g] + ps[g] * start for g in range(G)], carry

    def run(u_ref, o_ref, n_rows, d, carry):
        rev = d == 1
        nchunks = n_rows // R

        def body(i, carry):
            c = nchunks - 1 - i if rev else i
            t0 = pl.multiple_of(c * R, R)
            v = conv_chunk(u_ref, t0, n_rows)
            for j in range(NL):
                v_ref[j] = v[:, j * LANES:(j + 1) * LANES]
            hs, carry = scan_chunk(d, carry, rev)
            if o_ref is not None:
                for g in range(G):
                    for j in range(NL):
                        hs_ref[j, pl.ds(g, 8, stride=G), :] = hs[g][:, j * LANES:(j + 1) * LANES]
                h = jnp.concatenate([hs_ref[j] for j in range(NL)], axis=1)
                if d == 0:
                    o_ref[pl.ds(t0, R), :] = h
                else:
                    o_ref[pl.ds(t0, R), :] += h
            return carry

        return lax.fori_loop(0, nchunks, body, carry)

    for d in (0, 1):
        carry = run(uc_ref, hc_ref, Tc, d, jnp.zeros((1, W), F32))
        run(ul_ref, hl_ref, T, d, carry)


def lru_scan(pl_out, pc_out, conv_w, conv_b, wa, wx, ba, bx, lam, li, need_ctx):
    T = pl_out.shape[0]
    Tc = pc_out.shape[0]
    D = pl_out.shape[1] // 2
    NH, W = wa.shape[2], wa.shape[3]
    assert W % LANES == 0 and NH * W == D
    R = _tile(math.gcd(T, Tc), 256)
    col = D // W
    vec2 = pl.BlockSpec((None, 2, W), lambda h: (li, 0, h))
    out_shape = [jax.ShapeDtypeStruct((T, D), F32)]
    out_specs = [pl.BlockSpec((T, W), lambda h: (0, h))]
    if need_ctx:
        out_shape.append(jax.ShapeDtypeStruct((Tc, D), F32))
        out_specs.append(pl.BlockSpec((Tc, W), lambda h: (0, h)))
    return pl.pallas_call(
        functools.partial(_lru_kernel, R=R, need_ctx=need_ctx),
        grid=(NH,),
        in_specs=[
            pl.BlockSpec((T, W), lambda h: (0, col + h)),
            pl.BlockSpec((Tc, W), lambda h: (0, col + h)),
            pl.BlockSpec((None, 4, W), lambda h: (li, 0, h)),
            pl.BlockSpec((None, 1, W), lambda h: (li, 0, h)),
            pl.BlockSpec((None, 2, None, W, W), lambda h: (li, 0, h, 0, 0)),
            pl.BlockSpec((None, 2, None, W, W), lambda h: (li, 0, h, 0, 0)),
            vec2, vec2, vec2,
        ],
        out_specs=out_specs,
        out_shape=out_shape,
        scratch_shapes=[pltpu.VMEM((W // LANES, R, LANES), F32)] * 2,
        compiler_params=_cparams("parallel"),
        name="lru_scan",
    )(pl_out, pc_out, conv_w, conv_b.reshape(conv_b.shape[0], 1, D), wa, wx, ba, bx, lam)


def _mul_kernel(a_ref, b_ref, o_ref):
    o_ref[...] = _bf(a_ref[...] * b_ref[...])


def lru_combine(p_out, hs):
    M, D = hs.shape
    tm = _tile(M, 256)
    return pl.pallas_call(
        _mul_kernel,
        grid=(M // tm,),
        in_specs=[pl.BlockSpec((tm, D), lambda i: (i, 0)), pl.BlockSpec((tm, D), lambda i: (i, 0))],
        out_specs=pl.BlockSpec((tm, D), lambda i: (i, 0)),
        out_shape=jax.ShapeDtypeStruct((M, D), BF16),
        compiler_params=_cparams("parallel"),
        name="lru_combine",
    )(p_out, hs)


def _mix_write(h, shifted, mu_ref, o_refs, cols):
    xx = shifted - h
    for n, o_ref in enumerate(o_refs):
        o_ref[:, cols] = _bf(h + xx * mu_ref[n:n + 1, cols])


def _mix_lat_kernel(xm_ref, xp_ref, xn_ref, mod_ref, mu_ref, *o_refs, T):
    tm, D = xm_ref.shape
    q = D // 4
    i = pl.program_id(0)
    g, sh, sc = mod_ref[0:1], mod_ref[1:2], mod_ref[2:3]
    hm = _normmod(xm_ref[...], g, sh, sc)
    hp = _normmod(xp_ref[...], g, sh, sc)
    hn = _normmod(xn_ref[...], g, sh, sc)
    row = lax.broadcasted_iota(jnp.int32, (tm, 1), 0)
    t = i * tm + row
    colpos = row & (GRID_W - 1)
    s0 = slice(0, q)
    left = jnp.where(colpos > 0, pltpu.roll(hm[:, s0], 1, 0), 0.0)
    _mix_write(hm[:, s0], left, mu_ref, o_refs, s0)
    s1 = slice(q, 2 * q)
    right = jnp.where(colpos < GRID_W - 1, pltpu.roll(hm[:, s1], tm - 1, 0), 0.0)
    _mix_write(hm[:, s1], right, mu_ref, o_refs, s1)
    s2 = slice(2 * q, 3 * q)
    up = jnp.concatenate([hp[:, s2], hm[:tm - GRID_W, s2]], axis=0) if tm > GRID_W else hp[:, s2]
    up = jnp.where(t >= GRID_W, up, 0.0)
    _mix_write(hm[:, s2], up, mu_ref, o_refs, s2)
    s3 = slice(3 * q, D)
    down = jnp.concatenate([hm[GRID_W:, s3], hn[:, s3]], axis=0) if tm > GRID_W else hn[:, s3]
    down = jnp.where(t < T - GRID_W, down, 0.0)
    _mix_write(hm[:, s3], down, mu_ref, o_refs, s3)


def _mix_ctx_kernel(x_ref, mod_ref, mu_ref, *o_refs):
    Tc, D = x_ref.shape
    hh = D // 2
    h = _normmod(x_ref[...], mod_ref[0:1], mod_ref[1:2], mod_ref[2:3])
    row = lax.broadcasted_iota(jnp.int32, (Tc, 1), 0)
    s0 = slice(0, hh)
    prev = jnp.where(row > 0, pltpu.roll(h[:, s0], 1, 0), 0.0)
    _mix_write(h[:, s0], prev, mu_ref, o_refs, s0)
    s1 = slice(hh, D)
    nxt = jnp.where(row < Tc - 1, pltpu.roll(h[:, s1], Tc - 1, 0), 0.0)
    _mix_write(h[:, s1], nxt, mu_ref, o_refs, s1)


def rwkv_shiftmix(x, mod, mu, li, grid_tokens):
    M, D = x.shape
    n_out = mu.shape[1]
    mu_spec_args = ((None, n_out, D),)
    out_shape = [jax.ShapeDtypeStruct((M, D), BF16)] * n_out
    if grid_tokens:
        tm = _tile(M, 256)
        assert tm % GRID_W == 0 and M % GRID_W == 0
        r = tm // GRID_W
        nb = M // GRID_W
        return pl.pallas_call(
            functools.partial(_mix_lat_kernel, T=M),
            grid=(M // tm,),
            in_specs=[
                pl.BlockSpec((tm, D), lambda i: (i, 0)),
                pl.BlockSpec((GRID_W, D), lambda i: (jnp.maximum(i * r - 1, 0), 0)),
                pl.BlockSpec((GRID_W, D), lambda i: (jnp.minimum((i + 1) * r, nb - 1), 0)),
                pl.BlockSpec((8, D), lambda i: (0, 0)),
                pl.BlockSpec(*mu_spec_args, lambda i: (li, 0, 0)),
            ],
            out_specs=[pl.BlockSpec((tm, D), lambda i: (i, 0))] * n_out,
            out_shape=out_shape,
            compiler_params=_cparams("parallel"),
            name="rwkv_shiftmix_grid",
        )(x, x, x, mod, mu)
    return pl.pallas_call(
        _mix_ctx_kernel,
        grid=(1,),
        in_specs=[
            pl.BlockSpec((M, D), lambda i: (0, 0)),
            pl.BlockSpec((8, D), lambda i: (0, 0)),
            pl.BlockSpec(*mu_spec_args, lambda i: (li, 0, 0)),
        ],
        out_specs=[pl.BlockSpec((M, D), lambda i: (0, 0))] * n_out,
        out_shape=out_shape,
        compiler_params=_cparams("arbitrary"),
        name="rwkv_shiftmix_seq",
    )(x, mod, mu)


def _lora_kernel(xw_ref, xa_ref, xg_ref, w1_ref, w2_ref, w0_ref, a1_ref, a2_ref, a0_ref, g1_ref, g2_ref,
                 lw_ref, a_ref, g_ref, tw_ref, ta_ref, tg_ref):
    @pl.when(pl.program_id(1) == 0)
    def _():
        for d in (0, 1):
            tw_ref[d] = _bf(jnp.tanh(_dot(xw_ref[...], w1_ref[d])))
            ta_ref[d] = _bf(_dot(xa_ref[...], a1_ref[d]))
        tg_ref[...] = _bf(_sigmoid(_dot(xg_ref[...], g1_ref[...])))

    for d in (0, 1):
        z = w0_ref[d:d + 1] + _dot(tw_ref[d], w2_ref[d])
        lw_ref[d] = -math.exp(-0.5) * _sigmoid(z)
        a_ref[d] = _sigmoid(a0_ref[d:d + 1] + _dot(ta_ref[d], a2_ref[d]))
    g_ref[...] = _dot(tg_ref[...], g2_ref[...])


def rwkv_lora(xw, xa, xg, w1, w2, w0, a1, a2, a0, g1, g2, li):
    M, D = xw.shape
    RW, RA, RG = w1.shape[-1], a1.shape[-1], g1.shape[-1]
    tm = _tile(M, 256)
    tn = _tile(D, 1024)
    row = pl.BlockSpec((tm, D), lambda i, j: (i, 0))
    out2 = pl.BlockSpec((2, tm, tn), lambda i, j: (0, i, j))
    return pl.pallas_call(
        _lora_kernel,
        grid=(M // tm, D // tn),
        in_specs=[
            row, row, row,
            pl.BlockSpec((None, 2, D, RW), lambda i, j: (li, 0, 0, 0)),
            pl.BlockSpec((None, 2, RW, tn), lambda i, j: (li, 0, 0, j)),
            pl.BlockSpec((None, 2, tn), lambda i, j: (li, 0, j)),
            pl.BlockSpec((None, 2, D, RA), lambda i, j: (li, 0, 0, 0)),
            pl.BlockSpec((None, 2, RA, tn), lambda i, j: (li, 0, 0, j)),
            pl.BlockSpec((None, 2, tn), lambda i, j: (li, 0, j)),
            pl.BlockSpec((None, D, RG), lambda i, j: (li, 0, 0)),
            pl.BlockSpec((None, RG, tn), lambda i, j: (li, 0, j)),
        ],
        out_specs=[out2, out2, pl.BlockSpec((tm, tn), lambda i, j: (i, j))],
        out_shape=[jax.ShapeDtypeStruct((2, M, D), F32), jax.ShapeDtypeStruct((2, M, D), F32),
                   jax.ShapeDtypeStruct((M, D), F32)],
        scratch_shapes=[pltpu.VMEM((2, tm, RW), BF16), pltpu.VMEM((2, tm, RA), BF16), pltpu.VMEM((tm, RG), BF16)],
        compiler_params=_cparams("parallel", "arbitrary"),
        name="rwkv_lora",
    )(xw, xa, xg, w1, w2, w0, a1, a2, a0, g1, g2)


def _pair_masks():
    row = lax.broadcasted_iota(jnp.int32, (LANES, LANES), 0)
    col = lax.broadcasted_iota(jnp.int32, (LANES, LANES), 1)
    same_head = (row < WKV_HEAD) == (col < WKV_HEAD)
    return row, col, same_head


def _wkv_kernel(r_ref, k_ref, v_ref, lw_ref, a_ref, par_ref, s0_ref, y_ref, z_ref, s_ref,
                st_ref, rh_ref, y0_ref, g_ref, j_ref, et_ref, *, rev):
    C = CHUNK
    Tt = r_ref.shape[0]
    nchunks = Tt // C

    @pl.when(pl.program_id(1) == 0)
    def _():
        st_ref[...] = s0_ref[...]

    k_k, k_a, r_k = par_ref[0:1], par_ref[1:2], par_ref[2:3]
    row, col, same_head = _pair_masks()
    ones_head = jnp.where(same_head, 1.0, 0.0).astype(BF16)
    tr, tc = row & (C - 1), col & (C - 1)
    before = (tc > tr) if rev else (tc < tr)
    before_eq = (tc >= tr) if rev else (tc <= tr)
    r64 = lax.broadcasted_iota(jnp.int32, (C, C), 0)
    c64 = lax.broadcasted_iota(jnp.int32, (C, C), 1)
    cum_m = jnp.where((c64 >= r64) if rev else (c64 <= r64), 1.0, 0.0).astype(BF16)
    eye = jnp.where(row == col, 1.0, 0.0)
    lane = lax.broadcasted_iota(jnp.int32, (C, LANES), 1)
    head_a = lane < WKV_HEAD

    def stack2f(x):
        return jnp.concatenate([jnp.where(head_a, x, 0.0), jnp.where(head_a, 0.0, x)], axis=0)

    def unstack(x2):
        return x2[0:C] + x2[C:2 * C]

    blk16 = (row >> 4) == (col >> 4)
    blk32 = (row >> 5) == (col >> 5)

    def each(f, *lists):
        return [f(*xs) for xs in zip(*lists)]

    def square(m):
        return each(lambda x: _dot(_bf(x), _bf(x)), m)

    def times_one_plus(t, m):
        return each(lambda x, y: x + _dot(_bf(x), _bf(y)), t, m)

    def merge(t, a_kb, inside, outside):
        sel = jnp.logical_and(inside, jnp.logical_not(outside))
        tb = each(_bf, t)
        lt = each(lambda x, y: _bf(_dot(_bf(jnp.where(sel, x, 0.0)), y)), a_kb, tb)
        return each(lambda x, y, w: x - _dot(y, w), t, tb, lt)

    sls = [pl.ds(c * C, C) for c in range(nchunks)]
    r, k, v, lw, a = ([ref[sl, :] for sl in sls] for ref in (r_ref, k_ref, v_ref, lw_ref, a_ref))
    kk = each(lambda x: x * k_k, k)
    nrm = each(lambda x: _exact_right(x * x, ones_head, 2), kk)
    kk = each(lambda x, y: x * lax.rsqrt(jnp.maximum(y, 1e-24)), kk, nrm)
    kd = each(lambda x, y: x * (1.0 + (y - 1.0) * k_a), k, a)
    beta = each(lambda x, y: x * y, a, kk)
    bonus = each(lambda x, y: _exact_right(x * y * r_k, ones_head, 2), r, kd)
    for sl, x, y in zip(sls, bonus, v):
        z_ref[sl, :] = x * y
    b = each(lambda x: _exact_left(cum_m, x, 3), lw)
    b_tot = each(lambda x: x[0:1] if rev else x[C - 1:C], b)
    e_b = each(jnp.exp, b)
    e_nb = each(lambda x: jnp.exp(-x), b)
    e_rest = each(lambda x, y: jnp.exp(y - x), b, b_tot)
    r2f = each(lambda x, y: stack2f(x * y), r, e_b)
    r2 = each(_bf, r2f)
    kap2 = each(lambda x, y, w: _bf(stack2f(x * jnp.exp(y - w))), kk, b, lw)
    k2 = each(lambda x, y: _bf(stack2f(x * y)), kd, e_nb)
    be2 = each(lambda x, y: _bf(stack2f(x * y)), beta, e_nb)
    v2 = each(lambda x: _bf(stack2f(x)), v)
    kc2 = each(lambda x, y: _bf(stack2f(x * y)), kd, e_rest)
    bc2 = each(lambda x, y: _bf(stack2f(x * y)), beta, e_rest)
    kb2 = each(lambda x, y: jnp.concatenate([x, y], axis=0), k2, be2)
    g_kap = each(_dot_nt, kap2, kb2)
    a_kk = each(lambda x: _bf(jnp.where(before, x[:, :LANES], 0.0)), g_kap)
    a_kb = each(lambda x: jnp.where(before, x[:, LANES:], 0.0), g_kap)
    m1 = each(lambda x: jnp.where(blk16, -x, 0.0), a_kb)
    m2 = square(m1)
    g_r = each(_dot_nt, r2, kb2)
    m4 = square(m2)
    t = times_one_plus(each(lambda x: eye + x, m1), m2)
    a_rk = each(lambda x: _bf(jnp.where(before_eq, x[:, :LANES], 0.0)), g_r)
    a_rb = each(lambda x: _bf(jnp.where(before_eq, x[:, LANES:], 0.0)), g_r)
    m8 = square(m4)
    t = times_one_plus(t, m4)
    w0 = each(lambda x, y: _bf(_dot(x, y)), a_kk, v2)
    t = times_one_plus(t, m8)
    t = merge(t, a_kb, blk32, blk16)
    tinv = each(_bf, merge(t, a_kb, same_head, blk32))
    khu = each(lambda x, y, w: _bf(_dot(x, jnp.concatenate([y, w], axis=1))), tinv, kap2, w0)
    kh = each(lambda x: x[:, :LANES], khu)
    u0 = each(lambda x: x[:, LANES:], khu)
    vu = each(lambda x, y: jnp.concatenate([x, y], axis=0), v2, u0)
    rh = each(lambda x, y, w: _bf(x - _dot(y, w)), r2f, a_rb, kh)
    y0 = each(lambda x, y, w: _dot(jnp.concatenate([x, -y], axis=1), w), a_rk, a_rb, vu)
    jj = each(lambda x, y, w: _dot_tn(x, jnp.concatenate([y, -w], axis=0)), vu, kc2, bc2)
    gg = each(lambda x, y: _bf(-_dot_tn(x, y)), kh, bc2)
    for c in range(nchunks):
        rh_ref[c] = rh[c]
        y0_ref[c] = y0[c]
        j_ref[c] = jj[c]
        g_ref[c] = gg[c]
        et_ref[c] = jnp.broadcast_to(jnp.exp(b_tot[c]), (8, LANES))

    S = st_ref[...]
    for c in (range(nchunks - 1, -1, -1) if rev else range(nchunks)):
        Sb = _bf(S)
        y_ref[pl.ds(c * C, C), :] = unstack(_dot_nt(rh_ref[c], Sb) + y0_ref[c])
        S = S * et_ref[c, 0:1, :] + _dot(Sb, g_ref[c]) + j_ref[c]
    st_ref[...] = S

    @pl.when(pl.program_id(1) == pl.num_programs(1) - 1)
    def _():
        s_ref[...] = S


def wkv_scan(r, k, v, lw, a, par, s0, d):
    T, D = r.shape
    HP = D // LANES
    Tt = _tile(T, 1024)
    NT = T // Tt
    nck = Tt // CHUNK
    rev = d == 1

    def tt(t):
        return NT - 1 - t if rev else t

    tok = pl.BlockSpec((Tt, LANES), lambda h, t: (tt(t), h))
    tok_d = pl.BlockSpec((None, Tt, LANES), lambda h, t: (d, tt(t), h))
    st = pl.BlockSpec((None, LANES, LANES), lambda h, t: (h, 0, 0))
    return pl.pallas_call(
        functools.partial(_wkv_kernel, rev=rev),
        grid=(HP, NT),
        in_specs=[tok, tok, tok, tok_d, tok_d, pl.BlockSpec((8, LANES), lambda h, t: (0, h)), st],
        out_specs=[tok, tok, st],
        out_shape=[jax.ShapeDtypeStruct((T, D), F32), jax.ShapeDtypeStruct((T, D), F32),
                   jax.ShapeDtypeStruct((HP, LANES, LANES), F32)],
        scratch_shapes=[
            pltpu.VMEM((LANES, LANES), F32),
            pltpu.VMEM((nck, LANES, LANES), BF16),
            pltpu.VMEM((nck, LANES, LANES), F32),
            pltpu.VMEM((nck, LANES, LANES), BF16),
            pltpu.VMEM((nck, LANES, LANES), F32),
            pltpu.VMEM((nck, 8, LANES), F32),
        ],
        compiler_params=_cparams("parallel", "arbitrary"),
        name="wkv_scan",
    )(r, k, v, lw, a, par, s0)


def _rwkv_combine_kernel(y0_ref, y1_ref, z0_ref, z1_ref, g_ref, ln_ref, o_ref):
    D = o_ref.shape[1]
    row = lax.broadcasted_iota(jnp.int32, (LANES, LANES), 0)
    col = lax.broadcasted_iota(jnp.int32, (LANES, LANES), 1)
    mean_m = jnp.where((row < WKV_HEAD) == (col < WKV_HEAD), 1.0 / WKV_HEAD, 0.0).astype(BF16)
    for c in range(D // LANES):
        cs = slice(c * LANES, (c + 1) * LANES)
        y = y0_ref[:, cs] + y1_ref[:, cs]
        yc = y - _exact_right(y, mean_m, 2)
        var = _exact_right(yc * yc, mean_m, 2)
        out = yc * lax.rsqrt(var + GN_EPS) * ln_ref[0:1, cs] + ln_ref[1:2, cs] + (z0_ref[:, cs] + z1_ref[:, cs])
        o_ref[:, cs] = _bf(out * g_ref[:, cs])


def rwkv_combine(y0, y1, z0, z1, g, ln):
    M, D = g.shape
    tm = _tile(M, 256)
    tok = pl.BlockSpec((tm, D), lambda i: (i, 0))
    return pl.pallas_call(
        _rwkv_combine_kernel,
        grid=(M // tm,),
        in_specs=[tok] * 5 + [pl.BlockSpec((8, D), lambda i: (0, 0))],
        out_specs=tok,
        out_shape=jax.ShapeDtypeStruct((M, D), BF16),
        compiler_params=_cparams("parallel"),
        name="rwkv_combine",
    )(y0, y1, z0, z1, g, ln)


def _gla_kernel(q_ref, i_ref, f_ref, lbl_ref, s0_ref, o_ref, s_ref, st_ref, qe_ref, j_ref, et_ref, *, rev, layer):
    C = CHUNK
    Tt = q_ref.shape[0]
    nchunks = Tt // C
    nsub = C // SUB

    @pl.when(pl.program_id(1) == 0)
    def _():
        st_ref[...] = s0_ref[...]

    logits = lbl_ref[...]
    e = jnp.exp(logits - jnp.max(logits, axis=0, keepdims=True))
    p = e / jnp.sum(e, axis=0, keepdims=True)
    lb = jnp.zeros((1, LANES), F32)
    for l in range(1, layer + 1):
        lb = lb + p[l:l + 1]
    log_lb = jnp.log(lb)
    log_1m = jnp.log1p(-lb)

    r64 = lax.broadcasted_iota(jnp.int32, (C, C), 0)
    c64 = lax.broadcasted_iota(jnp.int32, (C, C), 1)
    cum_m = jnp.where((c64 >= r64) if rev else (c64 <= r64), 1.0, 0.0).astype(BF16)
    rows = lax.broadcasted_iota(jnp.int32, (C, 1), 0)
    lane_s = lax.broadcasted_iota(jnp.int32, (SUB, C), 1)
    row_s = lax.broadcasted_iota(jnp.int32, (SUB, 1), 0)

    def each(f, *lists):
        return [f(*xs) for xs in zip(*lists)]

    sls = [pl.ds(c * C, C) for c in range(nchunks)]
    q = [_silu(q_ref[sl, :]) for sl in sls]
    v = [i_ref[sl, :] for sl in sls]
    f = [f_ref[sl, :] for sl in sls]
    kg = each(lambda x: (1.0 - lb) * _sigmoid(-x), f)
    x2 = each(lambda x: log_1m - _softplus(-x), f)
    g = each(lambda x: jnp.maximum(log_lb, x) + jnp.log1p(jnp.exp(-jnp.abs(log_lb - x))), x2)
    b = each(lambda x: _exact_left(cum_m, x, 3), g)
    b_tot = each(lambda x: x[0:1] if rev else x[C - 1:C], b)
    vb = each(_bf, v)
    att_rows = [[] for _ in range(nchunks)]
    for I in range(nsub):
        lo = I * SUB
        qI = each(lambda x: x[lo:lo + SUB], q)
        bI = each(lambda x: x[lo:lo + SUB], b)
        first = (I == nsub - 1) if rev else (I == 0)
        if first:
            att = [jnp.zeros((SUB, C), F32)] * nchunks
        else:
            ref = each(lambda x: x[lo + SUB:lo + SUB + 1] if rev else x[lo - 1:lo], b)
            earlier = (rows >= lo + SUB) if rev else (rows < lo)
            kt = each(lambda x, y, w: _bf(jnp.where(earlier, x * jnp.exp(w - y), 0.0)), kg, b, ref)
            qt = each(lambda x, y, w: _bf(x * jnp.exp(y - w)), qI, bI, ref)
            att = each(_dot_nt, qt, kt)
        for j in range(SUB):
            s = lo + j
            place = jnp.logical_and(lane_s == s, (row_s <= j) if rev else (row_s >= j))
            pj = each(lambda x, y, w, u: x * y[s:s + 1] * jnp.exp(w - u[s:s + 1]), qI, kg, bI, b)
            att = each(lambda x, y: x + jnp.where(place, jnp.sum(y, axis=-1, keepdims=True), 0.0), att, pj)
        for c in range(nchunks):
            att_rows[c].append(att[c])
    o_in = each(lambda x, y: _dot(_bf(jnp.concatenate(x, axis=0)), y), att_rows, vb)
    for c in range(nchunks):
        o_ref[sls[c], :] = o_in[c]
        qe_ref[c] = _bf(q[c] * jnp.exp(b[c]))
        j_ref[c] = _dot_tn(vb[c], _bf(kg[c] * jnp.exp(b_tot[c] - b[c])))
        et_ref[c] = jnp.broadcast_to(jnp.exp(b_tot[c]), (8, LANES))

    Z = st_ref[...]
    for c in (range(nchunks - 1, -1, -1) if rev else range(nchunks)):
        o_ref[sls[c], :] += _dot_nt(qe_ref[c], _bf(Z))
        Z = Z * et_ref[c, 0:1, :] + j_ref[c]
    st_ref[...] = Z

    @pl.when(pl.program_id(1) == pl.num_programs(1) - 1)
    def _():
        s_ref[...] = Z


def gla_scan(raw, lb_logits, s0, d, layer):
    T = raw.shape[0]
    D = raw.shape[1] // 5
    H = D // LANES
    Tt = _tile(T, 1024)
    NT = T // Tt
    nck = Tt // CHUNK
    rev = d == 1
    L = lb_logits.shape[0]

    def tt(t):
        return NT - 1 - t if rev else t

    st = pl.BlockSpec((None, LANES, LANES), lambda h, t: (h, 0, 0))
    return pl.pallas_call(
        functools.partial(_gla_kernel, rev=rev, layer=layer),
        grid=(H, NT),
        in_specs=[
            pl.BlockSpec((Tt, LANES), lambda h, t: (tt(t), h)),
            pl.BlockSpec((Tt, LANES), lambda h, t: (tt(t), H + h)),
            pl.BlockSpec((Tt, LANES), lambda h, t: (tt(t), (2 + d) * H + h)),
            pl.BlockSpec((L, LANES), lambda h, t: (0, h)),
            st,
        ],
        out_specs=[pl.BlockSpec((Tt, LANES), lambda h, t: (tt(t), h)), st],
        out_shape=[jax.ShapeDtypeStruct((T, D), F32), jax.ShapeDtypeStruct((H, LANES, LANES), F32)],
        scratch_shapes=[
            pltpu.VMEM((LANES, LANES), F32),
            pltpu.VMEM((nck, CHUNK, LANES), BF16),
            pltpu.VMEM((nck, LANES, LANES), F32),
            pltpu.VMEM((nck, 8, LANES), F32),
        ],
        compiler_params=_cparams("parallel", "arbitrary"),
        name="gla_scan",
    )(raw, raw, raw, lb_logits, s0)


def _hgrn_combine_kernel(o0_ref, o1_ref, g_ref, gn_ref, o_ref):
    D = o_ref.shape[1]
    for c in range(D // LANES):
        cs = slice(c * LANES, (c + 1) * LANES)
        o = o0_ref[:, cs] + o1_ref[:, cs]
        o = o * lax.rsqrt(jnp.mean(o * o, axis=-1, keepdims=True) + EPS) * gn_ref[0:1, cs]
        o_ref[:, cs] = _bf(o * _silu(g_ref[:, cs]))


def hgrn_combine(o0, o1, raw, gn):
    M, D = o0.shape
    tm = _tile(M, 256)
    tok = pl.BlockSpec((tm, D), lambda i: (i, 0))
    return pl.pallas_call(
        _hgrn_combine_kernel,
        grid=(M // tm,),
        in_specs=[tok, tok, pl.BlockSpec((tm, D), lambda i: (i, 4)), pl.BlockSpec((1, D), lambda i: (0, 0))],
        out_specs=tok,
        out_shape=jax.ShapeDtypeStruct((M, D), BF16),
        compiler_params=_cparams("parallel"),
        name="hgrn_combine",
    )(o0, o1, raw, gn)


def _rows8(*vecs):
    D = vecs[0].shape[-1]
    rows = [v.reshape(1, D) for v in vecs]
    rows.append(jnp.zeros((8 - len(rows), D), F32))
    return jnp.concatenate(rows, axis=0)


def kernel(x, c, ctx, c_ctx, ada_down, ada_up, ada_b, norm_g, ffn_w13, ffn_w2, final_g, lru_w_in, lru_conv_w, lru_conv_b, lru_gate_a_w, lru_gate_a_b, lru_gate_x_w, lru_gate_x_b, lru_lam, lru_w_out, rwkv_mu, rwkv_w_r, rwkv_w_k, rwkv_w_v, rwkv_w_o, rwkv_w0, rwkv_w1, rwkv_w2, rwkv_a0, rwkv_a1, rwkv_a2, rwkv_g1, rwkv_g2, rwkv_k_k, rwkv_k_a, rwkv_r_k, rwkv_ln_w, rwkv_ln_b, hgrn_w_in, hgrn_lb_logits, hgrn_gn_g, hgrn_w_out):
    B, T, D = x.shape
    assert B == 1, "one sequence per call"
    depth = ada_down.shape[0]
    xl, xc = x[0], ctx[0]

    mods = ada_all_layers(_rows8(c[0], c_ctx), ada_down, ada_up, ada_b)
    mods = mods[:, :2].reshape(depth, 2, N_MOD, D)

    lru_wa, lru_wx = _bf(lru_gate_a_w), _bf(lru_gate_x_w)
    lw1, lw2, la1, la2 = _bf(rwkv_w1), _bf(rwkv_w2), _bf(rwkv_a1), _bf(rwkv_a2)
    lg1, lg2 = _bf(rwkv_g1), _bf(rwkv_g2)

    for i in range(depth):
        need_ctx = i < depth - 1
        kind, j = i % 3, i // 3
        ml, mc = mods[i, 0], mods[i, 1]

        def ffn_mod(m, k, which):
            return _rows8(norm_g[i, 2 * which], m[k], m[k + 1], m[k + 2])

        xc, *wb = half_ffn(xc, ffn_mod(mc, 0, 0), (ffn_w13, ffn_w2), (i, 0))
        xl = half_ffn(xl, ffn_mod(ml, 0, 0), wb)
        mod_l = _rows8(norm_g[i, 1], ml[3], ml[4])
        mod_c = _rows8(norm_g[i, 1], mc[3], mc[4])
        gate_l, gate_c = _rows8(ml[5]), _rows8(mc[5])

        if kind == 0:
            p_c, w_in = proj(xc, lru_w_in, (j,), mod=mod_c, gelu_cols=D, emit_weight=True)
            p_l = proj(xl, w_in, mod=mod_l, gelu_cols=D)
            hs = lru_scan(p_l, p_c, lru_conv_w, lru_conv_b, lru_wa, lru_wx, lru_gate_a_b, lru_gate_x_b, lru_lam,
                          j, need_ctx)
            a_l = lru_combine(p_l, hs[0])
            a_c = lru_combine(p_c, hs[1]) if need_ctx else None
            w_last = lru_w_out
        elif kind == 1:
            par = _rows8(rwkv_k_k[j], rwkv_k_a[j], rwkv_r_k[j].reshape(D))
            ln = _rows8(rwkv_ln_w[j], rwkv_ln_b[j])
            outs = []
            state = [jnp.zeros((D // LANES, LANES, LANES), F32)] * 2
            for xs, mod, is_grid in ((xc, mod_c, False), (xl, mod_l, True)):
                xr, xw, xk, xv, xa, xg = rwkv_shiftmix(xs, mod, rwkv_mu, j, is_grid)
                if not is_grid:
                    r, w_r = proj(xr, rwkv_w_r, (j,), emit_weight=True)
                    k, w_k = proj(xk, rwkv_w_k, (j,), emit_weight=True)
                    v, w_v = proj(xv, rwkv_w_v, (j,), emit_weight=True)
                else:
                    r, k, v = proj(xr, w_r), proj(xk, w_k), proj(xv, w_v)
                lw, a, g = rwkv_lora(xw, xa, xg, lw1, lw2, rwkv_w0, la1, la2, rwkv_a0, lg1, lg2, j)
                ys, zs = [], []
                for d in (0, 1):
                    y, z, state[d] = wkv_scan(r, k, v, lw, a, par, state[d], d)
                    ys.append(y)
                    zs.append(z)
                outs.append(rwkv_combine(ys[0], ys[1], zs[0], zs[1], g, ln))
            a_c, a_l = outs
            w_last = rwkv_w_o
        else:
            outs = []
            state = [jnp.zeros((D // LANES, LANES, LANES), F32)] * 2
            for xs, mod, is_ctx in ((xc, mod_c, True), (xl, mod_l, False)):
                if is_ctx:
                    raw, w_in = proj(xs, hgrn_w_in, (j,), mod=mod, emit_weight=True)
                else:
                    raw = proj(xs, w_in, mod=mod)
                os_ = []
                for d in (0, 1):
                    o, state[d] = gla_scan(raw, hgrn_lb_logits, state[d], d, i)
                    os_.append(o)
                outs.append(hgrn_combine(os_[0], os_[1], raw, hgrn_gn_g[j].reshape(1, D)))
            a_c, a_l = outs
            w_last = hgrn_w_out

        if need_ctx:
            xc, w_out = proj(a_c, w_last, (j,), res=xc, gate=gate_c, emit_weight=True)
            xc, *wb = half_ffn(xc, ffn_mod(mc, 6, 1), (ffn_w13, ffn_w2), (i, 1))
        else:
            w_out = _bf(w_last[j])
            wb = round_ffn_weights(ffn_w13, ffn_w2, (i, 1))
        xl = proj(a_l, w_out, res=xl, gate=gate_l)
        xl = half_ffn(xl, ffn_mod(ml, 6, 1), wb)

    return final_rmsnorm(xl, final_g)[None]
```

```python
import functools
import math

import jax
import jax.numpy as jnp
from jax import lax
from jax.experimental import pallas as pl
from jax.experimental.pallas import tpu as pltpu

F32 = jnp.float32
BF16 = jnp.bfloat16

EPS = 1e-6
GN_EPS = 64e-5
LRU_C = 8.0
GRID_W = 64
N_MOD = 9
WKV_HEAD = 64
GLA_HEAD = 128
CHUNK = 64
SUB = 16
LANES = 128
VMEM_LIMIT_BYTES = 60 * 1024 * 1024


def _cparams(*sem):
    return pltpu.CompilerParams(dimension_semantics=sem, vmem_limit_bytes=VMEM_LIMIT_BYTES)


def _tile(n, pref):
    if n <= pref:
        return n
    for t in range(pref, 7, -1):
        if n % t == 0 and t % 8 == 0:
            return t
    return n


def _bf(x):
    return x.astype(BF16)


def _dot(a, b):
    return jnp.dot(a, b, preferred_element_type=F32)


def _dot_nt(a, b):
    return lax.dot_general(a, b, (((1,), (1,)), ((), ())), preferred_element_type=F32)


def _dot_tn(a, b):
    return lax.dot_general(a, b, (((0,), (0,)), ((), ())), preferred_element_type=F32)


def _split_terms(x, terms):
    out, rem = [], x
    for _ in range(terms):
        p = _bf(rem)
        out.append(p)
        rem = rem - p.astype(F32)
    return out


def _exact_left(m_bf, x, terms):
    acc = None
    for p in _split_terms(x, terms):
        d = _dot(m_bf, p)
        acc = d if acc is None else acc + d
    return acc


def _exact_right(x, m_bf, terms):
    acc = None
    for p in _split_terms(x, terms):
        d = _dot(p, m_bf)
        acc = d if acc is None else acc + d
    return acc


def _sigmoid(x):
    return 0.5 * jnp.tanh(0.5 * x) + 0.5


def _silu(x):
    return x * _sigmoid(x)


def _gelu_tanh(x):
    return 0.5 * x * (1.0 + jnp.tanh(math.sqrt(2.0 / math.pi) * (x + 0.044715 * (x * x * x))))


def _softplus(x):
    return jnp.maximum(x, 0.0) + jnp.log1p(jnp.exp(-jnp.abs(x)))


def _row_rsqrt_ms(x):
    D = x.shape[-1]
    if D % LANES == 0:
        acc = x[:, 0:LANES] * x[:, 0:LANES]
        for c in range(LANES, D, LANES):
            acc = acc + x[:, c:c + LANES] * x[:, c:c + LANES]
        ms = jnp.sum(acc, axis=-1, keepdims=True) * (1.0 / D)
    else:
        ms = jnp.mean(x * x, axis=-1, keepdims=True)
    return lax.rsqrt(ms + EPS)


def _normmod(x, g, shift, scale):
    return (x * _row_rsqrt_ms(x)) * (g * (1.0 + scale)) + shift


def _ada_kernel(cc_ref, down_ref, up_ref, b_ref, o_ref, t_ref):
    hi = lax.Precision.HIGHEST

    @pl.when(pl.program_id(1) == 0)
    def _():
        t_ref[...] = jnp.dot(_silu(cc_ref[...]), down_ref[...], precision=hi, preferred_element_type=F32)

    o_ref[...] = jnp.dot(t_ref[...], up_ref[...], precision=hi, preferred_element_type=F32) + b_ref[...]


def ada_all_layers(cc, down, up, bias):
    L, D, R = down.shape
    N = up.shape[2]
    tn = _tile(N, 4096)
    return pl.pallas_call(
        _ada_kernel,
        grid=(L, N // tn),
        in_specs=[
            pl.BlockSpec((8, D), lambda l, j: (0, 0)),
            pl.BlockSpec((None, D, R), lambda l, j: (l, 0, 0)),
            pl.BlockSpec((None, R, tn), lambda l, j: (l, 0, j)),
            pl.BlockSpec((None, 1, tn), lambda l, j: (l, 0, j)),
        ],
        out_specs=pl.BlockSpec((None, 8, tn), lambda l, j: (l, 0, j)),
        out_shape=jax.ShapeDtypeStruct((L, 8, N), F32),
        scratch_shapes=[pltpu.VMEM((8, R), F32)],
        compiler_params=_cparams("parallel", "arbitrary"),
        name="ada",
    )(cc, down, up, bias.reshape(L, 1, N))


def _ffn_kernel(x_ref, mod_ref, w1_ref, w3_ref, w2_ref, o_ref, *rest, emit_weights):
    h_ref, rs_ref = rest[-2:]
    f = pl.program_id(1)
    tm, D = o_ref.shape
    strip = math.gcd(tm, 16)

    def for_strips(body):
        def step(s, carry):
            body(pl.ds(pl.multiple_of(s * strip, strip), strip))
            return carry
        lax.fori_loop(0, tm // strip, step, 0, unroll=math.gcd(tm // strip, 4))

    @pl.when(f == 0)
    def _():
        def scale_rows(rows):
            rs_ref[rows, :] = jnp.broadcast_to(_row_rsqrt_ms(x_ref[rows, :]), (strip, LANES))
        for_strips(scale_rows)
        gain = mod_ref[0:1] * (1.0 + mod_ref[2:3])

        def normalise(rows):
            h_ref[rows, :] = _bf(x_ref[rows, :] * rs_ref[rows, 0:1] * gain + mod_ref[1:2])
            o_ref[rows, :] = jnp.zeros((strip, D), F32)
        for_strips(normalise)

    w1, w3 = _bf(w1_ref[...]), _bf(w3_ref[...])
    if emit_weights:
        rest[0][...] = w1
        rest[1][...] = w3
    h = h_ref[...]
    act = _bf(_silu(_dot(h, w1)) * _dot(h, w3))
    slab = min(D, 512)
    for c0 in range(0, D, slab):
        w2 = _bf(w2_ref[:, c0:c0 + slab])
        if emit_weights:
            rest[2][:, c0:c0 + slab] = w2
        o_ref[:, c0:c0 + slab] += _dot(act, w2)

    @pl.when(f == pl.num_programs(1) - 1)
    def _():
        def body(rows):
            o_ref[rows, :] = x_ref[rows, :] + 0.5 * mod_ref[3:4] * o_ref[rows, :]
        for_strips(body)


def half_ffn(x, mod, weights, index=None):
    M, D = x.shape
    emit = index is not None
    F = weights[-1].shape[-2]
    tm = _tile(M, 512)
    tf = _tile(F, 256 if emit else 512)
    nf = F // tf
    if emit:
        assert M == tm
        w13, w2 = weights
        lead = tuple(index)
        w_specs = [
            pl.BlockSpec((None, None, D, tf), lambda i, f: lead + (0, f)),
            pl.BlockSpec((None, None, D, tf), lambda i, f: lead + (0, nf + f)),
            pl.BlockSpec((None, None, tf, D), lambda i, f: lead + (f, 0)),
        ]
        w_args = (w13, w13, w2)
    else:
        w_specs = [
            pl.BlockSpec((D, tf), lambda i, f: (0, f)),
            pl.BlockSpec((D, tf), lambda i, f: (0, f)),
            pl.BlockSpec((tf, D), lambda i, f: (f, 0)),
        ]
        w_args = tuple(weights)
    out_specs = [pl.BlockSpec((tm, D), lambda i, f: (i, 0))]
    out_shape = [jax.ShapeDtypeStruct((M, D), F32)]
    if emit:
        out_specs += [
            pl.BlockSpec((D, tf), lambda i, f: (0, f)),
            pl.BlockSpec((D, tf), lambda i, f: (0, f)),
            pl.BlockSpec((tf, D), lambda i, f: (f, 0)),
        ]
        out_shape += [jax.ShapeDtypeStruct((D, F), BF16), jax.ShapeDtypeStruct((D, F), BF16),
                      jax.ShapeDtypeStruct((F, D), BF16)]
    out = pl.pallas_call(
        functools.partial(_ffn_kernel, emit_weights=emit),
        grid=(M // tm, nf),
        in_specs=[
            pl.BlockSpec((tm, D), lambda i, f: (i, 0), pipeline_mode=pl.Buffered(1)),
            pl.BlockSpec((8, D), lambda i, f: (0, 0)),
        ] + w_specs,
        out_specs=out_specs,
        out_shape=out_shape,
        scratch_shapes=[pltpu.VMEM((tm, D), BF16), pltpu.VMEM((tm, LANES), F32)],
        compiler_params=_cparams("parallel", "arbitrary"),
        name="half_ffn",
    )(x, mod, *w_args)
    return out if emit else out[0]


def _round_weights_kernel(w1_ref, w3_ref, w2_ref, w1b_ref, w3b_ref, w2b_ref):
    w1b_ref[...] = _bf(w1_ref[...])
    w3b_ref[...] = _bf(w3_ref[...])
    w2b_ref[...] = _bf(w2_ref[...])


def round_ffn_weights(w13, w2, index):
    D, F = w2.shape[-1], w2.shape[-2]
    tf = _tile(F, 256)
    nf = F // tf
    lead = tuple(index)
    return pl.pallas_call(
        _round_weights_kernel,
        grid=(nf,),
        in_specs=[
            pl.BlockSpec((None, None, D, tf), lambda f: lead + (0, f)),
            pl.BlockSpec((None, None, D, tf), lambda f: lead + (0, nf + f)),
            pl.BlockSpec((None, None, tf, D), lambda f: lead + (f, 0)),
        ],
        out_specs=[
            pl.BlockSpec((D, tf), lambda f: (0, f)),
            pl.BlockSpec((D, tf), lambda f: (0, f)),
            pl.BlockSpec((tf, D), lambda f: (f, 0)),
        ],
        out_shape=[jax.ShapeDtypeStruct((D, F), BF16), jax.ShapeDtypeStruct((D, F), BF16),
                   jax.ShapeDtypeStruct((F, D), BF16)],
        compiler_params=_cparams("parallel"),
        name="round_ffn_weights",
    )(w13, w13, w2)


def _proj_kernel(*refs, norm, n_gelu, residual, emit_weight):
    it = iter(refs)
    a_ref = next(it)
    mod_ref = next(it) if norm else None
    w_ref = next(it)
    res_ref = next(it) if residual else None
    gate_ref = next(it) if residual else None
    o_ref = next(it)
    wb_ref = next(it) if emit_weight else None
    h_ref = next(it) if norm else None
    j = pl.program_id(1)

    if norm:
        @pl.when(j == 0)
        def _():
            h_ref[...] = _bf(_normmod(a_ref[...], mod_ref[0:1], mod_ref[1:2], mod_ref[2:3]))
        lhs = h_ref[...]
    else:
        lhs = a_ref[...]
    w = _bf(w_ref[...])
    if emit_weight:
        wb_ref[...] = w
    acc = _dot(lhs, w)
    if residual:
        o_ref[...] = res_ref[...] + gate_ref[0:1] * acc
    elif n_gelu:
        @pl.when(j < n_gelu)
        def _():
            o_ref[...] = _gelu_tanh(acc)

        @pl.when(j >= n_gelu)
        def _():
            o_ref[...] = acc
    else:
        o_ref[...] = acc


def proj(a, w, w_index=(), *, mod=None, gelu_cols=0, res=None, gate=None, emit_weight=False):
    M, K = a.shape
    N = w.shape[-1]
    tm = _tile(M, 512)
    tn = _tile(math.gcd(N, gelu_cols) if gelu_cols else N, 512 if emit_weight else 1024)
    norm = mod is not None
    residual = res is not None
    assert gelu_cols % tn == 0 and (M == tm or not emit_weight)
    lead = tuple(w_index)
    in_specs = [pl.BlockSpec((tm, K), lambda i, j: (i, 0))]
    args = [a]
    if norm:
        in_specs.append(pl.BlockSpec((8, K), lambda i, j: (0, 0)))
        args.append(mod)
    in_specs.append(pl.BlockSpec((None,) * len(lead) + (K, tn), lambda i, j: lead + (0, j)))
    args.append(w)
    if residual:
        in_specs.append(pl.BlockSpec((tm, tn), lambda i, j: (i, j)))
        in_specs.append(pl.BlockSpec((8, tn), lambda i, j: (0, j)))
        args += [res, gate]
    out_specs = [pl.BlockSpec((tm, tn), lambda i, j: (i, j))]
    out_shape = [jax.ShapeDtypeStruct((M, N), F32)]
    if emit_weight:
        out_specs.append(pl.BlockSpec((K, tn), lambda i, j: (0, j)))
        out_shape.append(jax.ShapeDtypeStruct((K, N), BF16))
    out = pl.pallas_call(
        functools.partial(_proj_kernel, norm=norm, n_gelu=gelu_cols // tn, residual=residual,
                          emit_weight=emit_weight),
        grid=(M // tm, N // tn),
        in_specs=in_specs,
        out_specs=out_specs,
        out_shape=out_shape,
        scratch_shapes=[pltpu.VMEM((tm, K), BF16)] if norm else [],
        compiler_params=_cparams("parallel", "arbitrary"),
        name="proj",
    )(*args)
    return out if emit_weight else out[0]


def _rmsnorm_kernel(x_ref, g_ref, o_ref):
    x = x_ref[...]
    ms = jnp.mean(x * x, axis=-1, keepdims=True)
    o_ref[...] = x * lax.rsqrt(ms + EPS) * g_ref[...]


def final_rmsnorm(x, g):
    M, D = x.shape
    tm = _tile(M, 512)
    return pl.pallas_call(
        _rmsnorm_kernel,
        grid=(M // tm,),
        in_specs=[pl.BlockSpec((tm, D), lambda i: (i, 0)), pl.BlockSpec((1, D), lambda i: (0, 0))],
        out_specs=pl.BlockSpec((tm, D), lambda i: (i, 0)),
        out_shape=jax.ShapeDtypeStruct((M, D), F32),
        compiler_params=_cparams("parallel"),
        name="final_rmsnorm",
    )(x, g.reshape(1, D))


def _lru_kernel(ul_ref, uc_ref, cw_ref, cb_ref, wa_ref, wx_ref, ba_ref, bx_ref, lam_ref, *refs, R, need_ctx):
    hl_ref = refs[0]
    hc_ref = refs[1] if need_ctx else None
    v_ref, hs_ref = refs[-2:]
    T, W = ul_ref.shape
    NL = W // LANES
    Tc = uc_ref.shape[0]
    cw = cw_ref[...]
    cb = cb_ref[...]

    def conv_chunk(u_ref, t0, n_rows):
        main = u_ref[pl.ds(t0, R), :]
        p0 = pl.multiple_of(jnp.maximum(t0 - 8, 0), 8)
        n0 = pl.multiple_of(jnp.minimum(t0 + R, n_rows - 8), 8)
        prev = jnp.where(t0 > 0, u_ref[pl.ds(p0, 8), :], 0.0)
        nxt = jnp.where(t0 + R < n_rows, u_ref[pl.ds(n0, 8), :], 0.0)
        ext = jnp.concatenate([prev, main, nxt], axis=0)
        n = R + 16
        out = cb + ext[8:8 + R] * cw[2:3]
        out = out + pltpu.roll(ext, 2, 0)[8:8 + R] * cw[0:1]
        out = out + pltpu.roll(ext, 1, 0)[8:8 + R] * cw[1:2]
        out = out + pltpu.roll(ext, n - 1, 0)[8:8 + R] * cw[3:4]
        return out

    G = R // 8

    def scan_chunk(d, carry, rev):
        vp = jnp.concatenate(
            [jnp.concatenate([v_ref[j, pl.ds(g, 8, stride=G), :] for j in range(NL)], axis=1) for g in range(G)],
            axis=0)
        vb = _bf(vp)
        r = _sigmoid(_dot(vb, wa_ref[d]) + ba_ref[d:d + 1])
        gi = _sigmoid(_dot(vb, wx_ref[d]) + bx_ref[d:d + 1])
        log_a = -LRU_C * r * _softplus(-lam_ref[d:d + 1])
        a = jnp.exp(log_a)
        b = jnp.sqrt(-jnp.tanh(log_a) * (a * a + 1.0)) * (gi * vp)
        h = jnp.zeros((8, W), F32)
        p = jnp.ones((8, W), F32)
        hs, ps = [None] * G, [None] * G
        for g in (range(G - 1, -1, -1) if rev else range(G)):
            ag = a[8 * g:8 * g + 8]
            h = ag * h + b[8 * g:8 * g + 8]
            p = ag * p
            hs[g], ps[g] = h, p
        starts = [None] * 8
        for s in (range(7, -1, -1) if rev else range(8)):
            starts[s] = carry
            carry = h[s:s + 1] + p[s:s + 1] * carry
        start = jnp.concatenate(starts, axis=0)
        return [hs[g] + ps[g] * start for g in range(G)], carry

    def run(u_ref, o_ref, n_rows, d, carry):
        rev = d == 1
        nchunks = n_rows // R

        def body(i, carry):
            c = nchunks - 1 - i if rev else i
            t0 = pl.multiple_of(c * R, R)
            v = conv_chunk(u_ref, t0, n_rows)
            for j in range(NL):
                v_ref[j] = v[:, j * LANES:(j + 1) * LANES]
            hs, carry = scan_chunk(d, carry, rev)
            if o_ref is not None:
                for g in range(G):
                    for j in range(NL):
                        hs_ref[j, pl.ds(g, 8, stride=G), :] = hs[g][:, j * LANES:(j + 1) * LANES]
                h = jnp.concatenate([hs_ref[j] for j in range(NL)], axis=1)
                if d == 0:
                    o_ref[pl.ds(t0, R), :] = h
                else:
                    o_ref[pl.ds(t0, R), :] += h
            return carry

        return lax.fori_loop(0, nchunks, body, carry)

    for d in (0, 1):
        carry = run(uc_ref, hc_ref, Tc, d, jnp.zeros((1, W), F32))
        run(ul_ref, hl_ref, T, d, carry)


def lru_scan(pl_out, pc_out, conv_w, conv_b, wa, wx, ba, bx, lam, li, need_ctx):
    T = pl_out.shape[0]
    Tc = pc_out.shape[0]
    D = pl_out.shape[1] // 2
    NH, W = wa.shape[2], wa.shape[3]
    assert W % LANES == 0 and NH * W == D
    R = _tile(math.gcd(T, Tc), 256)
    col = D // W
    vec2 = pl.BlockSpec((None, 2, W), lambda h: (li, 0, h))
    out_shape = [jax.ShapeDtypeStruct((T, D), F32)]
    out_specs = [pl.BlockSpec((T, W), lambda h: (0, h))]
    if need_ctx:
        out_shape.append(jax.ShapeDtypeStruct((Tc, D), F32))
        out_specs.append(pl.BlockSpec((Tc, W), lambda h: (0, h)))
    return pl.pallas_call(
        functools.partial(_lru_kernel, R=R, need_ctx=need_ctx),
        grid=(NH,),
        in_specs=[
            pl.BlockSpec((T, W), lambda h: (0, col + h)),
            pl.BlockSpec((Tc, W), lambda h: (0, col + h)),
            pl.BlockSpec((None, 4, W), lambda h: (li, 0, h)),
            pl.BlockSpec((None, 1, W), lambda h: (li, 0, h)),
            pl.BlockSpec((None, 2, None, W, W), lambda h: (li, 0, h, 0, 0)),
            pl.BlockSpec((None, 2, None, W, W), lambda h: (li, 0, h, 0, 0)),
            vec2, vec2, vec2,
        ],
        out_specs=out_specs,
        out_shape=out_shape,
        scratch_shapes=[pltpu.VMEM((W // LANES, R, LANES), F32)] * 2,
        compiler_params=_cparams("parallel"),
        name="lru_scan",
    )(pl_out, pc_out, conv_w, conv_b.reshape(conv_b.shape[0], 1, D), wa, wx, ba, bx, lam)


def _mul_kernel(a_ref, b_ref, o_ref):
    o_ref[...] = _bf(a_ref[...] * b_ref[...])


def lru_combine(p_out, hs):
    M, D = hs.shape
    tm = _tile(M, 256)
    return pl.pallas_call(
        _mul_kernel,
        grid=(M // tm,),
        in_specs=[pl.BlockSpec((tm, D), lambda i: (i, 0)), pl.BlockSpec((tm, D), lambda i: (i, 0))],
        out_specs=pl.BlockSpec((tm, D), lambda i: (i, 0)),
        out_shape=jax.ShapeDtypeStruct((M, D), BF16),
        compiler_params=_cparams("parallel"),
        name="lru_combine",
    )(p_out, hs)


def _mix_write(h, shifted, mu_ref, o_refs, cols):
    xx = shifted - h
    for n, o_ref in enumerate(o_refs):
        o_ref[:, cols] = _bf(h + xx * mu_ref[n:n + 1, cols])


def _mix_lat_kernel(xm_ref, xp_ref, xn_ref, mod_ref, mu_ref, *o_refs, T):
    tm, D = xm_ref.shape
    q = D // 4
    i = pl.program_id(0)
    g, sh, sc = mod_ref[0:1], mod_ref[1:2], mod_ref[2:3]
    hm = _normmod(xm_ref[...], g, sh, sc)
    hp = _normmod(xp_ref[...], g, sh, sc)
    hn = _normmod(xn_ref[...], g, sh, sc)
    row = lax.broadcasted_iota(jnp.int32, (tm, 1), 0)
    t = i * tm + row
    colpos = row & (GRID_W - 1)
    s0 = slice(0, q)
    left = jnp.where(colpos > 0, pltpu.roll(hm[:, s0], 1, 0), 0.0)
    _mix_write(hm[:, s0], left, mu_ref, o_refs, s0)
    s1 = slice(q, 2 * q)
    right = jnp.where(colpos < GRID_W - 1, pltpu.roll(hm[:, s1], tm - 1, 0), 0.0)
    _mix_write(hm[:, s1], right, mu_ref, o_refs, s1)
    s2 = slice(2 * q, 3 * q)
    up = jnp.concatenate([hp[:, s2], hm[:tm - GRID_W, s2]], axis=0) if tm > GRID_W else hp[:, s2]
    up = jnp.where(t >= GRID_W, up, 0.0)
    _mix_write(hm[:, s2], up, mu_ref, o_refs, s2)
    s3 = slice(3 * q, D)
    down = jnp.concatenate([hm[GRID_W:, s3], hn[:, s3]], axis=0) if tm > GRID_W else hn[:, s3]
    down = jnp.where(t < T - GRID_W, down, 0.0)
    _mix_write(hm[:, s3], down, mu_ref, o_refs, s3)


def _mix_ctx_kernel(x_ref, mod_ref, mu_ref, *o_refs):
    Tc, D = x_ref.shape
    hh = D // 2
    h = _normmod(x_ref[...], mod_ref[0:1], mod_ref[1:2], mod_ref[2:3])
    row = lax.broadcasted_iota(jnp.int32, (Tc, 1), 0)
    s0 = slice(0, hh)
    prev = jnp.where(row > 0, pltpu.roll(h[:, s0], 1, 0), 0.0)
    _mix_write(h[:, s0], prev, mu_ref, o_refs, s0)
    s1 = slice(hh, D)
    nxt = jnp.where(row < Tc - 1, pltpu.roll(h[:, s1], Tc - 1, 0), 0.0)
    _mix_write(h[:, s1], nxt, mu_ref, o_refs, s1)


def rwkv_shiftmix(x, mod, mu, li, grid_tokens):
    M, D = x.shape
    n_out = mu.shape[1]
    mu_spec_args = ((None, n_out, D),)
    out_shape = [jax.ShapeDtypeStruct((M, D), BF16)] * n_out
    if grid_tokens:
        tm = _tile(M, 256)
        assert tm % GRID_W == 0 and M % GRID_W == 0
        r = tm // GRID_W
        nb = M // GRID_W
        return pl.pallas_call(
            functools.partial(_mix_lat_kernel, T=M),
            grid=(M // tm,),
            in_specs=[
                pl.BlockSpec((tm, D), lambda i: (i, 0)),
                pl.BlockSpec((GRID_W, D), lambda i: (jnp.maximum(i * r - 1, 0), 0)),
                pl.BlockSpec((GRID_W, D), lambda i: (jnp.minimum((i + 1) * r, nb - 1), 0)),
                pl.BlockSpec((8, D), lambda i: (0, 0)),
                pl.BlockSpec(*mu_spec_args, lambda i: (li, 0, 0)),
            ],
            out_specs=[pl.BlockSpec((tm, D), lambda i: (i, 0))] * n_out,
            out_shape=out_shape,
            compiler_params=_cparams("parallel"),
            name="rwkv_shiftmix_grid",
        )(x, x, x, mod, mu)
    return pl.pallas_call(
        _mix_ctx_kernel,
        grid=(1,),
        in_specs=[
            pl.BlockSpec((M, D), lambda i: (0, 0)),
            pl.BlockSpec((8, D), lambda i: (0, 0)),
            pl.BlockSpec(*mu_spec_args, lambda i: (li, 0, 0)),
        ],
        out_specs=[pl.BlockSpec((M, D), lambda i: (0, 0))] * n_out,
        out_shape=out_shape,
        compiler_params=_cparams("arbitrary"),
        name="rwkv_shiftmix_seq",
    )(x, mod, mu)


def _lora_kernel(xw_ref, xa_ref, xg_ref, w1_ref, w2_ref, w0_ref, a1_ref, a2_ref, a0_ref, g1_ref, g2_ref,
                 lw_ref, a_ref, g_ref, tw_ref, ta_ref, tg_ref):
    @pl.when(pl.program_id(1) == 0)
    def _():
        for d in (0, 1):
            tw_ref[d] = _bf(jnp.tanh(_dot(xw_ref[...], w1_ref[d])))
            ta_ref[d] = _bf(_dot(xa_ref[...], a1_ref[d]))
        tg_ref[...] = _bf(_sigmoid(_dot(xg_ref[...], g1_ref[...])))

    for d in (0, 1):
        z = w0_ref[d:d + 1] + _dot(tw_ref[d], w2_ref[d])
        lw_ref[d] = -math.exp(-0.5) * _sigmoid(z)
        a_ref[d] = _sigmoid(a0_ref[d:d + 1] + _dot(ta_ref[d], a2_ref[d]))
    g_ref[...] = _dot(tg_ref[...], g2_ref[...])


def rwkv_lora(xw, xa, xg, w1, w2, w0, a1, a2, a0, g1, g2, li):
    M, D = xw.shape
    RW, RA, RG = w1.shape[-1], a1.shape[-1], g1.shape[-1]
    tm = _tile(M, 256)
    tn = _tile(D, 1024)
    row = pl.BlockSpec((tm, D), lambda i, j: (i, 0))
    out2 = pl.BlockSpec((2, tm, tn), lambda i, j: (0, i, j))
    return pl.pallas_call(
        _lora_kernel,
        grid=(M // tm, D // tn),
        in_specs=[
            row, row, row,
            pl.BlockSpec((None, 2, D, RW), lambda i, j: (li, 0, 0, 0)),
            pl.BlockSpec((None, 2, RW, tn), lambda i, j: (li, 0, 0, j)),
            pl.BlockSpec((None, 2, tn), lambda i, j: (li, 0, j)),
            pl.BlockSpec((None, 2, D, RA), lambda i, j: (li, 0, 0, 0)),
            pl.BlockSpec((None, 2, RA, tn), lambda i, j: (li, 0, 0, j)),
            pl.BlockSpec((None, 2, tn), lambda i, j: (li, 0, j)),
            pl.BlockSpec((None, D, RG), lambda i, j: (li, 0, 0)),
            pl.BlockSpec((None, RG, tn), lambda i, j: (li, 0, j)),
        ],
        out_specs=[out2, out2, pl.BlockSpec((tm, tn), lambda i, j: (i, j))],
        out_shape=[jax.ShapeDtypeStruct((2, M, D), F32), jax.ShapeDtypeStruct((2, M, D), F32),
                   jax.ShapeDtypeStruct((M, D), F32)],
        scratch_shapes=[pltpu.VMEM((2, tm, RW), BF16), pltpu.VMEM((2, tm, RA), BF16), pltpu.VMEM((tm, RG), BF16)],
        compiler_params=_cparams("parallel", "arbitrary"),
        name="rwkv_lora",
    )(xw, xa, xg, w1, w2, w0, a1, a2, a0, g1, g2)


def _pair_masks():
    row = lax.broadcasted_iota(jnp.int32, (LANES, LANES), 0)
    col = lax.broadcasted_iota(jnp.int32, (LANES, LANES), 1)
    same_head = (row < WKV_HEAD) == (col < WKV_HEAD)
    return row, col, same_head


def _wkv_kernel(r_ref, k_ref, v_ref, lw_ref, a_ref, par_ref, s0_ref, y_ref, z_ref, s_ref,
                st_ref, rh_ref, y0_ref, g_ref, j_ref, et_ref, *, rev):
    C = CHUNK
    Tt = r_ref.shape[0]
    nchunks = Tt // C

    @pl.when(pl.program_id(1) == 0)
    def _():
        st_ref[...] = s0_ref[...]

    k_k, k_a, r_k = par_ref[0:1], par_ref[1:2], par_ref[2:3]
    row, col, same_head = _pair_masks()
    ones_head = jnp.where(same_head, 1.0, 0.0).astype(BF16)
    tr, tc = row & (C - 1), col & (C - 1)
    before = (tc > tr) if rev else (tc < tr)
    before_eq = (tc >= tr) if rev else (tc <= tr)
    r64 = lax.broadcasted_iota(jnp.int32, (C, C), 0)
    c64 = lax.broadcasted_iota(jnp.int32, (C, C), 1)
    cum_m = jnp.where((c64 >= r64) if rev else (c64 <= r64), 1.0, 0.0).astype(BF16)
    eye = jnp.where(row == col, 1.0, 0.0)
    lane = lax.broadcasted_iota(jnp.int32, (C, LANES), 1)
    head_a = lane < WKV_HEAD

    def stack2f(x):
        return jnp.concatenate([jnp.where(head_a, x, 0.0), jnp.where(head_a, 0.0, x)], axis=0)

    def unstack(x2):
        return x2[0:C] + x2[C:2 * C]

    blk16 = (row >> 4) == (col >> 4)
    blk32 = (row >> 5) == (col >> 5)

    def each(f, *lists):
        return [f(*xs) for xs in zip(*lists)]

    def square(m):
        return each(lambda x: _dot(_bf(x), _bf(x)), m)

    def times_one_plus(t, m):
        return each(lambda x, y: x + _dot(_bf(x), _bf(y)), t, m)

    def merge(t, a_kb, inside, outside):
        sel = jnp.logical_and(inside, jnp.logical_not(outside))
        tb = each(_bf, t)
        lt = each(lambda x, y: _bf(_dot(_bf(jnp.where(sel, x, 0.0)), y)), a_kb, tb)
        return each(lambda x, y, w: x - _dot(y, w), t, tb, lt)

    sls = [pl.ds(c * C, C) for c in range(nchunks)]
    r, k, v, lw, a = ([ref[sl, :] for sl in sls] for ref in (r_ref, k_ref, v_ref, lw_ref, a_ref))
    kk = each(lambda x: x * k_k, k)
    nrm = each(lambda x: _exact_right(x * x, ones_head, 2), kk)
    kk = each(lambda x, y: x * lax.rsqrt(jnp.maximum(y, 1e-24)), kk, nrm)
    kd = each(lambda x, y: x * (1.0 + (y - 1.0) * k_a), k, a)
    beta = each(lambda x, y: x * y, a, kk)
    bonus = each(lambda x, y: _exact_right(x * y * r_k, ones_head, 2), r, kd)
    for sl, x, y in zip(sls, bonus, v):
        z_ref[sl, :] = x * y
    b = each(lambda x: _exact_left(cum_m, x, 3), lw)
    b_tot = each(lambda x: x[0:1] if rev else x[C - 1:C], b)
    e_b = each(jnp.exp, b)
    e_nb = each(lambda x: jnp.exp(-x), b)
    e_rest = each(lambda x, y: jnp.exp(y - x), b, b_tot)
    r2f = each(lambda x, y: stack2f(x * y), r, e_b)
    r2 = each(_bf, r2f)
    kap2 = each(lambda x, y, w: _bf(stack2f(x * jnp.exp(y - w))), kk, b, lw)
    k2 = each(lambda x, y: _bf(stack2f(x * y)), kd, e_nb)
    be2 = each(lambda x, y: _bf(stack2f(x * y)), beta, e_nb)
    v2 = each(lambda x: _bf(stack2f(x)), v)
    kc2 = each(lambda x, y: _bf(stack2f(x * y)), kd, e_rest)
    bc2 = each(lambda x, y: _bf(stack2f(x * y)), beta, e_rest)
    kb2 = each(lambda x, y: jnp.concatenate([x, y], axis=0), k2, be2)
    g_kap = each(_dot_nt, kap2, kb2)
    a_kk = each(lambda x: _bf(jnp.where(before, x[:, :LANES], 0.0)), g_kap)
    a_kb = each(lambda x: jnp.where(before, x[:, LANES:], 0.0), g_kap)
    m1 = each(lambda x: jnp.where(blk16, -x, 0.0), a_kb)
    m2 = square(m1)
    g_r = each(_dot_nt, r2, kb2)
    m4 = square(m2)
    t = times_one_plus(each(lambda x: eye + x, m1), m2)
    a_rk = each(lambda x: _bf(jnp.where(before_eq, x[:, :LANES], 0.0)), g_r)
    a_rb = each(lambda x: _bf(jnp.where(before_eq, x[:, LANES:], 0.0)), g_r)
    m8 = square(m4)
    t = times_one_plus(t, m4)
    w0 = each(lambda x, y: _bf(_dot(x, y)), a_kk, v2)
    t = times_one_plus(t, m8)
    t = merge(t, a_kb, blk32, blk16)
    tinv = each(_bf, merge(t, a_kb, same_head, blk32))
    khu = each(lambda x, y, w: _bf(_dot(x, jnp.concatenate([y, w], axis=1))), tinv, kap2, w0)
    kh = each(lambda x: x[:, :LANES], khu)
    u0 = each(lambda x: x[:, LANES:], khu)
    vu = each(lambda x, y: jnp.concatenate([x, y], axis=0), v2, u0)
    rh = each(lambda x, y, w: _bf(x - _dot(y, w)), r2f, a_rb, kh)
    y0 = each(lambda x, y, w: _dot(jnp.concatenate([x, -y], axis=1), w), a_rk, a_rb, vu)
    jj = each(lambda x, y, w: _dot_tn(x, jnp.concatenate([y, -w], axis=0)), vu, kc2, bc2)
    gg = each(lambda x, y: _bf(-_dot_tn(x, y)), kh, bc2)
    for c in range(nchunks):
        rh_ref[c] = rh[c]
        y0_ref[c] = y0[c]
        j_ref[c] = jj[c]
        g_ref[c] = gg[c]
        et_ref[c] = jnp.broadcast_to(jnp.exp(b_tot[c]), (8, LANES))

    S = st_ref[...]
    for c in (range(nchunks - 1, -1, -1) if rev else range(nchunks)):
        Sb = _bf(S)
        y_ref[pl.ds(c * C, C), :] = unstack(_dot_nt(rh_ref[c], Sb) + y0_ref[c])
        S = S * et_ref[c, 0:1, :] + _dot(Sb, g_ref[c]) + j_ref[c]
    st_ref[...] = S

    @pl.when(pl.program_id(1) == pl.num_programs(1) - 1)
    def _():
        s_ref[...] = S


def wkv_scan(r, k, v, lw, a, par, s0, d):
    T, D = r.shape
    HP = D // LANES
    Tt = _tile(T, 1024)
    NT = T // Tt
    nck = Tt // CHUNK
    rev = d == 1

    def tt(t):
        return NT - 1 - t if rev else t

    tok = pl.BlockSpec((Tt, LANES), lambda h, t: (tt(t), h))
    tok_d = pl.BlockSpec((None, Tt, LANES), lambda h, t: (d, tt(t), h))
    st = pl.BlockSpec((None, LANES, LANES), lambda h, t: (h, 0, 0))
    return pl.pallas_call(
        functools.partial(_wkv_kernel, rev=rev),
        grid=(HP, NT),
        in_specs=[tok, tok, tok, tok_d, tok_d, pl.BlockSpec((8, LANES), lambda h, t: (0, h)), st],
        out_specs=[tok, tok, st],
        out_shape=[jax.ShapeDtypeStruct((T, D), F32), jax.ShapeDtypeStruct((T, D), F32),
                   jax.ShapeDtypeStruct((HP, LANES, LANES), F32)],
        scratch_shapes=[
            pltpu.VMEM((LANES, LANES), F32),
            pltpu.VMEM((nck, LANES, LANES), BF16),
            pltpu.VMEM((nck, LANES, LANES), F32),
            pltpu.VMEM((nck, LANES, LANES), BF16),
            pltpu.VMEM((nck, LANES, LANES), F32),
            pltpu.VMEM((nck, 8, LANES), F32),
        ],
        compiler_params=_cparams("parallel", "arbitrary"),
        name="wkv_scan",
    )(r, k, v, lw, a, par, s0)


def _rwkv_combine_kernel(y0_ref, y1_ref, z0_ref, z1_ref, g_ref, ln_ref, o_ref):
    D = o_ref.shape[1]
    row = lax.broadcasted_iota(jnp.int32, (LANES, LANES), 0)
    col = lax.broadcasted_iota(jnp.int32, (LANES, LANES), 1)
    mean_m = jnp.where((row < WKV_HEAD) == (col < WKV_HEAD), 1.0 / WKV_HEAD, 0.0).astype(BF16)
    for c in range(D // LANES):
        cs = slice(c * LANES, (c + 1) * LANES)
        y = y0_ref[:, cs] + y1_ref[:, cs]
        yc = y - _exact_right(y, mean_m, 2)
        var = _exact_right(yc * yc, mean_m, 2)
        out = yc * lax.rsqrt(var + GN_EPS) * ln_ref[0:1, cs] + ln_ref[1:2, cs] + (z0_ref[:, cs] + z1_ref[:, cs])
        o_ref[:, cs] = _bf(out * g_ref[:, cs])


def rwkv_combine(y0, y1, z0, z1, g, ln):
    M, D = g.shape
    tm = _tile(M, 256)
    tok = pl.BlockSpec((tm, D), lambda i: (i, 0))
    return pl.pallas_call(
        _rwkv_combine_kernel,
        grid=(M // tm,),
        in_specs=[tok] * 5 + [pl.BlockSpec((8, D), lambda i: (0, 0))],
        out_specs=tok,
        out_shape=jax.ShapeDtypeStruct((M, D), BF16),
        compiler_params=_cparams("parallel"),
        name="rwkv_combine",
    )(y0, y1, z0, z1, g, ln)


def _gla_kernel(q_ref, i_ref, f_ref, lbl_ref, s0_ref, o_ref, s_ref, st_ref, qe_ref, j_ref, et_ref, *, rev, layer):
    C = CHUNK
    Tt = q_ref.shape[0]
    nchunks = Tt // C
    nsub = C // SUB

    @pl.when(pl.program_id(1) == 0)
    def _():
        st_ref[...] = s0_ref[...]

    logits = lbl_ref[...]
    e = jnp.exp(logits - jnp.max(logits, axis=0, keepdims=True))
    p = e / jnp.sum(e, axis=0, keepdims=True)
    lb = jnp.zeros((1, LANES), F32)
    for l in range(1, layer + 1):
        lb = lb + p[l:l + 1]
    log_lb = jnp.log(lb)
    log_1m = jnp.log1p(-lb)

    r64 = lax.broadcasted_iota(jnp.int32, (C, C), 0)
    c64 = lax.broadcasted_iota(jnp.int32, (C, C), 1)
    cum_m = jnp.where((c64 >= r64) if rev else (c64 <= r64), 1.0, 0.0).astype(BF16)
    rows = lax.broadcasted_iota(jnp.int32, (C, 1), 0)
    lane_s = lax.broadcasted_iota(jnp.int32, (SUB, C), 1)
    row_s = lax.broadcasted_iota(jnp.int32, (SUB, 1), 0)

    def each(f, *lists):
        return [f(*xs) for xs in zip(*lists)]

    sls = [pl.ds(c * C, C) for c in range(nchunks)]
    q = [_silu(q_ref[sl, :]) for sl in sls]
    v = [i_ref[sl, :] for sl in sls]
    f = [f_ref[sl, :] for sl in sls]
    kg = each(lambda x: (1.0 - lb) * _sigmoid(-x), f)
    x2 = each(lambda x: log_1m - _softplus(-x), f)
    g = each(lambda x: jnp.maximum(log_lb, x) + jnp.log1p(jnp.exp(-jnp.abs(log_lb - x))), x2)
    b = each(lambda x: _exact_left(cum_m, x, 3), g)
    b_tot = each(lambda x: x[0:1] if rev else x[C - 1:C], b)
    vb = each(_bf, v)
    att_rows = [[] for _ in range(nchunks)]
    for I in range(nsub):
        lo = I * SUB
        qI = each(lambda x: x[lo:lo + SUB], q)
        bI = each(lambda x: x[lo:lo + SUB], b)
        first = (I == nsub - 1) if rev else (I == 0)
        if first:
            att = [jnp.zeros((SUB, C), F32)] * nchunks
        else:
            ref = each(lambda x: x[lo + SUB:lo + SUB + 1] if rev else x[lo - 1:lo], b)
            earlier = (rows >= lo + SUB) if rev else (rows < lo)
            kt = each(lambda x, y, w: _bf(jnp.where(earlier, x * jnp.exp(w - y), 0.0)), kg, b, ref)
            qt = each(lambda x, y, w: _bf(x * jnp.exp(y - w)), qI, bI, ref)
            att = each(_dot_nt, qt, kt)
        for j in range(SUB):
            s = lo + j
            place = jnp.logical_and(lane_s == s, (row_s <= j) if rev else (row_s >= j))
            pj = each(lambda x, y, w, u: x * y[s:s + 1] * jnp.exp(w - u[s:s + 1]), qI, kg, bI, b)
            att = each(lambda x, y: x + jnp.where(place, jnp.sum(y, axis=-1, keepdims=True), 0.0), att, pj)
        for c in range(nchunks):
            att_rows[c].append(att[c])
    o_in = each(lambda x, y: _dot(_bf(jnp.concatenate(x, axis=0)), y), att_rows, vb)
    for c in range(nchunks):
        o_ref[sls[c], :] = o_in[c]
        qe_ref[c] = _bf(q[c] * jnp.exp(b[c]))
        j_ref[c] = _dot_tn(vb[c], _bf(kg[c] * jnp.exp(b_tot[c] - b[c])))
        et_ref[c] = jnp.broadcast_to(jnp.exp(b_tot[c]), (8, LANES))

    Z = st_ref[...]
    for c in (range(nchunks - 1, -1, -1) if rev else range(nchunks)):
        o_ref[sls[c], :] += _dot_nt(qe_ref[c], _bf(Z))
        Z = Z * et_ref[c, 0:1, :] + j_ref[c]
    st_ref[...] = Z

    @pl.when(pl.program_id(1) == pl.num_programs(1) - 1)
    def _():
        s_ref[...] = Z


def gla_scan(raw, lb_logits, s0, d, layer):
    T = raw.shape[0]
    D = raw.shape[1] // 5
    H = D // LANES
    Tt = _tile(T, 1024)
    NT = T // Tt
    nck = Tt // CHUNK
    rev = d == 1
    L = lb_logits.shape[0]

    def tt(t):
        return NT - 1 - t if rev else t

    st = pl.BlockSpec((None, LANES, LANES), lambda h, t: (h, 0, 0))
    return pl.pallas_call(
        functools.partial(_gla_kernel, rev=rev, layer=layer),
        grid=(H, NT),
        in_specs=[
            pl.BlockSpec((Tt, LANES), lambda h, t: (tt(t), h)),
            pl.BlockSpec((Tt, LANES), lambda h, t: (tt(t), H + h)),
            pl.BlockSpec((Tt, LANES), lambda h, t: (tt(t), (2 + d) * H + h)),
            pl.BlockSpec((L, LANES), lambda h, t: (0, h)),
            st,
        ],
        out_specs=[pl.BlockSpec((Tt, LANES), lambda h, t: (tt(t), h)), st],
        out_shape=[jax.ShapeDtypeStruct((T, D), F32), jax.ShapeDtypeStruct((H, LANES, LANES), F32)],
        scratch_shapes=[
            pltpu.VMEM((LANES, LANES), F32),
            pltpu.VMEM((nck, CHUNK, LANES), BF16),
            pltpu.VMEM((nck, LANES, LANES), F32),
            pltpu.VMEM((nck, 8, LANES), F32),
        ],
        compiler_params=_cparams("parallel", "arbitrary"),
        name="gla_scan",
    )(raw, raw, raw, lb_logits, s0)


def _hgrn_combine_kernel(o0_ref, o1_ref, g_ref, gn_ref, o_ref):
    D = o_ref.shape[1]
    for c in range(D // LANES):
        cs = slice(c * LANES, (c + 1) * LANES)
        o = o0_ref[:, cs] + o1_ref[:, cs]
        o = o * lax.rsqrt(jnp.mean(o * o, axis=-1, keepdims=True) + EPS) * gn_ref[0:1, cs]
        o_ref[:, cs] = _bf(o * _silu(g_ref[:, cs]))


def hgrn_combine(o0, o1, raw, gn):
    M, D = o0.shape
    tm = _tile(M, 256)
    tok = pl.BlockSpec((tm, D), lambda i: (i, 0))
    return pl.pallas_call(
        _hgrn_combine_kernel,
        grid=(M // tm,),
        in_specs=[tok, tok, pl.BlockSpec((tm, D), lambda i: (i, 4)), pl.BlockSpec((1, D), lambda i: (0, 0))],
        out_specs=tok,
        out_shape=jax.ShapeDtypeStruct((M, D), BF16),
        compiler_params=_cparams("parallel"),
        name="hgrn_combine",
    )(o0, o1, raw, gn)


def _rows8(*vecs):
    D = vecs[0].shape[-1]
    rows = [v.reshape(1, D) for v in vecs]
    rows.append(jnp.zeros((8 - len(rows), D), F32))
    return jnp.concatenate(rows, axis=0)


def kernel(x, c, ctx, c_ctx, ada_down, ada_up, ada_b, norm_g, ffn_w13, ffn_w2, final_g, lru_w_in, lru_conv_w, lru_conv_b, lru_gate_a_w, lru_gate_a_b, lru_gate_x_w, lru_gate_x_b, lru_lam, lru_w_out, rwkv_mu, rwkv_w_r, rwkv_w_k, rwkv_w_v, rwkv_w_o, rwkv_w0, rwkv_w1, rwkv_w2, rwkv_a0, rwkv_a1, rwkv_a2, rwkv_g1, rwkv_g2, rwkv_k_k, rwkv_k_a, rwkv_r_k, rwkv_ln_w, rwkv_ln_b, hgrn_w_in, hgrn_lb_logits, hgrn_gn_g, hgrn_w_out):
    B, T, D = x.shape
    assert B == 1, "one sequence per call"
    depth = ada_down.shape[0]
    xl, xc = x[0], ctx[0]

    mods = ada_all_layers(_rows8(c[0], c_ctx), ada_down, ada_up, ada_b)
    mods = mods[:, :2].reshape(depth, 2, N_MOD, D)

    lru_wa, lru_wx = _bf(lru_gate_a_w), _bf(lru_gate_x_w)
    lw1, lw2, la1, la2 = _bf(rwkv_w1), _bf(rwkv_w2), _bf(rwkv_a1), _bf(rwkv_a2)
    lg1, lg2 = _bf(rwkv_g1), _bf(rwkv_g2)

    for i in range(depth):
        need_ctx = i < depth - 1
        kind, j = i % 3, i // 3
        ml, mc = mods[i, 0], mods[i, 1]

        def ffn_mod(m, k, which):
            return _rows8(norm_g[i, 2 * which], m[k], m[k + 1], m[k + 2])

        xc, *wb = half_ffn(xc, ffn_mod(mc, 0, 0), (ffn_w13, ffn_w2), (i, 0))
        xl = half_ffn(xl, ffn_mod(ml, 0, 0), wb)
        mod_l = _rows8(norm_g[i, 1], ml[3], ml[4])
        mod_c = _rows8(norm_g[i, 1], mc[3], mc[4])
        gate_l, gate_c = _rows8(ml[5]), _rows8(mc[5])

        if kind == 0:
            p_c, w_in = proj(xc, lru_w_in, (j,), mod=mod_c, gelu_cols=D, emit_weight=True)
            p_l = proj(xl, w_in, mod=mod_l, gelu_cols=D)
            hs = lru_scan(p_l, p_c, lru_conv_w, lru_conv_b, lru_wa, lru_wx, lru_gate_a_b, lru_gate_x_b, lru_lam,
                          j, need_ctx)
            a_l = lru_combine(p_l, hs[0])
            a_c = lru_combine(p_c, hs[1]) if need_ctx else None
            w_last = lru_w_out
        elif kind == 1:
            par = _rows8(rwkv_k_k[j], rwkv_k_a[j], rwkv_r_k[j].reshape(D))
            ln = _rows8(rwkv_ln_w[j], rwkv_ln_b[j])
            outs = []
            state = [jnp.zeros((D // LANES, LANES, LANES), F32)] * 2
            for xs, mod, is_grid in ((xc, mod_c, False), (xl, mod_l, True)):
                xr, xw, xk, xv, xa, xg = rwkv_shiftmix(xs, mod, rwkv_mu, j, is_grid)
                if not is_grid:
                    r, w_r = proj(xr, rwkv_w_r, (j,), emit_weight=True)
                    k, w_k = proj(xk, rwkv_w_k, (j,), emit_weight=True)
                    v, w_v = proj(xv, rwkv_w_v, (j,), emit_weight=True)
                else:
                    r, k, v = proj(xr, w_r), proj(xk, w_k), proj(xv, w_v)
                lw, a, g = rwkv_lora(xw, xa, xg, lw1, lw2, rwkv_w0, la1, la2, rwkv_a0, lg1, lg2, j)
                ys, zs = [], []
                for d in (0, 1):
                    y, z, state[d] = wkv_scan(r, k, v, lw, a, par, state[d], d)
                    ys.append(y)
                    zs.append(z)
                outs.append(rwkv_combine(ys[0], ys[1], zs[0], zs[1], g, ln))
            a_c, a_l = outs
            w_last = rwkv_w_o
        else:
            outs = []
            state = [jnp.zeros((D // LANES, LANES, LANES), F32)] * 2
            for xs, mod, is_ctx in ((xc, mod_c, True), (xl, mod_l, False)):
                if is_ctx:
                    raw, w_in = proj(xs, hgrn_w_in, (j,), mod=mod, emit_weight=True)
                else:
                    raw = proj(xs, w_in, mod=mod)
                os_ = []
                for d in (0, 1):
                    o, state[d] = gla_scan(raw, hgrn_lb_logits, state[d], d, i)
                    os_.append(o)
                outs.append(hgrn_combine(os_[0], os_[1], raw, hgrn_gn_g[j].reshape(1, D)))
            a_c, a_l = outs
            w_last = hgrn_w_out

        if need_ctx:
            xc, w_out = proj(a_c, w_last, (j,), res=xc, gate=gate_c, emit_weight=True)
            xc, *wb = half_ffn(xc, ffn_mod(mc, 6, 1), (ffn_w13, ffn_w2), (i, 1))
        else:
            w_out = _bf(w_last[j])
            wb = round_ffn_weights(ffn_w13, ffn_w2, (i, 1))
        xl = proj(a_l, w_out, res=xl, gate=gate_l)
        xl = half_ffn(xl, ffn_mod(ml, 6, 1), wb)

    return final_rmsnorm(xl, final_g)[None]
```

```python
import functools
import math

import jax
import jax.numpy as jnp
from jax import lax
from jax.experimental import pallas as pl
from jax.experimental.pallas import tpu as pltpu

F32 = jnp.float32
BF16 = jnp.bfloat16

EPS = 1e-6
GN_EPS = 64e-5
LRU_C = 8.0
GRID_W = 64
N_MOD = 9
WKV_HEAD = 64
GLA_HEAD = 128
CHUNK = 64
SUB = 16
LANES = 128
VMEM_LIMIT_BYTES = 60 * 1024 * 1024


def _cparams(*sem):
    return pltpu.CompilerParams(dimension_semantics=sem, vmem_limit_bytes=VMEM_LIMIT_BYTES)


def _tile(n, pref):
    if n <= pref:
        return n
    for t in range(pref, 7, -1):
        if n % t == 0 and t % 8 == 0:
            return t
    return n


def _bf(x):
    return x.astype(BF16)


def _dot(a, b):
    return jnp.dot(a, b, preferred_element_type=F32)


def _dot_nt(a, b):
    return lax.dot_general(a, b, (((1,), (1,)), ((), ())), preferred_element_type=F32)


def _dot_tn(a, b):
    return lax.dot_general(a, b, (((0,), (0,)), ((), ())), preferred_element_type=F32)


def _split_terms(x, terms):
    out, rem = [], x
    for _ in range(terms):
        p = _bf(rem)
        out.append(p)
        rem = rem - p.astype(F32)
    return out


def _exact_left(m_bf, x, terms):
    acc = None
    for p in _split_terms(x, terms):
        d = _dot(m_bf, p)
        acc = d if acc is None else acc + d
    return acc


def _exact_right(x, m_bf, terms):
    acc = None
    for p in _split_terms(x, terms):
        d = _dot(p, m_bf)
        acc = d if acc is None else acc + d
    return acc


def _sigmoid(x):
    return 0.5 * jnp.tanh(0.5 * x) + 0.5


def _silu(x):
    return x * _sigmoid(x)


def _gelu_tanh(x):
    return 0.5 * x * (1.0 + jnp.tanh(math.sqrt(2.0 / math.pi) * (x + 0.044715 * (x * x * x))))


def _softplus(x):
    return jnp.maximum(x, 0.0) + jnp.log1p(jnp.exp(-jnp.abs(x)))


def _row_rsqrt_ms(x):
    D = x.shape[-1]
    if D % LANES == 0:
        acc = x[:, 0:LANES] * x[:, 0:LANES]
        for c in range(LANES, D, LANES):
            acc = acc + x[:, c:c + LANES] * x[:, c:c + LANES]
        ms = jnp.sum(acc, axis=-1, keepdims=True) * (1.0 / D)
    else:
        ms = jnp.mean(x * x, axis=-1, keepdims=True)
    return lax.rsqrt(ms + EPS)


def _normmod(x, g, shift, scale):
    return (x * _row_rsqrt_ms(x)) * (g * (1.0 + scale)) + shift


def _ada_kernel(cc_ref, down_ref, up_ref, b_ref, o_ref, t_ref):
    hi = lax.Precision.HIGHEST

    @pl.when(pl.program_id(1) == 0)
    def _():
        t_ref[...] = jnp.dot(_silu(cc_ref[...]), down_ref[...], precision=hi, preferred_element_type=F32)

    o_ref[...] = jnp.dot(t_ref[...], up_ref[...], precision=hi, preferred_element_type=F32) + b_ref[...]


def ada_all_layers(cc, down, up, bias):
    L, D, R = down.shape
    N = up.shape[2]
    tn = _tile(N, 4096)
    return pl.pallas_call(
        _ada_kernel,
        grid=(L, N // tn),
        in_specs=[
            pl.BlockSpec((8, D), lambda l, j: (0, 0)),
            pl.BlockSpec((None, D, R), lambda l, j: (l, 0, 0)),
            pl.BlockSpec((None, R, tn), lambda l, j: (l, 0, j)),
            pl.BlockSpec((None, 1, tn), lambda l, j: (l, 0, j)),
        ],
        out_specs=pl.BlockSpec((None, 8, tn), lambda l, j: (l, 0, j)),
        out_shape=jax.ShapeDtypeStruct((L, 8, N), F32),
        scratch_shapes=[pltpu.VMEM((8, R), F32)],
        compiler_params=_cparams("parallel", "arbitrary"),
        name="ada",
    )(cc, down, up, bias.reshape(L, 1, N))


def _ffn_kernel(x_ref, mod_ref, w1_ref, w3_ref, w2_ref, o_ref, *rest, emit_weights, final_norm):
    h_ref, rs_ref = rest[-2:]
    f = pl.program_id(1)
    tm, D = o_ref.shape
    strip = math.gcd(tm, 16)

    def for_strips(body):
        def step(s, carry):
            body(pl.ds(pl.multiple_of(s * strip, strip), strip))
            return carry
        lax.fori_loop(0, tm // strip, step, 0, unroll=math.gcd(tm // strip, 4))

    @pl.when(f == 0)
    def _():
        def scale_rows(rows):
            rs_ref[rows, :] = jnp.broadcast_to(_row_rsqrt_ms(x_ref[rows, :]), (strip, LANES))
        for_strips(scale_rows)
        gain = mod_ref[0:1] * (1.0 + mod_ref[2:3])

        def normalise(rows):
            h_ref[rows, :] = _bf(x_ref[rows, :] * rs_ref[rows, 0:1] * gain + mod_ref[1:2])
            o_ref[rows, :] = jnp.zeros((strip, D), F32)
        for_strips(normalise)

    w1, w3 = _bf(w1_ref[...]), _bf(w3_ref[...])
    if emit_weights:
        rest[0][...] = w1
        rest[1][...] = w3
    h = h_ref[...]
    act = _bf(_silu(_dot(h, w1)) * _dot(h, w3))
    slab = min(D, 512)
    for c0 in range(0, D, slab):
        w2 = _bf(w2_ref[:, c0:c0 + slab])
        if emit_weights:
            rest[2][:, c0:c0 + slab] = w2
        o_ref[:, c0:c0 + slab] += _dot(act, w2)

    @pl.when(f == pl.num_programs(1) - 1)
    def _():
        def body(rows):
            y = x_ref[rows, :] + 0.5 * mod_ref[3:4] * o_ref[rows, :]
            if final_norm:
                y = y * _row_rsqrt_ms(y) * mod_ref[4:5]
            o_ref[rows, :] = y
        for_strips(body)


def half_ffn(x, mod, weights, index=None, final_norm=False):
    M, D = x.shape
    emit = index is not None
    F = weights[-1].shape[-2]
    tm = _tile(M, 512)
    tf = _tile(F, 256 if emit else 512)
    nf = F // tf
    if emit:
        assert M == tm
        w13, w2 = weights
        lead = tuple(index)
        w_specs = [
            pl.BlockSpec((None, None, D, tf), lambda i, f: lead + (0, f)),
            pl.BlockSpec((None, None, D, tf), lambda i, f: lead + (0, nf + f)),
            pl.BlockSpec((None, None, tf, D), lambda i, f: lead + (f, 0)),
        ]
        w_args = (w13, w13, w2)
    else:
        w_specs = [
            pl.BlockSpec((D, tf), lambda i, f: (0, f)),
            pl.BlockSpec((D, tf), lambda i, f: (0, f)),
            pl.BlockSpec((tf, D), lambda i, f: (f, 0)),
        ]
        w_args = tuple(weights)
    out_specs = [pl.BlockSpec((tm, D), lambda i, f: (i, 0))]
    out_shape = [jax.ShapeDtypeStruct((M, D), F32)]
    if emit:
        out_specs += [
            pl.BlockSpec((D, tf), lambda i, f: (0, f)),
            pl.BlockSpec((D, tf), lambda i, f: (0, f)),
            pl.BlockSpec((tf, D), lambda i, f: (f, 0)),
        ]
        out_shape += [jax.ShapeDtypeStruct((D, F), BF16), jax.ShapeDtypeStruct((D, F), BF16),
                      jax.ShapeDtypeStruct((F, D), BF16)]
    out = pl.pallas_call(
        functools.partial(_ffn_kernel, emit_weights=emit, final_norm=final_norm),
        grid=(M // tm, nf),
        in_specs=[
            pl.BlockSpec((tm, D), lambda i, f: (i, 0), pipeline_mode=pl.Buffered(1)),
            pl.BlockSpec((8, D), lambda i, f: (0, 0)),
        ] + w_specs,
        out_specs=out_specs,
        out_shape=out_shape,
        scratch_shapes=[pltpu.VMEM((tm, D), BF16), pltpu.VMEM((tm, LANES), F32)],
        compiler_params=_cparams("parallel", "arbitrary"),
        name="half_ffn",
    )(x, mod, *w_args)
    return out if emit else out[0]


def _round_weights_kernel(w1_ref, w3_ref, w2_ref, w1b_ref, w3b_ref, w2b_ref):
    w1b_ref[...] = _bf(w1_ref[...])
    w3b_ref[...] = _bf(w3_ref[...])
    w2b_ref[...] = _bf(w2_ref[...])


def round_ffn_weights(w13, w2, index):
    D, F = w2.shape[-1], w2.shape[-2]
    tf = _tile(F, 256)
    nf = F // tf
    lead = tuple(index)
    return pl.pallas_call(
        _round_weights_kernel,
        grid=(nf,),
        in_specs=[
            pl.BlockSpec((None, None, D, tf), lambda f: lead + (0, f)),
            pl.BlockSpec((None, None, D, tf), lambda f: lead + (0, nf + f)),
            pl.BlockSpec((None, None, tf, D), lambda f: lead + (f, 0)),
        ],
        out_specs=[
            pl.BlockSpec((D, tf), lambda f: (0, f)),
            pl.BlockSpec((D, tf), lambda f: (0, f)),
            pl.BlockSpec((tf, D), lambda f: (f, 0)),
        ],
        out_shape=[jax.ShapeDtypeStruct((D, F), BF16), jax.ShapeDtypeStruct((D, F), BF16),
                   jax.ShapeDtypeStruct((F, D), BF16)],
        compiler_params=_cparams("parallel"),
        name="round_ffn_weights",
    )(w13, w13, w2)


def _proj_kernel(*refs, norm, n_gelu, residual, emit_weight):
    it = iter(refs)
    a_ref = next(it)
    mod_ref = next(it) if norm else None
    w_ref = next(it)
    res_ref = next(it) if residual else None
    gate_ref = next(it) if residual else None
    o_ref = next(it)
    wb_ref = next(it) if emit_weight else None
    h_ref = next(it) if norm else None
    j = pl.program_id(1)

    if norm:
        @pl.when(j == 0)
        def _():
            h_ref[...] = _bf(_normmod(a_ref[...], mod_ref[0:1], mod_ref[1:2], mod_ref[2:3]))
        lhs = h_ref[...]
    else:
        lhs = a_ref[...]
    w = _bf(w_ref[...])
    if emit_weight:
        wb_ref[...] = w
    acc = _dot(lhs, w)
    if residual:
        o_ref[...] = res_ref[...] + gate_ref[0:1] * acc
    elif n_gelu:
        @pl.when(j < n_gelu)
        def _():
            o_ref[...] = _gelu_tanh(acc)

        @pl.when(j >= n_gelu)
        def _():
            o_ref[...] = acc
    else:
        o_ref[...] = acc


def proj(a, w, w_index=(), *, mod=None, gelu_cols=0, res=None, gate=None, emit_weight=False):
    M, K = a.shape
    N = w.shape[-1]
    tm = _tile(M, 512)
    tn = _tile(math.gcd(N, gelu_cols) if gelu_cols else N, 512 if emit_weight else 1024)
    norm = mod is not None
    residual = res is not None
    assert gelu_cols % tn == 0 and (M == tm or not emit_weight)
    lead = tuple(w_index)
    in_specs = [pl.BlockSpec((tm, K), lambda i, j: (i, 0))]
    args = [a]
    if norm:
        in_specs.append(pl.BlockSpec((8, K), lambda i, j: (0, 0)))
        args.append(mod)
    in_specs.append(pl.BlockSpec((None,) * len(lead) + (K, tn), lambda i, j: lead + (0, j)))
    args.append(w)
    if residual:
        in_specs.append(pl.BlockSpec((tm, tn), lambda i, j: (i, j)))
        in_specs.append(pl.BlockSpec((8, tn), lambda i, j: (0, j)))
        args += [res, gate]
    out_specs = [pl.BlockSpec((tm, tn), lambda i, j: (i, j))]
    out_shape = [jax.ShapeDtypeStruct((M, N), F32)]
    if emit_weight:
        out_specs.append(pl.BlockSpec((K, tn), lambda i, j: (0, j)))
        out_shape.append(jax.ShapeDtypeStruct((K, N), BF16))
    out = pl.pallas_call(
        functools.partial(_proj_kernel, norm=norm, n_gelu=gelu_cols // tn, residual=residual,
                          emit_weight=emit_weight),
        grid=(M // tm, N // tn),
        in_specs=in_specs,
        out_specs=out_specs,
        out_shape=out_shape,
        scratch_shapes=[pltpu.VMEM((tm, K), BF16)] if norm else [],
        compiler_params=_cparams("parallel", "arbitrary"),
        name="proj",
    )(*args)
    return out if emit_weight else out[0]


def _lru_kernel(ul_ref, uc_ref, cw_ref, cb_ref, wa_ref, wx_ref, ba_ref, bx_ref, lam_ref, *refs, R, need_ctx):
    hl_ref = refs[0]
    hc_ref = refs[1] if need_ctx else None
    v_ref, hs_ref = refs[-2:]
    T, W = ul_ref.shape
    NL = W // LANES
    Tc = uc_ref.shape[0]
    cw = cw_ref[...]
    cb = cb_ref[...]

    def conv_chunk(u_ref, t0, n_rows):
        main = u_ref[pl.ds(t0, R), :]
        p0 = pl.multiple_of(jnp.maximum(t0 - 8, 0), 8)
        n0 = pl.multiple_of(jnp.minimum(t0 + R, n_rows - 8), 8)
        prev = jnp.where(t0 > 0, u_ref[pl.ds(p0, 8), :], 0.0)
        nxt = jnp.where(t0 + R < n_rows, u_ref[pl.ds(n0, 8), :], 0.0)
        ext = jnp.concatenate([prev, main, nxt], axis=0)
        n = R + 16
        out = cb + ext[8:8 + R] * cw[2:3]
        out = out + pltpu.roll(ext, 2, 0)[8:8 + R] * cw[0:1]
        out = out + pltpu.roll(ext, 1, 0)[8:8 + R] * cw[1:2]
        out = out + pltpu.roll(ext, n - 1, 0)[8:8 + R] * cw[3:4]
        return out

    G = R // 8

    def scan_chunk(d, carry, rev):
        vp = jnp.concatenate(
            [jnp.concatenate([v_ref[j, pl.ds(g, 8, stride=G), :] for j in range(NL)], axis=1) for g in range(G)],
            axis=0)
        vb = _bf(vp)
        r = _sigmoid(_dot(vb, wa_ref[d]) + ba_ref[d:d + 1])
        gi = _sigmoid(_dot(vb, wx_ref[d]) + bx_ref[d:d + 1])
        log_a = -LRU_C * r * _softplus(-lam_ref[d:d + 1])
        a = jnp.exp(log_a)
        b = jnp.sqrt(-jnp.tanh(log_a) * (a * a + 1.0)) * (gi * vp)
        h = jnp.zeros((8, W), F32)
        p = jnp.ones((8, W), F32)
        hs, ps = [None] * G, [None] * G
        for g in (range(G - 1, -1, -1) if rev else range(G)):
            ag = a[8 * g:8 * g + 8]
            h = ag * h + b[8 * g:8 * g + 8]
            p = ag * p
            hs[g], ps[g] = h, p
        starts = [None] * 8
        for s in (range(7, -1, -1) if rev else range(8)):
            starts[s] = carry
            carry = h[s:s + 1] + p[s:s + 1] * carry
        start = jnp.concatenate(starts, axis=0)
        return [hs[g] + ps[g] * start for g in range(G)], carry

    def run(u_ref, o_ref, n_rows, d, carry):
        rev = d == 1
        nchunks = n_rows // R

        def body(i, carry):
            c = nchunks - 1 - i if rev else i
            t0 = pl.multiple_of(c * R, R)
            v = conv_chunk(u_ref, t0, n_rows)
            for j in range(NL):
                v_ref[j] = v[:, j * LANES:(j + 1) * LANES]
            hs, carry = scan_chunk(d, carry, rev)
            if o_ref is not None:
                for g in range(G):
                    for j in range(NL):
                        hs_ref[j, pl.ds(g, 8, stride=G), :] = hs[g][:, j * LANES:(j + 1) * LANES]
                h = jnp.concatenate([hs_ref[j] for j in range(NL)], axis=1)
                if d == 0:
                    o_ref[pl.ds(t0, R), :] = h
                else:
                    o_ref[pl.ds(t0, R), :] += h
            return carry

        return lax.fori_loop(0, nchunks, body, carry)

    for d in (0, 1):
        carry = run(uc_ref, hc_ref, Tc, d, jnp.zeros((1, W), F32))
        run(ul_ref, hl_ref, T, d, carry)


def lru_scan(pl_out, pc_out, conv_w, conv_b, wa, wx, ba, bx, lam, li, need_ctx):
    T = pl_out.shape[0]
    Tc = pc_out.shape[0]
    D = pl_out.shape[1] // 2
    NH, W = wa.shape[2], wa.shape[3]
    assert W % LANES == 0 and NH * W == D
    R = _tile(math.gcd(T, Tc), 256)
    col = D // W
    vec2 = pl.BlockSpec((None, 2, W), lambda h: (li, 0, h))
    out_shape = [jax.ShapeDtypeStruct((T, D), F32)]
    out_specs = [pl.BlockSpec((T, W), lambda h: (0, h))]
    if need_ctx:
        out_shape.append(jax.ShapeDtypeStruct((Tc, D), F32))
        out_specs.append(pl.BlockSpec((Tc, W), lambda h: (0, h)))
    return pl.pallas_call(
        functools.partial(_lru_kernel, R=R, need_ctx=need_ctx),
        grid=(NH,),
        in_specs=[
            pl.BlockSpec((T, W), lambda h: (0, col + h)),
            pl.BlockSpec((Tc, W), lambda h: (0, col + h)),
            pl.BlockSpec((None, 4, W), lambda h: (li, 0, h)),
            pl.BlockSpec((None, 1, W), lambda h: (li, 0, h)),
            pl.BlockSpec((None, 2, None, W, W), lambda h: (li, 0, h, 0, 0)),
            pl.BlockSpec((None, 2, None, W, W), lambda h: (li, 0, h, 0, 0)),
            vec2, vec2, vec2,
        ],
        out_specs=out_specs,
        out_shape=out_shape,
        scratch_shapes=[pltpu.VMEM((W // LANES, R, LANES), F32)] * 2,
        compiler_params=_cparams("parallel"),
        name="lru_scan",
    )(pl_out, pc_out, conv_w, conv_b.reshape(conv_b.shape[0], 1, D), wa, wx, ba, bx, lam)


def _mul_kernel(a_ref, b_ref, o_ref):
    o_ref[...] = _bf(a_ref[...] * b_ref[...])


def lru_combine(p_out, hs):
    M, D = hs.shape
    tm = _tile(M, 256)
    return pl.pallas_call(
        _mul_kernel,
        grid=(M // tm,),
        in_specs=[pl.BlockSpec((tm, D), lambda i: (i, 0)), pl.BlockSpec((tm, D), lambda i: (i, 0))],
        out_specs=pl.BlockSpec((tm, D), lambda i: (i, 0)),
        out_shape=jax.ShapeDtypeStruct((M, D), BF16),
        compiler_params=_cparams("parallel"),
        name="lru_combine",
    )(p_out, hs)


def _mix_write(h, shifted, mu_ref, o_refs, cols):
    xx = shifted - h
    for n, o_ref in enumerate(o_refs):
        o_ref[:, cols] = _bf(h + xx * mu_ref[n:n + 1, cols])


def _mix_lat_kernel(xm_ref, xp_ref, xn_ref, mod_ref, mu_ref, *o_refs, T):
    tm, D = xm_ref.shape
    q = D // 4
    i = pl.program_id(0)
    g, sh, sc = mod_ref[0:1], mod_ref[1:2], mod_ref[2:3]
    hm = _normmod(xm_ref[...], g, sh, sc)
    hp = _normmod(xp_ref[...], g, sh, sc)
    hn = _normmod(xn_ref[...], g, sh, sc)
    row = lax.broadcasted_iota(jnp.int32, (tm, 1), 0)
    t = i * tm + row
    colpos = row & (GRID_W - 1)
    s0 = slice(0, q)
    left = jnp.where(colpos > 0, pltpu.roll(hm[:, s0], 1, 0), 0.0)
    _mix_write(hm[:, s0], left, mu_ref, o_refs, s0)
    s1 = slice(q, 2 * q)
    right = jnp.where(colpos < GRID_W - 1, pltpu.roll(hm[:, s1], tm - 1, 0), 0.0)
    _mix_write(hm[:, s1], right, mu_ref, o_refs, s1)
    s2 = slice(2 * q, 3 * q)
    up = jnp.concatenate([hp[:, s2], hm[:tm - GRID_W, s2]], axis=0) if tm > GRID_W else hp[:, s2]
    up = jnp.where(t >= GRID_W, up, 0.0)
    _mix_write(hm[:, s2], up, mu_ref, o_refs, s2)
    s3 = slice(3 * q, D)
    down = jnp.concatenate([hm[GRID_W:, s3], hn[:, s3]], axis=0) if tm > GRID_W else hn[:, s3]
    down = jnp.where(t < T - GRID_W, down, 0.0)
    _mix_write(hm[:, s3], down, mu_ref, o_refs, s3)


def _mix_ctx_kernel(x_ref, mod_ref, mu_ref, *o_refs):
    Tc, D = x_ref.shape
    hh = D // 2
    h = _normmod(x_ref[...], mod_ref[0:1], mod_ref[1:2], mod_ref[2:3])
    row = lax.broadcasted_iota(jnp.int32, (Tc, 1), 0)
    s0 = slice(0, hh)
    prev = jnp.where(row > 0, pltpu.roll(h[:, s0], 1, 0), 0.0)
    _mix_write(h[:, s0], prev, mu_ref, o_refs, s0)
    s1 = slice(hh, D)
    nxt = jnp.where(row < Tc - 1, pltpu.roll(h[:, s1], Tc - 1, 0), 0.0)
    _mix_write(h[:, s1], nxt, mu_ref, o_refs, s1)


def rwkv_shiftmix(x, mod, mu, li, grid_tokens):
    M, D = x.shape
    n_out = mu.shape[1]
    mu_spec_args = ((None, n_out, D),)
    out_shape = [jax.ShapeDtypeStruct((M, D), BF16)] * n_out
    if grid_tokens:
        tm = _tile(M, 256)
        assert tm % GRID_W == 0 and M % GRID_W == 0
        r = tm // GRID_W
        nb = M // GRID_W
        return pl.pallas_call(
            functools.partial(_mix_lat_kernel, T=M),
            grid=(M // tm,),
            in_specs=[
                pl.BlockSpec((tm, D), lambda i: (i, 0)),
                pl.BlockSpec((GRID_W, D), lambda i: (jnp.maximum(i * r - 1, 0), 0)),
                pl.BlockSpec((GRID_W, D), lambda i: (jnp.minimum((i + 1) * r, nb - 1), 0)),
                pl.BlockSpec((8, D), lambda i: (0, 0)),
                pl.BlockSpec(*mu_spec_args, lambda i: (li, 0, 0)),
            ],
            out_specs=[pl.BlockSpec((tm, D), lambda i: (i, 0))] * n_out,
            out_shape=out_shape,
            compiler_params=_cparams("parallel"),
            name="rwkv_shiftmix_grid",
        )(x, x, x, mod, mu)
    return pl.pallas_call(
        _mix_ctx_kernel,
        grid=(1,),
        in_specs=[
            pl.BlockSpec((M, D), lambda i: (0, 0)),
            pl.BlockSpec((8, D), lambda i: (0, 0)),
            pl.BlockSpec(*mu_spec_args, lambda i: (li, 0, 0)),
        ],
        out_specs=[pl.BlockSpec((M, D), lambda i: (0, 0))] * n_out,
        out_shape=out_shape,
        compiler_params=_cparams("arbitrary"),
        name="rwkv_shiftmix_seq",
    )(x, mod, mu)


def _lora_kernel(xw_ref, xa_ref, xg_ref, w1_ref, w2_ref, w0_ref, a1_ref, a2_ref, a0_ref, g1_ref, g2_ref,
                 lw_ref, a_ref, g_ref, tw_ref, ta_ref, tg_ref):
    @pl.when(pl.program_id(1) == 0)
    def _():
        def both_directions(x_ref, w_ref):
            rank = w_ref.shape[-1]
            if rank % LANES == 0:
                t = _dot(x_ref[...], jnp.concatenate([w_ref[0], w_ref[1]], axis=1))
                return t[:, :rank], t[:, rank:]
            return _dot(x_ref[...], w_ref[0]), _dot(x_ref[...], w_ref[1])

        for d, (tw, ta) in enumerate(zip(both_directions(xw_ref, w1_ref), both_directions(xa_ref, a1_ref))):
            tw_ref[d] = _bf(jnp.tanh(tw))
            ta_ref[d] = _bf(ta)
        tg_ref[...] = _bf(_sigmoid(_dot(xg_ref[...], g1_ref[...])))

    for d in (0, 1):
        z = w0_ref[d:d + 1] + _dot(tw_ref[d], w2_ref[d])
        lw_ref[d] = -math.exp(-0.5) * _sigmoid(z)
        a_ref[d] = _sigmoid(a0_ref[d:d + 1] + _dot(ta_ref[d], a2_ref[d]))
    g_ref[...] = _dot(tg_ref[...], g2_ref[...])


def rwkv_lora(xw, xa, xg, w1, w2, w0, a1, a2, a0, g1, g2, li):
    M, D = xw.shape
    RW, RA, RG = w1.shape[-1], a1.shape[-1], g1.shape[-1]
    tm = _tile(M, 256)
    tn = _tile(D, 1024)
    row = pl.BlockSpec((tm, D), lambda i, j: (i, 0))
    out2 = pl.BlockSpec((2, tm, tn), lambda i, j: (0, i, j))
    return pl.pallas_call(
        _lora_kernel,
        grid=(M // tm, D // tn),
        in_specs=[
            row, row, row,
            pl.BlockSpec((None, 2, D, RW), lambda i, j: (li, 0, 0, 0)),
            pl.BlockSpec((None, 2, RW, tn), lambda i, j: (li, 0, 0, j)),
            pl.BlockSpec((None, 2, tn), lambda i, j: (li, 0, j)),
            pl.BlockSpec((None, 2, D, RA), lambda i, j: (li, 0, 0, 0)),
            pl.BlockSpec((None, 2, RA, tn), lambda i, j: (li, 0, 0, j)),
            pl.BlockSpec((None, 2, tn), lambda i, j: (li, 0, j)),
            pl.BlockSpec((None, D, RG), lambda i, j: (li, 0, 0)),
            pl.BlockSpec((None, RG, tn), lambda i, j: (li, 0, j)),
        ],
        out_specs=[out2, out2, pl.BlockSpec((tm, tn), lambda i, j: (i, j))],
        out_shape=[jax.ShapeDtypeStruct((2, M, D), F32), jax.ShapeDtypeStruct((2, M, D), F32),
                   jax.ShapeDtypeStruct((M, D), F32)],
        scratch_shapes=[pltpu.VMEM((2, tm, RW), BF16), pltpu.VMEM((2, tm, RA), BF16), pltpu.VMEM((tm, RG), BF16)],
        compiler_params=_cparams("parallel", "arbitrary"),
        name="rwkv_lora",
    )(xw, xa, xg, w1, w2, w0, a1, a2, a0, g1, g2)


def _pair_masks():
    row = lax.broadcasted_iota(jnp.int32, (LANES, LANES), 0)
    col = lax.broadcasted_iota(jnp.int32, (LANES, LANES), 1)
    same_head = (row < WKV_HEAD) == (col < WKV_HEAD)
    return row, col, same_head


def _wkv_kernel(r_ref, k_ref, v_ref, lw_ref, a_ref, par_ref, s0_ref, y_ref, z_ref, s_ref,
                st_ref, rh_ref, y0_ref, g_ref, j_ref, et_ref, *, rev):
    C = CHUNK
    Tt = r_ref.shape[0]
    nchunks = Tt // C

    @pl.when(pl.program_id(1) == 0)
    def _():
        st_ref[...] = s0_ref[...]

    k_k, k_a, r_k = par_ref[0:1], par_ref[1:2], par_ref[2:3]
    row, col, same_head = _pair_masks()
    ones_head = jnp.where(same_head, 1.0, 0.0).astype(BF16)
    tr, tc = row & (C - 1), col & (C - 1)
    before = (tc > tr) if rev else (tc < tr)
    before_eq = (tc >= tr) if rev else (tc <= tr)
    r64 = lax.broadcasted_iota(jnp.int32, (C, C), 0)
    c64 = lax.broadcasted_iota(jnp.int32, (C, C), 1)
    cum_m = jnp.where((c64 >= r64) if rev else (c64 <= r64), 1.0, 0.0).astype(BF16)
    eye = jnp.where(row == col, 1.0, 0.0)
    lane = lax.broadcasted_iota(jnp.int32, (C, LANES), 1)
    head_a = lane < WKV_HEAD

    def stack2f(x):
        return jnp.concatenate([jnp.where(head_a, x, 0.0), jnp.where(head_a, 0.0, x)], axis=0)

    def unstack(x2):
        return x2[0:C] + x2[C:2 * C]

    blk16 = (row >> 4) == (col >> 4)
    blk32 = (row >> 5) == (col >> 5)

    def each(f, *lists):
        return [f(*xs) for xs in zip(*lists)]

    def square(m):
        return each(lambda x: _dot(_bf(x), _bf(x)), m)

    def times_one_plus(t, m):
        return each(lambda x, y: x + _dot(_bf(x), _bf(y)), t, m)

    def merge(t, a_kb, inside, outside):
        sel = jnp.logical_and(inside, jnp.logical_not(outside))
        tb = each(_bf, t)
        lt = each(lambda x, y: _bf(_dot(_bf(jnp.where(sel, x, 0.0)), y)), a_kb, tb)
        return each(lambda x, y, w: x - _dot(y, w), t, tb, lt)

    sls = [pl.ds(c * C, C) for c in range(nchunks)]
    r, k, v, lw, a = ([ref[sl, :] for sl in sls] for ref in (r_ref, k_ref, v_ref, lw_ref, a_ref))
    kk = each(lambda x: x * k_k, k)
    nrm = each(lambda x: _exact_right(x * x, ones_head, 2), kk)
    kk = each(lambda x, y: x * lax.rsqrt(jnp.maximum(y, 1e-24)), kk, nrm)
    kd = each(lambda x, y: x * (1.0 + (y - 1.0) * k_a), k, a)
    beta = each(lambda x, y: x * y, a, kk)
    bonus = each(lambda x, y: _exact_right(x * y * r_k, ones_head, 2), r, kd)
    for sl, x, y in zip(sls, bonus, v):
        z_ref[sl, :] = x * y
    b = each(lambda x: _exact_left(cum_m, x, 3), lw)
    b_tot = each(lambda x: x[0:1] if rev else x[C - 1:C], b)
    e_b = each(jnp.exp, b)
    e_nb = each(lambda x: jnp.exp(-x), b)
    e_rest = each(lambda x, y: jnp.exp(y - x), b, b_tot)
    r2f = each(lambda x, y: stack2f(x * y), r, e_b)
    r2 = each(_bf, r2f)
    kap2 = each(lambda x, y, w: _bf(stack2f(x * jnp.exp(y - w))), kk, b, lw)
    k2 = each(lambda x, y: _bf(stack2f(x * y)), kd, e_nb)
    be2 = each(lambda x, y: _bf(stack2f(x * y)), beta, e_nb)
    v2 = each(lambda x: _bf(stack2f(x)), v)
    kc2 = each(lambda x, y: _bf(stack2f(x * y)), kd, e_rest)
    bc2 = each(lambda x, y: _bf(stack2f(x * y)), beta, e_rest)
    kb2 = each(lambda x, y: jnp.concatenate([x, y], axis=0), k2, be2)
    g_kap = each(_dot_nt, kap2, kb2)
    a_kk = each(lambda x: _bf(jnp.where(before, x[:, :LANES], 0.0)), g_kap)
    a_kb = each(lambda x: jnp.where(before, x[:, LANES:], 0.0), g_kap)
    m1 = each(lambda x: jnp.where(blk16, -x, 0.0), a_kb)
    m2 = square(m1)
    g_r = each(_dot_nt, r2, kb2)
    m4 = square(m2)
    t = times_one_plus(each(lambda x: eye + x, m1), m2)
    a_rk = each(lambda x: _bf(jnp.where(before_eq, x[:, :LANES], 0.0)), g_r)
    a_rb = each(lambda x: _bf(jnp.where(before_eq, x[:, LANES:], 0.0)), g_r)
    m8 = square(m4)
    t = times_one_plus(t, m4)
    w0 = each(lambda x, y: _bf(_dot(x, y)), a_kk, v2)
    t = times_one_plus(t, m8)
    t = merge(t, a_kb, blk32, blk16)
    tinv = each(_bf, merge(t, a_kb, same_head, blk32))
    khu = each(lambda x, y, w: _bf(_dot(x, jnp.concatenate([y, w], axis=1))), tinv, kap2, w0)
    kh = each(lambda x: x[:, :LANES], khu)
    u0 = each(lambda x: x[:, LANES:], khu)
    vu = each(lambda x, y: jnp.concatenate([x, y], axis=0), v2, u0)
    rh = each(lambda x, y, w: _bf(x - _dot(y, w)), r2f, a_rb, kh)
    y0 = each(lambda x, y, w: _dot(jnp.concatenate([x, -y], axis=1), w), a_rk, a_rb, vu)
    jj = each(lambda x, y, w: _dot_tn(x, jnp.concatenate([y, -w], axis=0)), vu, kc2, bc2)
    gg = each(lambda x, y: _bf(-_dot_tn(x, y)), kh, bc2)
    for c in range(nchunks):
        rh_ref[c] = rh[c]
        y0_ref[c] = y0[c]
        j_ref[c] = jj[c]
        g_ref[c] = gg[c]
        et_ref[c] = jnp.broadcast_to(jnp.exp(b_tot[c]), (8, LANES))

    S = st_ref[...]
    for c in (range(nchunks - 1, -1, -1) if rev else range(nchunks)):
        Sb = _bf(S)
        y_ref[pl.ds(c * C, C), :] = unstack(_dot_nt(rh_ref[c], Sb) + y0_ref[c])
        S = S * et_ref[c, 0:1, :] + _dot(Sb, g_ref[c]) + j_ref[c]
    st_ref[...] = S

    @pl.when(pl.program_id(1) == pl.num_programs(1) - 1)
    def _():
        s_ref[...] = S


def wkv_scan(r, k, v, lw, a, par, s0, d):
    T, D = r.shape
    HP = D // LANES
    Tt = _tile(T, 1024)
    NT = T // Tt
    nck = Tt // CHUNK
    rev = d == 1

    def tt(t):
        return NT - 1 - t if rev else t

    tok = pl.BlockSpec((Tt, LANES), lambda h, t: (tt(t), h))
    tok_d = pl.BlockSpec((None, Tt, LANES), lambda h, t: (d, tt(t), h))
    st = pl.BlockSpec((None, LANES, LANES), lambda h, t: (h, 0, 0))
    return pl.pallas_call(
        functools.partial(_wkv_kernel, rev=rev),
        grid=(HP, NT),
        in_specs=[tok, tok, tok, tok_d, tok_d, pl.BlockSpec((8, LANES), lambda h, t: (0, h)), st],
        out_specs=[tok, tok, st],
        out_shape=[jax.ShapeDtypeStruct((T, D), F32), jax.ShapeDtypeStruct((T, D), F32),
                   jax.ShapeDtypeStruct((HP, LANES, LANES), F32)],
        scratch_shapes=[
            pltpu.VMEM((LANES, LANES), F32),
            pltpu.VMEM((nck, LANES, LANES), BF16),
            pltpu.VMEM((nck, LANES, LANES), F32),
            pltpu.VMEM((nck, LANES, LANES), BF16),
            pltpu.VMEM((nck, LANES, LANES), F32),
            pltpu.VMEM((nck, 8, LANES), F32),
        ],
        compiler_params=_cparams("parallel", "arbitrary"),
        name="wkv_scan",
    )(r, k, v, lw, a, par, s0)


def _rwkv_combine_kernel(y0_ref, y1_ref, z0_ref, z1_ref, g_ref, ln_ref, o_ref):
    D = o_ref.shape[1]
    row = lax.broadcasted_iota(jnp.int32, (LANES, LANES), 0)
    col = lax.broadcasted_iota(jnp.int32, (LANES, LANES), 1)
    mean_m = jnp.where((row < WKV_HEAD) == (col < WKV_HEAD), 1.0 / WKV_HEAD, 0.0).astype(BF16)
    for c in range(D // LANES):
        cs = slice(c * LANES, (c + 1) * LANES)
        y = y0_ref[:, cs] + y1_ref[:, cs]
        yc = y - _exact_right(y, mean_m, 2)
        var = _exact_right(yc * yc, mean_m, 2)
        out = yc * lax.rsqrt(var + GN_EPS) * ln_ref[0:1, cs] + ln_ref[1:2, cs] + (z0_ref[:, cs] + z1_ref[:, cs])
        o_ref[:, cs] = _bf(out * g_ref[:, cs])


def rwkv_combine(y0, y1, z0, z1, g, ln):
    M, D = g.shape
    tm = _tile(M, 256)
    tok = pl.BlockSpec((tm, D), lambda i: (i, 0))
    return pl.pallas_call(
        _rwkv_combine_kernel,
        grid=(M // tm,),
        in_specs=[tok] * 5 + [pl.BlockSpec((8, D), lambda i: (0, 0))],
        out_specs=tok,
        out_shape=jax.ShapeDtypeStruct((M, D), BF16),
        compiler_params=_cparams("parallel"),
        name="rwkv_combine",
    )(y0, y1, z0, z1, g, ln)


def _gla_kernel(q_ref, i_ref, f_ref, lbl_ref, s0_ref, o_ref, s_ref, st_ref, qe_ref, j_ref, et_ref, *, rev, layer):
    C = CHUNK
    Tt = q_ref.shape[0]
    nchunks = Tt // C
    nsub = C // SUB

    @pl.when(pl.program_id(1) == 0)
    def _():
        st_ref[...] = s0_ref[...]

    logits = lbl_ref[...]
    e = jnp.exp(logits - jnp.max(logits, axis=0, keepdims=True))
    p = e / jnp.sum(e, axis=0, keepdims=True)
    lb = jnp.zeros((1, LANES), F32)
    for l in range(1, layer + 1):
        lb = lb + p[l:l + 1]
    log_lb = jnp.log(lb)
    log_1m = jnp.log1p(-lb)

    r64 = lax.broadcasted_iota(jnp.int32, (C, C), 0)
    c64 = lax.broadcasted_iota(jnp.int32, (C, C), 1)
    cum_m = jnp.where((c64 >= r64) if rev else (c64 <= r64), 1.0, 0.0).astype(BF16)
    rows = lax.broadcasted_iota(jnp.int32, (C, 1), 0)
    lane_s = lax.broadcasted_iota(jnp.int32, (SUB, C), 1)
    row_s = lax.broadcasted_iota(jnp.int32, (SUB, 1), 0)

    def each(f, *lists):
        return [f(*xs) for xs in zip(*lists)]

    sls = [pl.ds(c * C, C) for c in range(nchunks)]
    q = [_silu(q_ref[sl, :]) for sl in sls]
    v = [i_ref[sl, :] for sl in sls]
    f = [f_ref[sl, :] for sl in sls]
    kg = each(lambda x: (1.0 - lb) * _sigmoid(-x), f)
    x2 = each(lambda x: log_1m - _softplus(-x), f)
    g = each(lambda x: jnp.maximum(log_lb, x) + jnp.log1p(jnp.exp(-jnp.abs(log_lb - x))), x2)
    b = each(lambda x: _exact_left(cum_m, x, 3), g)
    b_tot = each(lambda x: x[0:1] if rev else x[C - 1:C], b)
    vb = each(_bf, v)
    att_rows = [[] for _ in range(nchunks)]
    for I in range(nsub):
        lo = I * SUB
        qI = each(lambda x: x[lo:lo + SUB], q)
        bI = each(lambda x: x[lo:lo + SUB], b)
        first = (I == nsub - 1) if rev else (I == 0)
        if first:
            att = [jnp.zeros((SUB, C), F32)] * nchunks
        else:
            ref = each(lambda x: x[lo + SUB:lo + SUB + 1] if rev else x[lo - 1:lo], b)
            earlier = (rows >= lo + SUB) if rev else (rows < lo)
            kt = each(lambda x, y, w: _bf(jnp.where(earlier, x * jnp.exp(w - y), 0.0)), kg, b, ref)
            qt = each(lambda x, y, w: _bf(x * jnp.exp(y - w)), qI, bI, ref)
            att = each(_dot_nt, qt, kt)
        for j in range(SUB):
            s = lo + j
            place = jnp.logical_and(lane_s == s, (row_s <= j) if rev else (row_s >= j))
            pj = each(lambda x, y, w, u: x * y[s:s + 1] * jnp.exp(w - u[s:s + 1]), qI, kg, bI, b)
            att = each(lambda x, y: x + jnp.where(place, jnp.sum(y, axis=-1, keepdims=True), 0.0), att, pj)
        for c in range(nchunks):
            att_rows[c].append(att[c])
    o_in = each(lambda x, y: _dot(_bf(jnp.concatenate(x, axis=0)), y), att_rows, vb)
    for c in range(nchunks):
        o_ref[sls[c], :] = o_in[c]
        qe_ref[c] = _bf(q[c] * jnp.exp(b[c]))
        j_ref[c] = _dot_tn(vb[c], _bf(kg[c] * jnp.exp(b_tot[c] - b[c])))
        et_ref[c] = jnp.broadcast_to(jnp.exp(b_tot[c]), (8, LANES))

    Z = st_ref[...]
    for c in (range(nchunks - 1, -1, -1) if rev else range(nchunks)):
        o_ref[sls[c], :] += _dot_nt(qe_ref[c], _bf(Z))
        Z = Z * et_ref[c, 0:1, :] + j_ref[c]
    st_ref[...] = Z

    @pl.when(pl.program_id(1) == pl.num_programs(1) - 1)
    def _():
        s_ref[...] = Z


def gla_scan(raw, lb_logits, s0, d, layer):
    T = raw.shape[0]
    D = raw.shape[1] // 5
    H = D // LANES
    Tt = _tile(T, 1024)
    NT = T // Tt
    nck = Tt // CHUNK
    rev = d == 1
    L = lb_logits.shape[0]

    def tt(t):
        return NT - 1 - t if rev else t

    st = pl.BlockSpec((None, LANES, LANES), lambda h, t: (h, 0, 0))
    return pl.pallas_call(
        functools.partial(_gla_kernel, rev=rev, layer=layer),
        grid=(H, NT),
        in_specs=[
            pl.BlockSpec((Tt, LANES), lambda h, t: (tt(t), h)),
            pl.BlockSpec((Tt, LANES), lambda h, t: (tt(t), H + h)),
            pl.BlockSpec((Tt, LANES), lambda h, t: (tt(t), (2 + d) * H + h)),
            pl.BlockSpec((L, LANES), lambda h, t: (0, h)),
            st,
        ],
        out_specs=[pl.BlockSpec((Tt, LANES), lambda h, t: (tt(t), h)), st],
        out_shape=[jax.ShapeDtypeStruct((T, D), F32), jax.ShapeDtypeStruct((H, LANES, LANES), F32)],
        scratch_shapes=[
            pltpu.VMEM((LANES, LANES), F32),
            pltpu.VMEM((nck, CHUNK, LANES), BF16),
            pltpu.VMEM((nck, LANES, LANES), F32),
            pltpu.VMEM((nck, 8, LANES), F32),
        ],
        compiler_params=_cparams("parallel", "arbitrary"),
        name="gla_scan",
    )(raw, raw, raw, lb_logits, s0)


def _hgrn_combine_kernel(o0_ref, o1_ref, g_ref, gn_ref, o_ref):
    D = o_ref.shape[1]
    for c in range(D // LANES):
        cs = slice(c * LANES, (c + 1) * LANES)
        o = o0_ref[:, cs] + o1_ref[:, cs]
        o = o * lax.rsqrt(jnp.mean(o * o, axis=-1, keepdims=True) + EPS) * gn_ref[0:1, cs]
        o_ref[:, cs] = _bf(o * _silu(g_ref[:, cs]))


def hgrn_combine(o0, o1, raw, gn):
    M, D = o0.shape
    tm = _tile(M, 256)
    tok = pl.BlockSpec((tm, D), lambda i: (i, 0))
    return pl.pallas_call(
        _hgrn_combine_kernel,
        grid=(M // tm,),
        in_specs=[tok, tok, pl.BlockSpec((tm, D), lambda i: (i, 4)), pl.BlockSpec((1, D), lambda i: (0, 0))],
        out_specs=tok,
        out_shape=jax.ShapeDtypeStruct((M, D), BF16),
        compiler_params=_cparams("parallel"),
        name="hgrn_combine",
    )(o0, o1, raw, gn)


def _rows8(*vecs):
    D = vecs[0].shape[-1]
    rows = [v.reshape(1, D) for v in vecs]
    rows.append(jnp.zeros((8 - len(rows), D), F32))
    return jnp.concatenate(rows, axis=0)


def kernel(x, c, ctx, c_ctx, ada_down, ada_up, ada_b, norm_g, ffn_w13, ffn_w2, final_g, lru_w_in, lru_conv_w, lru_conv_b, lru_gate_a_w, lru_gate_a_b, lru_gate_x_w, lru_gate_x_b, lru_lam, lru_w_out, rwkv_mu, rwkv_w_r, rwkv_w_k, rwkv_w_v, rwkv_w_o, rwkv_w0, rwkv_w1, rwkv_w2, rwkv_a0, rwkv_a1, rwkv_a2, rwkv_g1, rwkv_g2, rwkv_k_k, rwkv_k_a, rwkv_r_k, rwkv_ln_w, rwkv_ln_b, hgrn_w_in, hgrn_lb_logits, hgrn_gn_g, hgrn_w_out):
    B, T, D = x.shape
    assert B == 1, "one sequence per call"
    depth = ada_down.shape[0]
    xl, xc = x[0], ctx[0]

    mods = ada_all_layers(_rows8(c[0], c_ctx), ada_down, ada_up, ada_b)
    mods = mods[:, :2].reshape(depth, 2, N_MOD, D)

    lru_wa, lru_wx = _bf(lru_gate_a_w), _bf(lru_gate_x_w)
    lw1, lw2, la1, la2 = _bf(rwkv_w1), _bf(rwkv_w2), _bf(rwkv_a1), _bf(rwkv_a2)
    lg1, lg2 = _bf(rwkv_g1), _bf(rwkv_g2)

    for i in range(depth):
        need_ctx = i < depth - 1
        kind, j = i % 3, i // 3
        ml, mc = mods[i, 0], mods[i, 1]

        def ffn_mod(m, k, which):
            return _rows8(norm_g[i, 2 * which], m[k], m[k + 1], m[k + 2], final_g)

        xc, *wb = half_ffn(xc, ffn_mod(mc, 0, 0), (ffn_w13, ffn_w2), (i, 0))
        xl = half_ffn(xl, ffn_mod(ml, 0, 0), wb)
        mod_l = _rows8(norm_g[i, 1], ml[3], ml[4])
        mod_c = _rows8(norm_g[i, 1], mc[3], mc[4])
        gate_l, gate_c = _rows8(ml[5]), _rows8(mc[5])

        if kind == 0:
            p_c, w_in = proj(xc, lru_w_in, (j,), mod=mod_c, gelu_cols=D, emit_weight=True)
            p_l = proj(xl, w_in, mod=mod_l, gelu_cols=D)
            hs = lru_scan(p_l, p_c, lru_conv_w, lru_conv_b, lru_wa, lru_wx, lru_gate_a_b, lru_gate_x_b, lru_lam,
                          j, need_ctx)
            a_l = lru_combine(p_l, hs[0])
            a_c = lru_combine(p_c, hs[1]) if need_ctx else None
            w_last = lru_w_out
        elif kind == 1:
            par = _rows8(rwkv_k_k[j], rwkv_k_a[j], rwkv_r_k[j].reshape(D))
            ln = _rows8(rwkv_ln_w[j], rwkv_ln_b[j])
            outs = []
            state = [jnp.zeros((D // LANES, LANES, LANES), F32)] * 2
            for xs, mod, is_grid in ((xc, mod_c, False), (xl, mod_l, True)):
                xr, xw, xk, xv, xa, xg = rwkv_shiftmix(xs, mod, rwkv_mu, j, is_grid)
                if not is_grid:
                    r, w_r = proj(xr, rwkv_w_r, (j,), emit_weight=True)
                    k, w_k = proj(xk, rwkv_w_k, (j,), emit_weight=True)
                    v, w_v = proj(xv, rwkv_w_v, (j,), emit_weight=True)
                else:
                    r, k, v = proj(xr, w_r), proj(xk, w_k), proj(xv, w_v)
                lw, a, g = rwkv_lora(xw, xa, xg, lw1, lw2, rwkv_w0, la1, la2, rwkv_a0, lg1, lg2, j)
                ys, zs = [], []
                for d in (0, 1):
                    y, z, state[d] = wkv_scan(r, k, v, lw, a, par, state[d], d)
                    ys.append(y)
                    zs.append(z)
                outs.append(rwkv_combine(ys[0], ys[1], zs[0], zs[1], g, ln))
            a_c, a_l = outs
            w_last = rwkv_w_o
        else:
            outs = []
            state = [jnp.zeros((D // LANES, LANES, LANES), F32)] * 2
            for xs, mod, is_ctx in ((xc, mod_c, True), (xl, mod_l, False)):
                if is_ctx:
                    raw, w_in = proj(xs, hgrn_w_in, (j,), mod=mod, emit_weight=True)
                else:
                    raw = proj(xs, w_in, mod=mod)
                os_ = []
                for d in (0, 1):
                    o, state[d] = gla_scan(raw, hgrn_lb_logits, state[d], d, i)
                    os_.append(o)
                outs.append(hgrn_combine(os_[0], os_[1], raw, hgrn_gn_g[j].reshape(1, D)))
            a_c, a_l = outs
            w_last = hgrn_w_out

        if need_ctx:
            xc, w_out = proj(a_c, w_last, (j,), res=xc, gate=gate_c, emit_weight=True)
            xc, *wb = half_ffn(xc, ffn_mod(mc, 6, 1), (ffn_w13, ffn_w2), (i, 1))
        else:
            w_out = _bf(w_last[j])
            wb = round_ffn_weights(ffn_w13, ffn_w2, (i, 1))
        xl = proj(a_l, w_out, res=xl, gate=gate_l)
        xl = half_ffn(xl, ffn_mod(ml, 6, 1), wb, final_norm=not need_ctx)

    return xl[None]
```

```python
import functools
import math

import jax
import jax.numpy as jnp
from jax import lax
from jax.experimental import pallas as pl
from jax.experimental.pallas import tpu as pltpu

F32 = jnp.float32
BF16 = jnp.bfloat16

EPS = 1e-6
GN_EPS = 64e-5
LRU_C = 8.0
GRID_W = 64
N_MOD = 9
WKV_HEAD = 64
GLA_HEAD = 128
CHUNK = 64
SUB = 8
LANES = 128
VMEM_LIMIT_BYTES = 60 * 1024 * 1024


def _cparams(*sem):
    return pltpu.CompilerParams(dimension_semantics=sem, vmem_limit_bytes=VMEM_LIMIT_BYTES)


def _tile(n, pref):
    if n <= pref:
        return n
    for t in range(pref, 7, -1):
        if n % t == 0 and t % 8 == 0:
            return t
    return n


def _bf(x):
    return x.astype(BF16)


def _dot(a, b):
    return jnp.dot(a, b, preferred_element_type=F32)


def _dot_nt(a, b):
    return lax.dot_general(a, b, (((1,), (1,)), ((), ())), preferred_element_type=F32)


def _dot_tn(a, b):
    return lax.dot_general(a, b, (((0,), (0,)), ((), ())), preferred_element_type=F32)


def _split_terms(x, terms):
    out, rem = [], x
    for _ in range(terms):
        p = _bf(rem)
        out.append(p)
        rem = rem - p.astype(F32)
    return out


def _exact_left(m_bf, x, terms):
    acc = None
    for p in _split_terms(x, terms):
        d = _dot(m_bf, p)
        acc = d if acc is None else acc + d
    return acc


def _exact_right(x, m_bf, terms):
    acc = None
    for p in _split_terms(x, terms):
        d = _dot(p, m_bf)
        acc = d if acc is None else acc + d
    return acc


def _sigmoid(x):
    return 0.5 * jnp.tanh(0.5 * x) + 0.5


def _silu(x):
    return x * _sigmoid(x)


def _gelu_tanh(x):
    return 0.5 * x * (1.0 + jnp.tanh(math.sqrt(2.0 / math.pi) * (x + 0.044715 * (x * x * x))))


def _softplus(x):
    return jnp.maximum(x, 0.0) + jnp.log1p(jnp.exp(-jnp.abs(x)))


def _row_rsqrt_ms(x):
    D = x.shape[-1]
    if D % LANES == 0:
        acc = x[:, 0:LANES] * x[:, 0:LANES]
        for c in range(LANES, D, LANES):
            acc = acc + x[:, c:c + LANES] * x[:, c:c + LANES]
        ms = jnp.sum(acc, axis=-1, keepdims=True) * (1.0 / D)
    else:
        ms = jnp.mean(x * x, axis=-1, keepdims=True)
    return lax.rsqrt(ms + EPS)


def _normmod(x, g, shift, scale):
    return (x * _row_rsqrt_ms(x)) * (g * (1.0 + scale)) + shift


def _ada_kernel(cc_ref, down_ref, up_ref, b_ref, o_ref, t_ref):
    hi = lax.Precision.HIGHEST

    @pl.when(pl.program_id(1) == 0)
    def _():
        t_ref[...] = jnp.dot(_silu(cc_ref[...]), down_ref[...], precision=hi, preferred_element_type=F32)

    o_ref[...] = jnp.dot(t_ref[...], up_ref[...], precision=hi, preferred_element_type=F32) + b_ref[...]


def ada_all_layers(cc, down, up, bias):
    L, D, R = down.shape
    N = up.shape[2]
    tn = _tile(N, 4096)
    return pl.pallas_call(
        _ada_kernel,
        grid=(L, N // tn),
        in_specs=[
            pl.BlockSpec((8, D), lambda l, j: (0, 0)),
            pl.BlockSpec((None, D, R), lambda l, j: (l, 0, 0)),
            pl.BlockSpec((None, R, tn), lambda l, j: (l, 0, j)),
            pl.BlockSpec((None, 1, tn), lambda l, j: (l, 0, j)),
        ],
        out_specs=pl.BlockSpec((None, 8, tn), lambda l, j: (l, 0, j)),
        out_shape=jax.ShapeDtypeStruct((L, 8, N), F32),
        scratch_shapes=[pltpu.VMEM((8, R), F32)],
        compiler_params=_cparams("parallel", "arbitrary"),
        name="ada",
    )(cc, down, up, bias.reshape(L, 1, N))


def _ffn_kernel(x_ref, mod_ref, w1_ref, w3_ref, w2_ref, o_ref, *rest, emit_weights, final_norm):
    h_ref, rs_ref = rest[-2:]
    f = pl.program_id(1)
    tm, D = o_ref.shape
    strip = math.gcd(tm, 16)

    def for_strips(body):
        def step(s, carry):
            body(pl.ds(pl.multiple_of(s * strip, strip), strip))
            return carry
        lax.fori_loop(0, tm // strip, step, 0, unroll=math.gcd(tm // strip, 4))

    @pl.when(f == 0)
    def _():
        def scale_rows(rows):
            rs_ref[rows, :] = jnp.broadcast_to(_row_rsqrt_ms(x_ref[rows, :]), (strip, LANES))
        for_strips(scale_rows)
        gain = mod_ref[0:1] * (1.0 + mod_ref[2:3])

        def normalise(rows):
            h_ref[rows, :] = _bf(x_ref[rows, :] * rs_ref[rows, 0:1] * gain + mod_ref[1:2])
            o_ref[rows, :] = jnp.zeros((strip, D), F32)
        for_strips(normalise)

    w1, w3 = _bf(w1_ref[...]), _bf(w3_ref[...])
    if emit_weights:
        rest[0][...] = w1
        rest[1][...] = w3
    h = h_ref[...]
    act = _bf(_silu(_dot(h, w1)) * _dot(h, w3))
    slab = min(D, 512)
    for c0 in range(0, D, slab):
        w2 = _bf(w2_ref[:, c0:c0 + slab])
        if emit_weights:
            rest[2][:, c0:c0 + slab] = w2
        o_ref[:, c0:c0 + slab] += _dot(act, w2)

    @pl.when(f == pl.num_programs(1) - 1)
    def _():
        def body(rows):
            y = x_ref[rows, :] + 0.5 * mod_ref[3:4] * o_ref[rows, :]
            if final_norm:
                y = y * _row_rsqrt_ms(y) * mod_ref[4:5]
            o_ref[rows, :] = y
        for_strips(body)


def half_ffn(x, mod, weights, index=None, final_norm=False):
    M, D = x.shape
    emit = index is not None
    F = weights[-1].shape[-2]
    tm = _tile(M, 512)
    tf = _tile(F, 256 if emit else 512)
    nf = F // tf
    if emit:
        assert M == tm
        w13, w2 = weights
        lead = tuple(index)
        w_specs = [
            pl.BlockSpec((None, None, D, tf), lambda i, f: lead + (0, f)),
            pl.BlockSpec((None, None, D, tf), lambda i, f: lead + (0, nf + f)),
            pl.BlockSpec((None, None, tf, D), lambda i, f: lead + (f, 0)),
        ]
        w_args = (w13, w13, w2)
    else:
        w_specs = [
            pl.BlockSpec((D, tf), lambda i, f: (0, f)),
            pl.BlockSpec((D, tf), lambda i, f: (0, f)),
            pl.BlockSpec((tf, D), lambda i, f: (f, 0)),
        ]
        w_args = tuple(weights)
    out_specs = [pl.BlockSpec((tm, D), lambda i, f: (i, 0))]
    out_shape = [jax.ShapeDtypeStruct((M, D), F32)]
    if emit:
        out_specs += [
            pl.BlockSpec((D, tf), lambda i, f: (0, f)),
            pl.BlockSpec((D, tf), lambda i, f: (0, f)),
            pl.BlockSpec((tf, D), lambda i, f: (f, 0)),
        ]
        out_shape += [jax.ShapeDtypeStruct((D, F), BF16), jax.ShapeDtypeStruct((D, F), BF16),
                      jax.ShapeDtypeStruct((F, D), BF16)]
    out = pl.pallas_call(
        functools.partial(_ffn_kernel, emit_weights=emit, final_norm=final_norm),
        grid=(M // tm, nf),
        in_specs=[
            pl.BlockSpec((tm, D), lambda i, f: (i, 0), pipeline_mode=pl.Buffered(1)),
            pl.BlockSpec((8, D), lambda i, f: (0, 0)),
        ] + w_specs,
        out_specs=out_specs,
        out_shape=out_shape,
        scratch_shapes=[pltpu.VMEM((tm, D), BF16), pltpu.VMEM((tm, LANES), F32)],
        compiler_params=_cparams("parallel", "arbitrary"),
        name="half_ffn",
    )(x, mod, *w_args)
    return out if emit else out[0]


def _round_weights_kernel(w1_ref, w3_ref, w2_ref, w1b_ref, w3b_ref, w2b_ref):
    w1b_ref[...] = _bf(w1_ref[...])
    w3b_ref[...] = _bf(w3_ref[...])
    w2b_ref[...] = _bf(w2_ref[...])


def round_ffn_weights(w13, w2, index):
    D, F = w2.shape[-1], w2.shape[-2]
    tf = _tile(F, 256)
    nf = F // tf
    lead = tuple(index)
    return pl.pallas_call(
        _round_weights_kernel,
        grid=(nf,),
        in_specs=[
            pl.BlockSpec((None, None, D, tf), lambda f: lead + (0, f)),
            pl.BlockSpec((None, None, D, tf), lambda f: lead + (0, nf + f)),
            pl.BlockSpec((None, None, tf, D), lambda f: lead + (f, 0)),
        ],
        out_specs=[
            pl.BlockSpec((D, tf), lambda f: (0, f)),
            pl.BlockSpec((D, tf), lambda f: (0, f)),
            pl.BlockSpec((tf, D), lambda f: (f, 0)),
        ],
        out_shape=[jax.ShapeDtypeStruct((D, F), BF16), jax.ShapeDtypeStruct((D, F), BF16),
                   jax.ShapeDtypeStruct((F, D), BF16)],
        compiler_params=_cparams("parallel"),
        name="round_ffn_weights",
    )(w13, w13, w2)


def _proj_kernel(*refs, norm, n_gelu, residual, emit_weight):
    it = iter(refs)
    a_ref = next(it)
    mod_ref = next(it) if norm else None
    w_ref = next(it)
    res_ref = next(it) if residual else None
    gate_ref = next(it) if residual else None
    o_ref = next(it)
    wb_ref = next(it) if emit_weight else None
    h_ref = next(it) if norm else None
    j = pl.program_id(1)

    if norm:
        @pl.when(j == 0)
        def _():
            h_ref[...] = _bf(_normmod(a_ref[...], mod_ref[0:1], mod_ref[1:2], mod_ref[2:3]))
        lhs = h_ref[...]
    else:
        lhs = a_ref[...]
    w = _bf(w_ref[...])
    if emit_weight:
        wb_ref[...] = w
    acc = _dot(lhs, w)
    if residual:
        o_ref[...] = res_ref[...] + gate_ref[0:1] * acc
    elif n_gelu:
        @pl.when(j < n_gelu)
        def _():
            o_ref[...] = _gelu_tanh(acc)

        @pl.when(j >= n_gelu)
        def _():
            o_ref[...] = acc
    else:
        o_ref[...] = acc


def proj(a, w, w_index=(), *, mod=None, gelu_cols=0, res=None, gate=None, emit_weight=False):
    M, K = a.shape
    N = w.shape[-1]
    tm = _tile(M, 512)
    tn = _tile(math.gcd(N, gelu_cols) if gelu_cols else N, 512 if emit_weight else 1024)
    norm = mod is not None
    residual = res is not None
    assert gelu_cols % tn == 0 and (M == tm or not emit_weight)
    lead = tuple(w_index)
    in_specs = [pl.BlockSpec((tm, K), lambda i, j: (i, 0))]
    args = [a]
    if norm:
        in_specs.append(pl.BlockSpec((8, K), lambda i, j: (0, 0)))
        args.append(mod)
    in_specs.append(pl.BlockSpec((None,) * len(lead) + (K, tn), lambda i, j: lead + (0, j)))
    args.append(w)
    if residual:
        in_specs.append(pl.BlockSpec((tm, tn), lambda i, j: (i, j)))
        in_specs.append(pl.BlockSpec((8, tn), lambda i, j: (0, j)))
        args += [res, gate]
    out_specs = [pl.BlockSpec((tm, tn), lambda i, j: (i, j))]
    out_shape = [jax.ShapeDtypeStruct((M, N), F32)]
    if emit_weight:
        out_specs.append(pl.BlockSpec((K, tn), lambda i, j: (0, j)))
        out_shape.append(jax.ShapeDtypeStruct((K, N), BF16))
    out = pl.pallas_call(
        functools.partial(_proj_kernel, norm=norm, n_gelu=gelu_cols // tn, residual=residual,
                          emit_weight=emit_weight),
        grid=(M // tm, N // tn),
        in_specs=in_specs,
        out_specs=out_specs,
        out_shape=out_shape,
        scratch_shapes=[pltpu.VMEM((tm, K), BF16)] if norm else [],
        compiler_params=_cparams("parallel", "arbitrary"),
        name="proj",
    )(*args)
    return out if emit_weight else out[0]


def _lru_kernel(ul_ref, uc_ref, cw_ref, cb_ref, wa_ref, wx_ref, ba_ref, bx_ref, lam_ref, *refs, R, Rc, need_ctx):
    hl_ref = refs[0]
    hc_ref = refs[1] if need_ctx else None
    v_ref, hs_ref = refs[-2:]
    T, W = ul_ref.shape
    NL = W // LANES
    Tc = uc_ref.shape[0]
    cw = cw_ref[...]
    cb = cb_ref[...]

    def conv_chunk(u_ref, t0, n_rows, R):
        main = u_ref[pl.ds(t0, R), :]
        p0 = pl.multiple_of(jnp.maximum(t0 - 8, 0), 8)
        n0 = pl.multiple_of(jnp.minimum(t0 + R, n_rows - 8), 8)
        prev = jnp.where(t0 > 0, u_ref[pl.ds(p0, 8), :], 0.0)
        nxt = jnp.where(t0 + R < n_rows, u_ref[pl.ds(n0, 8), :], 0.0)
        ext = jnp.concatenate([prev, main, nxt], axis=0)
        n = R + 16
        out = cb + ext[8:8 + R] * cw[2:3]
        out = out + pltpu.roll(ext, 2, 0)[8:8 + R] * cw[0:1]
        out = out + pltpu.roll(ext, 1, 0)[8:8 + R] * cw[1:2]
        out = out + pltpu.roll(ext, n - 1, 0)[8:8 + R] * cw[3:4]
        return out

    def scan_chunk(d, carry, rev, G):
        vp = jnp.concatenate(
            [jnp.concatenate([v_ref[j, pl.ds(g, 8, stride=G), :] for j in range(NL)], axis=1) for g in range(G)],
            axis=0)
        vb = _bf(vp)
        r = _sigmoid(_dot(vb, wa_ref[d]) + ba_ref[d:d + 1])
        gi = _sigmoid(_dot(vb, wx_ref[d]) + bx_ref[d:d + 1])
        log_a = -LRU_C * r * _softplus(-lam_ref[d:d + 1])
        a = jnp.exp(log_a)
        b = jnp.sqrt(-jnp.tanh(log_a) * (a * a + 1.0)) * (gi * vp)
        h = jnp.zeros((8, W), F32)
        p = jnp.ones((8, W), F32)
        hs, ps = [None] * G, [None] * G
        for g in (range(G - 1, -1, -1) if rev else range(G)):
            ag = a[8 * g:8 * g + 8]
            h = ag * h + b[8 * g:8 * g + 8]
            p = ag * p
            hs[g], ps[g] = h, p
        starts = [None] * 8
        for s in (range(7, -1, -1) if rev else range(8)):
            starts[s] = carry
            carry = h[s:s + 1] + p[s:s + 1] * carry
        start = jnp.concatenate(starts, axis=0)
        return [hs[g] + ps[g] * start for g in range(G)], carry

    def run(u_ref, o_ref, n_rows, d, carry, R):
        rev = d == 1
        nchunks = n_rows // R
        G = R // 8

        def body(i, carry):
            c = nchunks - 1 - i if rev else i
            t0 = pl.multiple_of(c * R, R)
            v = conv_chunk(u_ref, t0, n_rows, R)
            for j in range(NL):
                v_ref[j, 0:R, :] = v[:, j * LANES:(j + 1) * LANES]
            hs, carry = scan_chunk(d, carry, rev, G)
            if o_ref is not None:
                for g in range(G):
                    for j in range(NL):
                        hs_ref[j, pl.ds(g, 8, stride=G), :] = hs[g][:, j * LANES:(j + 1) * LANES]
                h = jnp.concatenate([hs_ref[j, 0:R, :] for j in range(NL)], axis=1)
                if d == 0:
                    o_ref[pl.ds(t0, R), :] = h
                else:
                    o_ref[pl.ds(t0, R), :] += h
            return carry

        return lax.fori_loop(0, nchunks, body, carry)

    for d in (0, 1):
        carry = run(uc_ref, hc_ref, Tc, d, jnp.zeros((1, W), F32), Rc)
        run(ul_ref, hl_ref, T, d, carry, R)


def lru_scan(pl_out, pc_out, conv_w, conv_b, wa, wx, ba, bx, lam, li, need_ctx):
    T = pl_out.shape[0]
    Tc = pc_out.shape[0]
    D = pl_out.shape[1] // 2
    NH, W = wa.shape[2], wa.shape[3]
    assert W % LANES == 0 and NH * W == D
    R, Rc = _tile(T, 512), _tile(Tc, 256)
    col = D // W
    vec2 = pl.BlockSpec((None, 2, W), lambda h: (li, 0, h))
    out_shape = [jax.ShapeDtypeStruct((T, D), F32)]
    out_specs = [pl.BlockSpec((T, W), lambda h: (0, h))]
    if need_ctx:
        out_shape.append(jax.ShapeDtypeStruct((Tc, D), F32))
        out_specs.append(pl.BlockSpec((Tc, W), lambda h: (0, h)))
    return pl.pallas_call(
        functools.partial(_lru_kernel, R=R, Rc=Rc, need_ctx=need_ctx),
        grid=(NH,),
        in_specs=[
            pl.BlockSpec((T, W), lambda h: (0, col + h)),
            pl.BlockSpec((Tc, W), lambda h: (0, col + h)),
            pl.BlockSpec((None, 4, W), lambda h: (li, 0, h)),
            pl.BlockSpec((None, 1, W), lambda h: (li, 0, h)),
            pl.BlockSpec((None, 2, None, W, W), lambda h: (li, 0, h, 0, 0)),
            pl.BlockSpec((None, 2, None, W, W), lambda h: (li, 0, h, 0, 0)),
            vec2, vec2, vec2,
        ],
        out_specs=out_specs,
        out_shape=out_shape,
        scratch_shapes=[pltpu.VMEM((W // LANES, max(R, Rc), LANES), F32)] * 2,
        compiler_params=_cparams("parallel"),
        name="lru_scan",
    )(pl_out, pc_out, conv_w, conv_b.reshape(conv_b.shape[0], 1, D), wa, wx, ba, bx, lam)


def _mul_kernel(a_ref, b_ref, o_ref):
    o_ref[...] = _bf(a_ref[...] * b_ref[...])


def lru_combine(p_out, hs):
    M, D = hs.shape
    tm = _tile(M, 256)
    return pl.pallas_call(
        _mul_kernel,
        grid=(M // tm,),
        in_specs=[pl.BlockSpec((tm, D), lambda i: (i, 0)), pl.BlockSpec((tm, D), lambda i: (i, 0))],
        out_specs=pl.BlockSpec((tm, D), lambda i: (i, 0)),
        out_shape=jax.ShapeDtypeStruct((M, D), BF16),
        compiler_params=_cparams("parallel"),
        name="lru_combine",
    )(p_out, hs)


def _mix_write(h, shifted, mu_ref, o_refs, cols):
    xx = shifted - h
    for n, o_ref in enumerate(o_refs):
        o_ref[:, cols] = _bf(h + xx * mu_ref[n:n + 1, cols])


def _mix_lat_kernel(xm_ref, xp_ref, xn_ref, mod_ref, mu_ref, *o_refs, T):
    tm, D = xm_ref.shape
    q = D // 4
    i = pl.program_id(0)
    g, sh, sc = mod_ref[0:1], mod_ref[1:2], mod_ref[2:3]
    hm = _normmod(xm_ref[...], g, sh, sc)
    hp = _normmod(xp_ref[...], g, sh, sc)
    hn = _normmod(xn_ref[...], g, sh, sc)
    row = lax.broadcasted_iota(jnp.int32, (tm, 1), 0)
    t = i * tm + row
    colpos = row & (GRID_W - 1)
    s0 = slice(0, q)
    left = jnp.where(colpos > 0, pltpu.roll(hm[:, s0], 1, 0), 0.0)
    _mix_write(hm[:, s0], left, mu_ref, o_refs, s0)
    s1 = slice(q, 2 * q)
    right = jnp.where(colpos < GRID_W - 1, pltpu.roll(hm[:, s1], tm - 1, 0), 0.0)
    _mix_write(hm[:, s1], right, mu_ref, o_refs, s1)
    s2 = slice(2 * q, 3 * q)
    up = jnp.concatenate([hp[:, s2], hm[:tm - GRID_W, s2]], axis=0) if tm > GRID_W else hp[:, s2]
    up = jnp.where(t >= GRID_W, up, 0.0)
    _mix_write(hm[:, s2], up, mu_ref, o_refs, s2)
    s3 = slice(3 * q, D)
    down = jnp.concatenate([hm[GRID_W:, s3], hn[:, s3]], axis=0) if tm > GRID_W else hn[:, s3]
    down = jnp.where(t < T - GRID_W, down, 0.0)
    _mix_write(hm[:, s3], down, mu_ref, o_refs, s3)


def _mix_ctx_kernel(x_ref, mod_ref, mu_ref, *o_refs):
    Tc, D = x_ref.shape
    hh = D // 2
    h = _normmod(x_ref[...], mod_ref[0:1], mod_ref[1:2], mod_ref[2:3])
    row = lax.broadcasted_iota(jnp.int32, (Tc, 1), 0)
    s0 = slice(0, hh)
    prev = jnp.where(row > 0, pltpu.roll(h[:, s0], 1, 0), 0.0)
    _mix_write(h[:, s0], prev, mu_ref, o_refs, s0)
    s1 = slice(hh, D)
    nxt = jnp.where(row < Tc - 1, pltpu.roll(h[:, s1], Tc - 1, 0), 0.0)
    _mix_write(h[:, s1], nxt, mu_ref, o_refs, s1)


def rwkv_shiftmix(x, mod, mu, li, grid_tokens):
    M, D = x.shape
    n_out = mu.shape[1]
    mu_spec_args = ((None, n_out, D),)
    out_shape = [jax.ShapeDtypeStruct((M, D), BF16)] * n_out
    if grid_tokens:
        tm = _tile(M, 256)
        assert tm % GRID_W == 0 and M % GRID_W == 0
        r = tm // GRID_W
        nb = M // GRID_W
        return pl.pallas_call(
            functools.partial(_mix_lat_kernel, T=M),
            grid=(M // tm,),
            in_specs=[
                pl.BlockSpec((tm, D), lambda i: (i, 0)),
                pl.BlockSpec((GRID_W, D), lambda i: (jnp.maximum(i * r - 1, 0), 0)),
                pl.BlockSpec((GRID_W, D), lambda i: (jnp.minimum((i + 1) * r, nb - 1), 0)),
                pl.BlockSpec((8, D), lambda i: (0, 0)),
                pl.BlockSpec(*mu_spec_args, lambda i: (li, 0, 0)),
            ],
            out_specs=[pl.BlockSpec((tm, D), lambda i: (i, 0))] * n_out,
            out_shape=out_shape,
            compiler_params=_cparams("parallel"),
            name="rwkv_shiftmix_grid",
        )(x, x, x, mod, mu)
    return pl.pallas_call(
        _mix_ctx_kernel,
        grid=(1,),
        in_specs=[
            pl.BlockSpec((M, D), lambda i: (0, 0)),
            pl.BlockSpec((8, D), lambda i: (0, 0)),
            pl.BlockSpec(*mu_spec_args, lambda i: (li, 0, 0)),
        ],
        out_specs=[pl.BlockSpec((M, D), lambda i: (0, 0))] * n_out,
        out_shape=out_shape,
        compiler_params=_cparams("arbitrary"),
        name="rwkv_shiftmix_seq",
    )(x, mod, mu)


def _lora_kernel(xw_ref, xa_ref, xg_ref, w1_ref, w2_ref, w0_ref, a1_ref, a2_ref, a0_ref, g1_ref, g2_ref,
                 lw_ref, a_ref, g_ref, tw_ref, ta_ref, tg_ref):
    @pl.when(pl.program_id(1) == 0)
    def _():
        def both_directions(x_ref, w_ref):
            rank = w_ref.shape[-1]
            if rank % LANES == 0:
                t = _dot(x_ref[...], jnp.concatenate([w_ref[0], w_ref[1]], axis=1))
                return t[:, :rank], t[:, rank:]
            return _dot(x_ref[...], w_ref[0]), _dot(x_ref[...], w_ref[1])

        for d, (tw, ta) in enumerate(zip(both_directions(xw_ref, w1_ref), both_directions(xa_ref, a1_ref))):
            tw_ref[d] = _bf(jnp.tanh(tw))
            ta_ref[d] = _bf(ta)
        tg_ref[...] = _bf(_sigmoid(_dot(xg_ref[...], g1_ref[...])))

    for d in (0, 1):
        z = w0_ref[d:d + 1] + _dot(tw_ref[d], w2_ref[d])
        lw_ref[d] = -math.exp(-0.5) * _sigmoid(z)
        a_ref[d] = _sigmoid(a0_ref[d:d + 1] + _dot(ta_ref[d], a2_ref[d]))
    g_ref[...] = _dot(tg_ref[...], g2_ref[...])


def rwkv_lora(xw, xa, xg, w1, w2, w0, a1, a2, a0, g1, g2, li):
    M, D = xw.shape
    RW, RA, RG = w1.shape[-1], a1.shape[-1], g1.shape[-1]
    tm = _tile(M, 256)
    tn = _tile(D, 1024)
    row = pl.BlockSpec((tm, D), lambda i, j: (i, 0))
    out2 = pl.BlockSpec((2, tm, tn), lambda i, j: (0, i, j))
    return pl.pallas_call(
        _lora_kernel,
        grid=(M // tm, D // tn),
        in_specs=[
            row, row, row,
            pl.BlockSpec((None, 2, D, RW), lambda i, j: (li, 0, 0, 0)),
            pl.BlockSpec((None, 2, RW, tn), lambda i, j: (li, 0, 0, j)),
            pl.BlockSpec((None, 2, tn), lambda i, j: (li, 0, j)),
            pl.BlockSpec((None, 2, D, RA), lambda i, j: (li, 0, 0, 0)),
            pl.BlockSpec((None, 2, RA, tn), lambda i, j: (li, 0, 0, j)),
            pl.BlockSpec((None, 2, tn), lambda i, j: (li, 0, j)),
            pl.BlockSpec((None, D, RG), lambda i, j: (li, 0, 0)),
            pl.BlockSpec((None, RG, tn), lambda i, j: (li, 0, j)),
        ],
        out_specs=[out2, out2, pl.BlockSpec((tm, tn), lambda i, j: (i, j))],
        out_shape=[jax.ShapeDtypeStruct((2, M, D), F32), jax.ShapeDtypeStruct((2, M, D), F32),
                   jax.ShapeDtypeStruct((M, D), F32)],
        scratch_shapes=[pltpu.VMEM((2, tm, RW), BF16), pltpu.VMEM((2, tm, RA), BF16), pltpu.VMEM((tm, RG), BF16)],
        compiler_params=_cparams("parallel", "arbitrary"),
        name="rwkv_lora",
    )(xw, xa, xg, w1, w2, w0, a1, a2, a0, g1, g2)


def _pair_masks():
    row = lax.broadcasted_iota(jnp.int32, (LANES, LANES), 0)
    col = lax.broadcasted_iota(jnp.int32, (LANES, LANES), 1)
    same_head = (row < WKV_HEAD) == (col < WKV_HEAD)
    return row, col, same_head


def _wkv_kernel(r_ref, k_ref, v_ref, lw_ref, a_ref, par_ref, s0_ref, y_ref, z_ref, s_ref,
                st_ref, rh_ref, y0_ref, g_ref, j_ref, et_ref, *, rev):
    C = CHUNK
    Tt = r_ref.shape[0]
    nchunks = Tt // C

    @pl.when(pl.program_id(1) == 0)
    def _():
        st_ref[...] = s0_ref[...]

    k_k, k_a, r_k = par_ref[0:1], par_ref[1:2], par_ref[2:3]
    row, col, same_head = _pair_masks()
    ones_head = jnp.where(same_head, 1.0, 0.0).astype(BF16)
    tr, tc = row & (C - 1), col & (C - 1)
    before = (tc > tr) if rev else (tc < tr)
    before_eq = (tc >= tr) if rev else (tc <= tr)
    r64 = lax.broadcasted_iota(jnp.int32, (C, C), 0)
    c64 = lax.broadcasted_iota(jnp.int32, (C, C), 1)
    cum_m = jnp.where((c64 >= r64) if rev else (c64 <= r64), 1.0, 0.0).astype(BF16)
    eye = jnp.where(row == col, 1.0, 0.0)
    lane = lax.broadcasted_iota(jnp.int32, (C, LANES), 1)
    head_a = lane < WKV_HEAD

    def stack2f(x):
        return jnp.concatenate([jnp.where(head_a, x, 0.0), jnp.where(head_a, 0.0, x)], axis=0)

    def unstack(x2):
        return x2[0:C] + x2[C:2 * C]

    blk16 = (row >> 4) == (col >> 4)
    blk32 = (row >> 5) == (col >> 5)

    def each(f, *lists):
        return [f(*xs) for xs in zip(*lists)]

    def square(m):
        return each(lambda x: _dot(_bf(x), _bf(x)), m)

    def times_one_plus(t, m):
        return each(lambda x, y: x + _dot(_bf(x), _bf(y)), t, m)

    def merge(t, a_kb, inside, outside):
        sel = jnp.logical_and(inside, jnp.logical_not(outside))
        tb = each(_bf, t)
        lt = each(lambda x, y: _bf(_dot(_bf(jnp.where(sel, x, 0.0)), y)), a_kb, tb)
        return each(lambda x, y, w: x - _dot(y, w), t, tb, lt)

    sls = [pl.ds(c * C, C) for c in range(nchunks)]
    r, k, v, lw, a = ([ref[sl, :] for sl in sls] for ref in (r_ref, k_ref, v_ref, lw_ref, a_ref))
    kk = each(lambda x: x * k_k, k)
    nrm = each(lambda x: _exact_right(x * x, ones_head, 2), kk)
    kk = each(lambda x, y: x * lax.rsqrt(jnp.maximum(y, 1e-24)), kk, nrm)
    kd = each(lambda x, y: x * (1.0 + (y - 1.0) * k_a), k, a)
    beta = each(lambda x, y: x * y, a, kk)
    bonus = each(lambda x, y: _exact_right(x * y * r_k, ones_head, 2), r, kd)
    for sl, x, y in zip(sls, bonus, v):
        z_ref[sl, :] = x * y
    b = each(lambda x: _exact_left(cum_m, x, 3), lw)
    b_tot = each(lambda x: x[0:1] if rev else x[C - 1:C], b)
    e_b = each(jnp.exp, b)
    e_nb = each(lambda x: jnp.exp(-x), b)
    e_rest = each(lambda x, y: jnp.exp(y - x), b, b_tot)
    r2f = each(lambda x, y: stack2f(x * y), r, e_b)
    r2 = each(_bf, r2f)
    kap2 = each(lambda x, y, w: _bf(stack2f(x * jnp.exp(y - w))), kk, b, lw)
    k2 = each(lambda x, y: _bf(stack2f(x * y)), kd, e_nb)
    be2 = each(lambda x, y: _bf(stack2f(x * y)), beta, e_nb)
    v2 = each(lambda x: _bf(stack2f(x)), v)
    kc2 = each(lambda x, y: _bf(stack2f(x * y)), kd, e_rest)
    bc2 = each(lambda x, y: _bf(stack2f(x * y)), beta, e_rest)
    kb2 = each(lambda x, y: jnp.concatenate([x, y], axis=0), k2, be2)
    g_kap = each(_dot_nt, kap2, kb2)
    a_kk = each(lambda x: _bf(jnp.where(before, x[:, :LANES], 0.0)), g_kap)
    a_kb = each(lambda x: jnp.where(before, x[:, LANES:], 0.0), g_kap)
    m1 = each(lambda x: jnp.where(blk16, -x, 0.0), a_kb)
    m2 = square(m1)
    g_r = each(_dot_nt, r2, kb2)
    m4 = square(m2)
    t = times_one_plus(each(lambda x: eye + x, m1), m2)
    a_rk = each(lambda x: _bf(jnp.where(before_eq, x[:, :LANES], 0.0)), g_r)
    a_rb = each(lambda x: _bf(jnp.where(before_eq, x[:, LANES:], 0.0)), g_r)
    m8 = square(m4)
    t = times_one_plus(t, m4)
    w0 = each(lambda x, y: _bf(_dot(x, y)), a_kk, v2)
    t = times_one_plus(t, m8)
    t = merge(t, a_kb, blk32, blk16)
    tinv = each(_bf, merge(t, a_kb, same_head, blk32))
    khu = each(lambda x, y, w: _bf(_dot(x, jnp.concatenate([y, w], axis=1))), tinv, kap2, w0)
    kh = each(lambda x: x[:, :LANES], khu)
    u0 = each(lambda x: x[:, LANES:], khu)
    vu = each(lambda x, y: jnp.concatenate([x, y], axis=0), v2, u0)
    rh = each(lambda x, y, w: _bf(x - _dot(y, w)), r2f, a_rb, kh)
    y0 = each(lambda x, y, w: _dot(jnp.concatenate([x, -y], axis=1), w), a_rk, a_rb, vu)
    jj = each(lambda x, y, w: _dot_tn(x, jnp.concatenate([y, -w], axis=0)), vu, kc2, bc2)
    gg = each(lambda x, y: _bf(-_dot_tn(x, y)), kh, bc2)
    for c in range(nchunks):
        rh_ref[c] = rh[c]
        y0_ref[c] = y0[c]
        j_ref[c] = jj[c]
        g_ref[c] = gg[c]
        et_ref[c] = jnp.broadcast_to(jnp.exp(b_tot[c]), (8, LANES))

    S = st_ref[...]
    for c in (range(nchunks - 1, -1, -1) if rev else range(nchunks)):
        Sb = _bf(S)
        y_ref[pl.ds(c * C, C), :] = unstack(_dot_nt(rh_ref[c], Sb) + y0_ref[c])
        S = S * et_ref[c, 0:1, :] + _dot(Sb, g_ref[c]) + j_ref[c]
    st_ref[...] = S

    @pl.when(pl.program_id(1) == pl.num_programs(1) - 1)
    def _():
        s_ref[...] = S


def wkv_scan(r, k, v, lw, a, par, s0, d):
    T, D = r.shape
    HP = D // LANES
    Tt = _tile(T, 1024)
    NT = T // Tt
    nck = Tt // CHUNK
    rev = d == 1

    def tt(t):
        return NT - 1 - t if rev else t

    tok = pl.BlockSpec((Tt, LANES), lambda h, t: (tt(t), h))
    tok_d = pl.BlockSpec((None, Tt, LANES), lambda h, t: (d, tt(t), h))
    st = pl.BlockSpec((None, LANES, LANES), lambda h, t: (h, 0, 0))
    return pl.pallas_call(
        functools.partial(_wkv_kernel, rev=rev),
        grid=(HP, NT),
        in_specs=[tok, tok, tok, tok_d, tok_d, pl.BlockSpec((8, LANES), lambda h, t: (0, h)), st],
        out_specs=[tok, tok, st],
        out_shape=[jax.ShapeDtypeStruct((T, D), F32), jax.ShapeDtypeStruct((T, D), F32),
                   jax.ShapeDtypeStruct((HP, LANES, LANES), F32)],
        scratch_shapes=[
            pltpu.VMEM((LANES, LANES), F32),
            pltpu.VMEM((nck, LANES, LANES), BF16),
            pltpu.VMEM((nck, LANES, LANES), F32),
            pltpu.VMEM((nck, LANES, LANES), BF16),
            pltpu.VMEM((nck, LANES, LANES), F32),
            pltpu.VMEM((nck, 8, LANES), F32),
        ],
        compiler_params=_cparams("parallel", "arbitrary"),
        name="wkv_scan",
    )(r, k, v, lw, a, par, s0)


def _rwkv_combine_kernel(y0_ref, y1_ref, z0_ref, z1_ref, g_ref, ln_ref, o_ref):
    D = o_ref.shape[1]
    row = lax.broadcasted_iota(jnp.int32, (LANES, LANES), 0)
    col = lax.broadcasted_iota(jnp.int32, (LANES, LANES), 1)
    mean_m = jnp.where((row < WKV_HEAD) == (col < WKV_HEAD), 1.0 / WKV_HEAD, 0.0).astype(BF16)
    for c in range(D // LANES):
        cs = slice(c * LANES, (c + 1) * LANES)
        y = y0_ref[:, cs] + y1_ref[:, cs]
        yc = y - _exact_right(y, mean_m, 2)
        var = _exact_right(yc * yc, mean_m, 2)
        out = yc * lax.rsqrt(var + GN_EPS) * ln_ref[0:1, cs] + ln_ref[1:2, cs] + (z0_ref[:, cs] + z1_ref[:, cs])
        o_ref[:, cs] = _bf(out * g_ref[:, cs])


def rwkv_combine(y0, y1, z0, z1, g, ln):
    M, D = g.shape
    tm = _tile(M, 256)
    tok = pl.BlockSpec((tm, D), lambda i: (i, 0))
    return pl.pallas_call(
        _rwkv_combine_kernel,
        grid=(M // tm,),
        in_specs=[tok] * 5 + [pl.BlockSpec((8, D), lambda i: (0, 0))],
        out_specs=tok,
        out_shape=jax.ShapeDtypeStruct((M, D), BF16),
        compiler_params=_cparams("parallel"),
        name="rwkv_combine",
    )(y0, y1, z0, z1, g, ln)


def _gla_kernel(q_ref, i_ref, f_ref, lbl_ref, s0_ref, o_ref, s_ref, st_ref, qe_ref, j_ref, et_ref, *, rev, layer):
    C = CHUNK
    Tt = q_ref.shape[0]
    nchunks = Tt // C
    nsub = C // SUB

    @pl.when(pl.program_id(1) == 0)
    def _():
        st_ref[...] = s0_ref[...]

    logits = lbl_ref[...]
    e = jnp.exp(logits - jnp.max(logits, axis=0, keepdims=True))
    p = e / jnp.sum(e, axis=0, keepdims=True)
    lb = jnp.zeros((1, LANES), F32)
    for l in range(1, layer + 1):
        lb = lb + p[l:l + 1]
    log_lb = jnp.log(lb)
    log_1m = jnp.log1p(-lb)

    r64 = lax.broadcasted_iota(jnp.int32, (C, C), 0)
    c64 = lax.broadcasted_iota(jnp.int32, (C, C), 1)
    cum_m = jnp.where((c64 >= r64) if rev else (c64 <= r64), 1.0, 0.0).astype(BF16)
    rows = lax.broadcasted_iota(jnp.int32, (C, 1), 0)
    lane_s = lax.broadcasted_iota(jnp.int32, (SUB, C), 1)
    row_s = lax.broadcasted_iota(jnp.int32, (SUB, 1), 0)

    def each(f, *lists):
        return [f(*xs) for xs in zip(*lists)]

    sls = [pl.ds(c * C, C) for c in range(nchunks)]
    q = [_silu(q_ref[sl, :]) for sl in sls]
    v = [i_ref[sl, :] for sl in sls]
    f = [f_ref[sl, :] for sl in sls]
    kg = each(lambda x: (1.0 - lb) * _sigmoid(-x), f)
    x2 = each(lambda x: log_1m - _softplus(-x), f)
    g = each(lambda x: jnp.maximum(log_lb, x) + jnp.log1p(jnp.exp(-jnp.abs(log_lb - x))), x2)
    b = each(lambda x: _exact_left(cum_m, x, 3), g)
    b_tot = each(lambda x: x[0:1] if rev else x[C - 1:C], b)
    vb = each(_bf, v)
    att_rows = [[] for _ in range(nchunks)]
    for I in range(nsub):
        lo = I * SUB
        qI = each(lambda x: x[lo:lo + SUB], q)
        bI = each(lambda x: x[lo:lo + SUB], b)
        first = (I == nsub - 1) if rev else (I == 0)
        if first:
            att = [jnp.zeros((SUB, C), F32)] * nchunks
        else:
            ref = each(lambda x: x[lo + SUB:lo + SUB + 1] if rev else x[lo - 1:lo], b)
            earlier = (rows >= lo + SUB) if rev else (rows < lo)
            kt = each(lambda x, y, w: _bf(jnp.where(earlier, x * jnp.exp(w - y), 0.0)), kg, b, ref)
            qt = each(lambda x, y, w: _bf(x * jnp.exp(y - w)), qI, bI, ref)
            att = each(_dot_nt, qt, kt)
        for j in range(SUB):
            s = lo + j
            place = jnp.logical_and(lane_s == s, (row_s <= j) if rev else (row_s >= j))
            pj = each(lambda x, y, w, u: x * y[s:s + 1] * jnp.exp(w - u[s:s + 1]), qI, kg, bI, b)
            att = each(lambda x, y: x + jnp.where(place, jnp.sum(y, axis=-1, keepdims=True), 0.0), att, pj)
        for c in range(nchunks):
            att_rows[c].append(att[c])
    o_in = each(lambda x, y: _dot(_bf(jnp.concatenate(x, axis=0)), y), att_rows, vb)
    for c in range(nchunks):
        o_ref[sls[c], :] = o_in[c]
        qe_ref[c] = _bf(q[c] * jnp.exp(b[c]))
        j_ref[c] = _dot_tn(vb[c], _bf(kg[c] * jnp.exp(b_tot[c] - b[c])))
        et_ref[c] = jnp.broadcast_to(jnp.exp(b_tot[c]), (8, LANES))

    Z = st_ref[...]
    for c in (range(nchunks - 1, -1, -1) if rev else range(nchunks)):
        o_ref[sls[c], :] += _dot_nt(qe_ref[c], _bf(Z))
        Z = Z * et_ref[c, 0:1, :] + j_ref[c]
    st_ref[...] = Z

    @pl.when(pl.program_id(1) == pl.num_programs(1) - 1)
    def _():
        s_ref[...] = Z


def gla_scan(raw, lb_logits, s0, d, layer):
    T = raw.shape[0]
    D = raw.shape[1] // 5
    H = D // LANES
    Tt = _tile(T, 1024)
    NT = T // Tt
    nck = Tt // CHUNK
    rev = d == 1
    L = lb_logits.shape[0]

    def tt(t):
        return NT - 1 - t if rev else t

    st = pl.BlockSpec((None, LANES, LANES), lambda h, t: (h, 0, 0))
    return pl.pallas_call(
        functools.partial(_gla_kernel, rev=rev, layer=layer),
        grid=(H, NT),
        in_specs=[
            pl.BlockSpec((Tt, LANES), lambda h, t: (tt(t), h)),
            pl.BlockSpec((Tt, LANES), lambda h, t: (tt(t), H + h)),
            pl.BlockSpec((Tt, LANES), lambda h, t: (tt(t), (2 + d) * H + h)),
            pl.BlockSpec((L, LANES), lambda h, t: (0, h)),
            st,
        ],
        out_specs=[pl.BlockSpec((Tt, LANES), lambda h, t: (tt(t), h)), st],
        out_shape=[jax.ShapeDtypeStruct((T, D), F32), jax.ShapeDtypeStruct((H, LANES, LANES), F32)],
        scratch_shapes=[
            pltpu.VMEM((LANES, LANES), F32),
            pltpu.VMEM((nck, CHUNK, LANES), BF16),
            pltpu.VMEM((nck, LANES, LANES), F32),
            pltpu.VMEM((nck, 8, LANES), F32),
        ],
        compiler_params=_cparams("parallel", "arbitrary"),
        name="gla_scan",
    )(raw, raw, raw, lb_logits, s0)


def _hgrn_combine_kernel(o0_ref, o1_ref, g_ref, gn_ref, o_ref):
    D = o_ref.shape[1]
    for c in range(D // LANES):
        cs = slice(c * LANES, (c + 1) * LANES)
        o = o0_ref[:, cs] + o1_ref[:, cs]
        o = o * lax.rsqrt(jnp.mean(o * o, axis=-1, keepdims=True) + EPS) * gn_ref[0:1, cs]
        o_ref[:, cs] = _bf(o * _silu(g_ref[:, cs]))


def hgrn_combine(o0, o1, raw, gn):
    M, D = o0.shape
    tm = _tile(M, 256)
    tok = pl.BlockSpec((tm, D), lambda i: (i, 0))
    return pl.pallas_call(
        _hgrn_combine_kernel,
        grid=(M // tm,),
        in_specs=[tok, tok, pl.BlockSpec((tm, D), lambda i: (i, 4)), pl.BlockSpec((1, D), lambda i: (0, 0))],
        out_specs=tok,
        out_shape=jax.ShapeDtypeStruct((M, D), BF16),
        compiler_params=_cparams("parallel"),
        name="hgrn_combine",
    )(o0, o1, raw, gn)


def _rows8(*vecs):
    D = vecs[0].shape[-1]
    rows = [v.reshape(1, D) for v in vecs]
    rows.append(jnp.zeros((8 - len(rows), D), F32))
    return jnp.concatenate(rows, axis=0)


def kernel(x, c, ctx, c_ctx, ada_down, ada_up, ada_b, norm_g, ffn_w13, ffn_w2, final_g, lru_w_in, lru_conv_w, lru_conv_b, lru_gate_a_w, lru_gate_a_b, lru_gate_x_w, lru_gate_x_b, lru_lam, lru_w_out, rwkv_mu, rwkv_w_r, rwkv_w_k, rwkv_w_v, rwkv_w_o, rwkv_w0, rwkv_w1, rwkv_w2, rwkv_a0, rwkv_a1, rwkv_a2, rwkv_g1, rwkv_g2, rwkv_k_k, rwkv_k_a, rwkv_r_k, rwkv_ln_w, rwkv_ln_b, hgrn_w_in, hgrn_lb_logits, hgrn_gn_g, hgrn_w_out):
    B, T, D = x.shape
    assert B == 1, "one sequence per call"
    depth = ada_down.shape[0]
    xl, xc = x[0], ctx[0]

    mods = ada_all_layers(_rows8(c[0], c_ctx), ada_down, ada_up, ada_b)
    mods = mods[:, :2].reshape(depth, 2, N_MOD, D)

    lru_wa, lru_wx = _bf(lru_gate_a_w), _bf(lru_gate_x_w)
    lw1, lw2, la1, la2 = _bf(rwkv_w1), _bf(rwkv_w2), _bf(rwkv_a1), _bf(rwkv_a2)
    lg1, lg2 = _bf(rwkv_g1), _bf(rwkv_g2)

    for i in range(depth):
        need_ctx = i < depth - 1
        kind, j = i % 3, i // 3
        ml, mc = mods[i, 0], mods[i, 1]

        def ffn_mod(m, k, which):
            return _rows8(norm_g[i, 2 * which], m[k], m[k + 1], m[k + 2], final_g)

        xc, *wb = half_ffn(xc, ffn_mod(mc, 0, 0), (ffn_w13, ffn_w2), (i, 0))
        xl = half_ffn(xl, ffn_mod(ml, 0, 0), wb)
        mod_l = _rows8(norm_g[i, 1], ml[3], ml[4])
        mod_c = _rows8(norm_g[i, 1], mc[3], mc[4])
        gate_l, gate_c = _rows8(ml[5]), _rows8(mc[5])

        if kind == 0:
            p_c, w_in = proj(xc, lru_w_in, (j,), mod=mod_c, gelu_cols=D, emit_weight=True)
            p_l = proj(xl, w_in, mod=mod_l, gelu_cols=D)
            hs = lru_scan(p_l, p_c, lru_conv_w, lru_conv_b, lru_wa, lru_wx, lru_gate_a_b, lru_gate_x_b, lru_lam,
                          j, need_ctx)
            a_l = lru_combine(p_l, hs[0])
            a_c = lru_combine(p_c, hs[1]) if need_ctx else None
            w_last = lru_w_out
        elif kind == 1:
            par = _rows8(rwkv_k_k[j], rwkv_k_a[j], rwkv_r_k[j].reshape(D))
            ln = _rows8(rwkv_ln_w[j], rwkv_ln_b[j])
            outs = []
            state = [jnp.zeros((D // LANES, LANES, LANES), F32)] * 2
            for xs, mod, is_grid in ((xc, mod_c, False), (xl, mod_l, True)):
                xr, xw, xk, xv, xa, xg = rwkv_shiftmix(xs, mod, rwkv_mu, j, is_grid)
                if not is_grid:
                    r, w_r = proj(xr, rwkv_w_r, (j,), emit_weight=True)
                    k, w_k = proj(xk, rwkv_w_k, (j,), emit_weight=True)
                    v, w_v = proj(xv, rwkv_w_v, (j,), emit_weight=True)
                else:
                    r, k, v = proj(xr, w_r), proj(xk, w_k), proj(xv, w_v)
                lw, a, g = rwkv_lora(xw, xa, xg, lw1, lw2, rwkv_w0, la1, la2, rwkv_a0, lg1, lg2, j)
                ys, zs = [], []
                for d in (0, 1):
                    y, z, state[d] = wkv_scan(r, k, v, lw, a, par, state[d], d)
                    ys.append(y)
                    zs.append(z)
                outs.append(rwkv_combine(ys[0], ys[1], zs[0], zs[1], g, ln))
            a_c, a_l = outs
            w_last = rwkv_w_o
        else:
            outs = []
            state = [jnp.zeros((D // LANES, LANES, LANES), F32)] * 2
            for xs, mod, is_ctx in ((xc, mod_c, True), (xl, mod_l, False)):
                if is_ctx:
                    raw, w_in = proj(xs, hgrn_w_in, (j,), mod=mod, emit_weight=True)
                else:
                    raw = proj(xs, w_in, mod=mod)
                os_ = []
                for d in (0, 1):
                    o, state[d] = gla_scan(raw, hgrn_lb_logits, state[d], d, i)
                    os_.append(o)
                outs.append(hgrn_combine(os_[0], os_[1], raw, hgrn_gn_g[j].reshape(1, D)))
            a_c, a_l = outs
            w_last = hgrn_w_out

        if need_ctx:
            xc, w_out = proj(a_c, w_last, (j,), res=xc, gate=gate_c, emit_weight=True)
            xc, *wb = half_ffn(xc, ffn_mod(mc, 6, 1), (ffn_w13, ffn_w2), (i, 1))
        else:
            w_out = _bf(w_last[j])
            wb = round_ffn_weights(ffn_w13, ffn_w2, (i, 1))
        xl = proj(a_l, w_out, res=xl, gate=gate_l)
        xl = half_ffn(xl, ffn_mod(ml, 6, 1), wb, final_norm=not need_ctx)

    return xl[None]
```
